```python
import jax, jax.numpy as jnp
from jax import lax
import numpy as np

D_MODEL = 1024
BATCH = 16
SEQ = 2048
DEPTH = 2
DEC_BATCH = 32
DEC_SEQ = 64
PAST_LEN = 4096

CHUNK = 64
N_EVEN = (DEPTH + 1) // 2
N_ODD = DEPTH // 2
EPS = 1e-6
A_HEADS = 16
A_KV_HEADS = 4
A_HEAD_DIM = 64
A_GROUP = A_HEADS // A_KV_HEADS
A_WIDTH = A_HEADS * A_HEAD_DIM
WINDOW = 128
WIN_CHUNKS = -(-WINDOW // CHUNK)
WIN_ROWS = WIN_CHUNKS * CHUNK
B_HEADS = 4
B_DK = 128
B_DV = 256
B_WIDTH = B_HEADS * B_DV
B_LOWRANK = 16
B_GATE_NORM = 16.0
C_WIDTH = 1536
C_BLOCKS = 8
C_BLOCK = C_WIDTH // C_BLOCKS
CONV_W = 4
LRU_C = 8.0
MIX_EVEN = A_WIDTH + B_WIDTH
SPLIT_EVEN = (A_WIDTH, A_KV_HEADS * A_HEAD_DIM, A_KV_HEADS * A_HEAD_DIM,
              B_HEADS * B_DK, B_HEADS * B_DK, B_WIDTH, B_LOWRANK, MIX_EVEN)
IN_EVEN = sum(SPLIT_EVEN)
IN_ODD = 2 * C_WIDTH

kernel_name = "hybrid_swa_gla_rglru_stream_step"


def _rmsnorm(x, g):
    xf = x.astype(jnp.float32)
    y = xf * lax.rsqrt(jnp.mean(xf * xf, axis=-1, keepdims=True) + EPS) * g.astype(jnp.float32)
    return y.astype(x.dtype)


def _split(z, sizes):
    offs = [int(o) for o in np.cumsum(sizes)[:-1]]
    return jnp.split(z, offs, axis=-1)


def _alibi_slopes():
    return 2.0 ** (-8.0 * jnp.arange(1, A_HEADS + 1, dtype=jnp.float32) / A_HEADS)


def _band(t):
    b, tl = t.shape[:2]
    nc = tl // CHUNK
    tp = jnp.pad(t, ((0, 0), (WIN_ROWS, 0), (0, 0), (0, 0)))
    tp = tp.reshape(b, nc + WIN_CHUNKS, CHUNK, t.shape[2], t.shape[3])
    return jnp.concatenate([tp[:, w:w + nc] for w in range(WIN_CHUNKS + 1)], axis=2)


def _sink_alibi_attention(qb, kb, vb, key_ok, sinks):
    lq, lk = qb.shape[2], kb.shape[2]
    s = jnp.einsum('bnikgd,bnjkd->bnkgij', qb, kb).astype(jnp.float32) * (A_HEAD_DIM ** -0.5)
    dist = jnp.abs(WIN_ROWS + jnp.arange(lq)[:, None] - jnp.arange(lk)[None, :]).astype(jnp.float32)
    slopes = _alibi_slopes().reshape(A_KV_HEADS, A_GROUP)
    s = s - slopes[:, :, None, None] * dist
    s = jnp.where(key_ok[None, :, None, None, None, :], s, jnp.float32(-1e30))
    sk = jnp.broadcast_to(sinks.astype(jnp.float32).reshape(A_KV_HEADS, A_GROUP)[:, :, None, None],
                          s.shape[:-1] + (1,))
    p = jax.nn.softmax(jnp.concatenate([s, sk], axis=-1), axis=-1)[..., :-1]
    o = jnp.einsum('bnkgij,bnjkd->bnikgd', p.astype(vb.dtype), vb)
    return o


def _gla(q, k, v, g, s0, blk):
    b, tl, h, dk = q.shape
    dv = v.shape[-1]
    n = tl // blk
    q = q.astype(jnp.float32).reshape(b, n, blk, h, dk)
    k = k.astype(jnp.float32).reshape(b, n, blk, h, dk)
    v = v.astype(jnp.float32).reshape(b, n, blk, h, dv)
    G = jnp.cumsum(g.reshape(b, n, blk, h, dk), axis=2)
    g_last = G[:, :, -1]
    qg = q * jnp.exp(G)
    kg = k * jnp.exp(-G)
    a = jnp.einsum('bnihd,bnjhd->bnhij', qg, kg)
    a = jnp.where(jnp.tril(jnp.ones((blk, blk), bool)), a, 0.0)
    o = jnp.einsum('bnhij,bnjhe->bnihe', a, v)
    kd = k * jnp.exp(g_last[:, :, None] - G)
    ds = jnp.einsum('bnjhd,bnjhe->bnhde', kd, v)

    def step(S, inp):
        dec, d = inp
        return jnp.exp(dec)[..., None] * S + d, S

    s_fin, s_in = lax.scan(step, s0, (jnp.moveaxis(g_last, 1, 0), jnp.moveaxis(ds, 1, 0)))
    o = o + jnp.einsum('bnihd,bnhde->bnihe', qg, jnp.moveaxis(s_in, 0, 1))
    return o.reshape(b, tl, h, dv), s_fin


def _even_layer(x, past, norm_g, w_in, w_lr, b_lr, sinks, gla_g, w_out):
    b, tl, _ = x.shape
    h = _rmsnorm(x, norm_g)
    z = h @ w_in
    aq, ak, av, bq, bk, bv, blr, gate = _split(z, SPLIT_EVEN)
    aq = aq.reshape(b, tl, A_KV_HEADS, A_GROUP, A_HEAD_DIM)
    ak = ak.reshape(b, tl, A_KV_HEADS, A_HEAD_DIM)
    av = av.reshape(b, tl, A_KV_HEADS, A_HEAD_DIM)
    if past is None:
        nc = tl // CHUNK
        qb = aq.reshape(b, nc, CHUNK, A_KV_HEADS, A_GROUP, A_HEAD_DIM)
        kb, vb = _band(ak), _band(av)
        key_pos = (jnp.arange(nc)[:, None] - WIN_CHUNKS) * CHUNK + jnp.arange(WIN_ROWS + CHUNK)[None, :]
        key_ok = key_pos >= 0
        new_k, new_v = ak[:, -WIN_ROWS:], av[:, -WIN_ROWS:]
        s0 = jnp.zeros((b, B_HEADS, B_DK, B_DV), jnp.float32)
        blk = CHUNK
    else:
        ck, cv, s0 = past
        qb = aq[:, None]
        kb = jnp.concatenate([ck.astype(ak.dtype), ak], axis=1)[:, None]
        vb = jnp.concatenate([cv.astype(av.dtype), av], axis=1)[:, None]
        key_ok = jnp.ones((1, WIN_ROWS + tl), bool)
        new_k, new_v = ak, av
        blk = tl
    ao = _sink_alibi_attention(qb, kb, vb, key_ok, sinks).reshape(b, tl, A_WIDTH)
    glog = jax.nn.log_sigmoid((blr @ w_lr + b_lr).astype(jnp.float32)) / B_GATE_NORM
    bo, s_new = _gla(bq.reshape(b, tl, B_HEADS, B_DK) * (B_DK ** -0.5),
                     bk.reshape(b, tl, B_HEADS, B_DK),
                     bv.reshape(b, tl, B_HEADS, B_DV),
                     glog.reshape(b, tl, B_HEADS, B_DK),
                     s0.astype(jnp.float32), blk)
    bo = _rmsnorm(bo, gla_g).reshape(b, tl, B_WIDTH).astype(x.dtype)
    y = jnp.concatenate([ao.astype(x.dtype), bo], axis=-1) * jax.nn.silu(gate)
    return x + y @ w_out, new_k, new_v, s_new


def _blockdiag(u, w, bias):
    b, tl = u.shape[:2]
    y = jnp.einsum('btnc,ncd->btnd', u.reshape(b, tl, C_BLOCKS, C_BLOCK), w)
    return y.reshape(b, tl, C_WIDTH) + bias


def _lin_comb(l, r):
    a1, b1 = l
    a2, b2 = r
    return a1 * a2, a2 * b1 + b2


def _odd_layer(x, hist, h0, norm_g, w_in, conv_w, conv_b, w_ra, b_ra, w_ri, b_ri, lam, w_out):
    b, tl, _ = x.shape
    h = _rmsnorm(x, norm_g)
    xb, gate = _split(h @ w_in, (C_WIDTH, C_WIDTH))
    xp = jnp.concatenate([hist.astype(xb.dtype), xb], axis=1)
    u = sum(xp[:, j:j + tl] * conv_w[j] for j in range(CONV_W)) + conv_b
    new_conv = xp[:, -(CONV_W - 1):]
    rg = jax.nn.sigmoid(_blockdiag(u, w_ra, b_ra).astype(jnp.float32))
    ig = jax.nn.sigmoid(_blockdiag(u, w_ri, b_ri).astype(jnp.float32))
    log_a = -LRU_C * rg * jax.nn.softplus(-lam.astype(jnp.float32))
    a = jnp.exp(log_a)
    bterm = jnp.sqrt(-jnp.expm1(2.0 * log_a)) * ig * u.astype(jnp.float32)
    bterm = bterm.at[:, 0].add(a[:, 0] * h0.astype(jnp.float32))
    _, hs = lax.associative_scan(_lin_comb, (a, bterm), axis=1)
    y = hs.astype(x.dtype) * jax.nn.silu(gate)
    return x + y @ w_out, new_conv, hs[:, -1]


def setup_inputs(seed: int = 0) -> dict:
    key = jax.random.key(seed)
    ks = jax.random.split(key, 32)
    f32 = jnp.float32
    nrm = lambda k, s, sc: jax.random.normal(k, s, f32) * sc
    u = jax.random.uniform(ks[26], (N_ODD, C_WIDTH), f32, 0.9, 0.999) ** (1.0 / LRU_C)
    return {
        "x_prompt": nrm(ks[0], (BATCH, SEQ, D_MODEL), 1.0),
        "x_sample": nrm(ks[1], (DEC_BATCH, DEC_SEQ, D_MODEL), 1.0),
        "cache_swa_k": nrm(ks[2], (N_EVEN, DEC_BATCH, WIN_ROWS, A_KV_HEADS, A_HEAD_DIM), 1.0),
        "cache_swa_v": nrm(ks[3], (N_EVEN, DEC_BATCH, WIN_ROWS, A_KV_HEADS, A_HEAD_DIM), 1.0),
        "state_gla": nrm(ks[4], (N_EVEN, DEC_BATCH, B_HEADS, B_DK, B_DV), 1.0),
        "cache_conv": nrm(ks[5], (N_ODD, DEC_BATCH, CONV_W - 1, C_WIDTH), 1.0),
        "state_lru": nrm(ks[6], (N_ODD, DEC_BATCH, C_WIDTH), 1.0),
        "norm_even": 1.0 + nrm(ks[7], (N_EVEN, D_MODEL), 0.02),
        "w_in_even": nrm(ks[8], (N_EVEN, D_MODEL, IN_EVEN), D_MODEL ** -0.5),
        "w_gate_lr": nrm(ks[9], (N_EVEN, B_LOWRANK, B_HEADS * B_DK), B_LOWRANK ** -0.5),
        "b_gate_lr": nrm(ks[10], (N_EVEN, B_HEADS * B_DK), 0.1),
        "sinks": nrm(ks[11], (N_EVEN, A_HEADS), 0.5),
        "gla_norm": 1.0 + nrm(ks[12], (N_EVEN, B_DV), 0.02),
        "w_out_even": nrm(ks[13], (N_EVEN, MIX_EVEN, D_MODEL), MIX_EVEN ** -0.5),
        "norm_odd": 1.0 + nrm(ks[14], (N_ODD, D_MODEL), 0.02),
        "w_in_odd": nrm(ks[15], (N_ODD, D_MODEL, IN_ODD), D_MODEL ** -0.5),
        "conv_w": nrm(ks[16], (N_ODD, CONV_W, C_WIDTH), CONV_W ** -0.5),
        "conv_b": nrm(ks[17], (N_ODD, C_WIDTH), 0.02),
        "w_rg_a": nrm(ks[18], (N_ODD, C_BLOCKS, C_BLOCK, C_BLOCK), C_BLOCK ** -0.5),
        "b_rg_a": nrm(ks[19], (N_ODD, C_WIDTH), 0.02),
        "w_rg_i": nrm(ks[20], (N_ODD, C_BLOCKS, C_BLOCK, C_BLOCK), C_BLOCK ** -0.5),
        "b_rg_i": nrm(ks[21], (N_ODD, C_WIDTH), 0.02),
        "lru_lambda": jnp.log(u / (1.0 - u)),
        "w_out_odd": nrm(ks[22], (N_ODD, C_WIDTH, D_MODEL), C_WIDTH ** -0.5),
        "norm_final": 1.0 + nrm(ks[23], (D_MODEL,), 0.02),
    }


def reference(x_prompt, x_sample, cache_swa_k, cache_swa_v, state_gla, cache_conv, state_lru,
              norm_even, w_in_even, w_gate_lr, b_gate_lr, sinks, gla_norm, w_out_even,
              norm_odd, w_in_odd, conv_w, conv_b, w_rg_a, b_rg_a, w_rg_i, b_rg_i, lru_lambda,
              w_out_odd, norm_final):
    xp, xs = x_prompt, x_sample
    pk, pv, pg, pc, pl = [], [], [], [], []
    sk, sv, sg, sc, sl = [], [], [], [], []
    for layer in range(DEPTH):
        if layer % 2 == 0:
            e = layer // 2
            wts = (norm_even[e], w_in_even[e], w_gate_lr[e], b_gate_lr[e], sinks[e], gla_norm[e], w_out_even[e])
            xp, k_, v_, s_ = _even_layer(xp, None, *wts)
            pk.append(k_); pv.append(v_); pg.append(s_)
            xs, k_, v_, s_ = _even_layer(xs, (cache_swa_k[e], cache_swa_v[e], state_gla[e]), *wts)
            sk.append(k_); sv.append(v_); sg.append(s_)
        else:
            o = layer // 2
            wts = (norm_odd[o], w_in_odd[o], conv_w[o], conv_b[o], w_rg_a[o], b_rg_a[o],
                   w_rg_i[o], b_rg_i[o], lru_lambda[o], w_out_odd[o])
            hist0 = jnp.zeros((xp.shape[0], CONV_W - 1, C_WIDTH), xp.dtype)
            h00 = jnp.zeros((xp.shape[0], C_WIDTH), jnp.float32)
            xp, c_, l_ = _odd_layer(xp, hist0, h00, *wts)
            pc.append(c_); pl.append(l_)
            xs, c_, l_ = _odd_layer(xs, cache_conv[o], state_lru[o], *wts)
            sc.append(c_); sl.append(l_)
    y_prompt = _rmsnorm(xp, norm_final)
    y_sample = _rmsnorm(xs, norm_final)
    new_swa_k_prompt = jnp.stack(pk)
    new_swa_v_prompt = jnp.stack(pv)
    new_gla_prompt = jnp.stack(pg)
    new_conv_prompt = jnp.stack(pc)
    new_lru_prompt = jnp.stack(pl)
    new_swa_k_sample = jnp.stack(sk)
    new_swa_v_sample = jnp.stack(sv)
    new_gla_sample = jnp.stack(sg)
    new_conv_sample = jnp.stack(sc)
    new_lru_sample = jnp.stack(sl)
    return (y_prompt, y_sample, new_swa_k_prompt, new_swa_v_prompt, new_gla_prompt, new_conv_prompt,
            new_lru_prompt, new_swa_k_sample, new_swa_v_sample, new_gla_sample, new_conv_sample,
            new_lru_sample)
```

```python
import functools

import jax
import jax.numpy as jnp
import numpy as np
from jax import lax
from jax.experimental import pallas as pl
from jax.experimental.pallas import tpu as pltpu

F32 = jnp.float32
BF16 = jnp.bfloat16

D_MODEL = 1024
CHUNK = 64
EPS = 1e-6
A_HEADS = 16
A_KV_HEADS = 4
A_HEAD_DIM = 64
A_GROUP = A_HEADS // A_KV_HEADS
A_WIDTH = A_HEADS * A_HEAD_DIM
A_KV_WIDTH = A_KV_HEADS * A_HEAD_DIM
WIN_ROWS = 128
WIN_KEYS = WIN_ROWS + CHUNK
B_HEADS = 4
B_DK = 128
B_DV = 256
B_QK_WIDTH = B_HEADS * B_DK
B_WIDTH = B_HEADS * B_DV
B_LOWRANK = 16
B_GATE_NORM = 16.0
C_WIDTH = 1536
C_BLOCKS = 8
C_BLOCK = C_WIDTH // C_BLOCKS
CONV_W = 4
LRU_C = 8.0
MIX_EVEN = A_WIDTH + B_WIDTH

LANES = 128
SUBLANES = 8
MXU_DIM = 256
ROW_TILE = 512
VMEM_LIMIT = 48 * 1024 * 1024

_OFF_Q = 0
_OFF_K = _OFF_Q + A_WIDTH
_OFF_V = _OFF_K + A_KV_WIDTH
_OFF_BQ = _OFF_V + A_KV_WIDTH
_OFF_BK = _OFF_BQ + B_QK_WIDTH
_OFF_BV = _OFF_BK + B_QK_WIDTH
_OFF_GATE = _OFF_BV + B_WIDTH
_MAIN_WIDTH = _OFF_GATE + MIX_EVEN


def _const_spec(shape):
    nd = len(shape)
    return pl.BlockSpec(shape, lambda *_: (0,) * nd, pipeline_mode=pl.Buffered(1))


def _rms_scale(x):
    return x * lax.rsqrt(jnp.mean(x * x, axis=-1, keepdims=True) + EPS)


def _silu(x):
    return x * (1.0 / (1.0 + jnp.exp(-x)))


def _sigmoid(x):
    return 1.0 / (1.0 + jnp.exp(-x))


def _dot(a, b):
    return jnp.dot(a, b, preferred_element_type=F32)


def _dot_nt(a, b):
    return lax.dot_general(a, b, (((1,), (1,)), ((), ())), preferred_element_type=F32)


def _dot_tn(a, b):
    return lax.dot_general(a, b, (((0,), (0,)), ((), ())), preferred_element_type=F32)


def _even_in_kernel(x_ref, g_ref, w_ref, wlr1_ref, wlr2_ref, blr_ref, tri_ref,
                    q_ref, k_ref, v_ref, bq_ref, bk_ref, bv_ref, gcum_ref, gate_ref):
    h = (_rms_scale(x_ref[...]) * g_ref[...]).astype(BF16)

    def proj(off, width):
        return _dot(h, w_ref[:, off:off + width])

    q_ref[...] = proj(_OFF_Q, A_WIDTH).astype(BF16)
    k_ref[...] = proj(_OFF_K, A_KV_WIDTH)
    v_ref[...] = proj(_OFF_V, A_KV_WIDTH)
    bq_ref[...] = proj(_OFF_BQ, B_QK_WIDTH).astype(BF16)
    bk_ref[...] = proj(_OFF_BK, B_QK_WIDTH).astype(BF16)
    bv_ref[...] = proj(_OFF_BV, B_WIDTH).astype(BF16)
    gate_ref[...] = proj(_OFF_GATE, MIX_EVEN).astype(BF16)

    low = _dot(h, wlr1_ref[...])
    pre = _dot(low.astype(BF16), wlr2_ref[...]) + blr_ref[...]
    glog = (jnp.minimum(pre, 0.0) - jnp.log1p(jnp.exp(-jnp.abs(pre)))) * (1.0 / B_GATE_NORM)
    tri = tri_ref[...]
    g1 = glog.astype(BF16)
    r1 = glog - g1.astype(F32)
    g2 = r1.astype(BF16)
    g3 = (r1 - g2.astype(F32)).astype(BF16)
    gcum_ref[...] = _dot(tri, g1) + _dot(tri, g2) + _dot(tri, g3)


def _even_in(x2, norm_g, w_main, w_lr1, w_lr2, b_lr, tri):
    rows = x2.shape[0]
    nsteps = rows // ROW_TILE
    row_spec = lambda w: pl.BlockSpec((ROW_TILE, w), lambda i: (i, 0))
    out_widths = (A_WIDTH, A_KV_WIDTH, A_KV_WIDTH, B_QK_WIDTH, B_QK_WIDTH, B_WIDTH, B_QK_WIDTH, MIX_EVEN)
    out_dtypes = (BF16, F32, F32, BF16, BF16, BF16, F32, BF16)
    return pl.pallas_call(
        _even_in_kernel,
        grid=(nsteps,),
        in_specs=[row_spec(D_MODEL), _const_spec(norm_g.shape), _const_spec(w_main.shape),
                  _const_spec(w_lr1.shape), _const_spec(w_lr2.shape), _const_spec(b_lr.shape),
                  _const_spec(tri.shape)],
        out_specs=[row_spec(w) for w in out_widths],
        out_shape=[jax.ShapeDtypeStruct((rows, w), d) for w, d in zip(out_widths, out_dtypes)],
        compiler_params=pltpu.CompilerParams(dimension_semantics=("arbitrary",),
                                             vmem_limit_bytes=VMEM_LIMIT),
        name="even_in",
    )(x2, norm_g, w_main, w_lr1, w_lr2, b_lr, tri)


def _even_mix_kernel(*refs, bb, tb, nt, has_past):
    if has_past:
        (sinks_ref, q_ref, kc_ref, vc_ref, kp_ref, vp_ref, bq_ref, bk_ref, bv_ref, gcum_ref,
         gate_ref, x_ref, alibi_ref, glag_ref, wout_ref, s0_ref,
         out_ref, sfin_ref, kbuf, vbuf, st_ref, y_ref) = refs
    else:
        (sinks_ref, q_ref, kc_ref, vc_ref, kp_ref, vp_ref, bq_ref, bk_ref, bv_ref, gcum_ref,
         gate_ref, x_ref, alibi_ref, glag_ref, wout_ref,
         out_ref, sfin_ref, kbuf, vbuf, st_ref, y_ref) = refs
        s0_ref = None
    t = pl.program_id(1)
    tc = tb // CHUNK

    for bi in range(bb):
        for kh in range(A_KV_HEADS):
            cols = slice(kh * A_HEAD_DIM, (kh + 1) * A_HEAD_DIM)
            kbuf[bi, kh, 0:WIN_ROWS, :] = kp_ref[bi * WIN_ROWS:(bi + 1) * WIN_ROWS, cols].astype(BF16)
            vbuf[bi, kh, 0:WIN_ROWS, :] = vp_ref[bi * WIN_ROWS:(bi + 1) * WIN_ROWS, cols].astype(BF16)
            kbuf[bi, kh, WIN_ROWS:WIN_ROWS + tb, :] = kc_ref[bi * tb:(bi + 1) * tb, cols].astype(BF16)
            vbuf[bi, kh, WIN_ROWS:WIN_ROWS + tb, :] = vc_ref[bi * tb:(bi + 1) * tb, cols].astype(BF16)

    @pl.when(t == 0)
    def _init_state():
        for bi in range(bb):
            for hh in range(B_HEADS):
                if has_past:
                    st_ref[bi, hh] = s0_ref[bi, hh].T
                else:
                    st_ref[bi, hh] = jnp.zeros((B_DV, B_DK), F32)

    key_idx = lax.broadcasted_iota(jnp.int32, (CHUNK, WIN_KEYS), 1)
    tril = (lax.broadcasted_iota(jnp.int32, (CHUNK, CHUNK), 0)
            >= lax.broadcasted_iota(jnp.int32, (CHUNK, CHUNK), 1))
    glag = glag_ref[...]

    def chunk_body(n, carry):
        bi = n // tc
        c = n % tc
        r0 = pl.multiple_of(n * CHUNK, CHUNK)
        w0 = pl.multiple_of(c * CHUNK, CHUNK)
        rows = pl.ds(r0, CHUNK)

        if not has_past:
            key_ok = key_idx + (t * tc + c - WIN_ROWS // CHUNK) * CHUNK >= 0
        for kh in range(A_KV_HEADS):
            q4 = q_ref[rows, kh * A_GROUP * A_HEAD_DIM:(kh + 1) * A_GROUP * A_HEAD_DIM]
            qs = jnp.concatenate([q4[:, g * A_HEAD_DIM:(g + 1) * A_HEAD_DIM] for g in range(A_GROUP)],
                                 axis=0)
            kw = kbuf[bi, kh, pl.ds(w0, WIN_KEYS), :]
            vw = vbuf[bi, kh, pl.ds(w0, WIN_KEYS), :]
            s = _dot_nt(qs, kw)
            ps, inv = [], []
            for g in range(A_GROUP):
                hd = kh * A_GROUP + g
                sg = s[g * CHUNK:(g + 1) * CHUNK] * (A_HEAD_DIM ** -0.5) - alibi_ref[hd]
                if not has_past:
                    sg = jnp.where(key_ok, sg, -1e30)
                sink = sinks_ref[hd]
                m = jnp.maximum(jnp.max(sg, axis=-1, keepdims=True), sink)
                p = jnp.exp(sg - m)
                den = jnp.sum(p, axis=-1, keepdims=True) + jnp.exp(sink - m)
                ps.append((p * (1.0 / den)).astype(BF16))
            o = _dot(jnp.concatenate(ps, axis=0), vw)
            ao = jnp.concatenate([o[g * CHUNK:(g + 1) * CHUNK] for g in range(A_GROUP)], axis=1)
            cols = slice(kh * A_GROUP * A_HEAD_DIM, (kh + 1) * A_GROUP * A_HEAD_DIM)
            y_ref[rows, cols] = (ao * _silu(gate_ref[rows, cols].astype(F32))).astype(BF16)

        gc = gcum_ref[rows, :]
        gl = gcum_ref[pl.ds(r0 + CHUNK - 1, 1), :]
        bk = bk_ref[rows, :].astype(F32)
        qg = (bq_ref[rows, :].astype(F32) * (B_DK ** -0.5) * jnp.exp(gc)).astype(BF16)
        kg = (bk * jnp.exp(-gc)).astype(BF16)
        kd = (bk * jnp.exp(gl - gc)).astype(BF16)
        dec = jnp.exp(gl)
        for hh in range(B_HEADS):
            ks = slice(hh * B_DK, (hh + 1) * B_DK)
            vs = slice(hh * B_DV, (hh + 1) * B_DV)
            a = jnp.where(tril, _dot_nt(qg[:, ks], kg[:, ks]), 0.0).astype(BF16)
            vh = bv_ref[rows, vs]
            st = st_ref[bi, hh]
            o = _dot(a, vh) + _dot_nt(qg[:, ks], st.astype(BF16))
            st_ref[bi, hh] = st * dec[:, ks] + _dot_tn(vh, kd[:, ks])
            bo = _rms_scale(o) * glag
            cols = slice(A_WIDTH + hh * B_DV, A_WIDTH + (hh + 1) * B_DV)
            y_ref[rows, cols] = (bo * _silu(gate_ref[rows, cols].astype(F32))).astype(BF16)
        return carry

    lax.fori_loop(0, bb * tc, chunk_body, 0)

    out_ref[...] = x_ref[...] + _dot(y_ref[...], wout_ref[...])

    @pl.when(t == nt - 1)
    def _emit_state():
        for bi in range(bb):
            for hh in range(B_HEADS):
                sfin_ref[bi, hh] = st_ref[bi, hh].T


def _even_mix(sinks, q, k, v, kprev, vprev, bq, bk, bv, gcum, gate, x2, alibi, gla_g, w_out, s0,
              *, batch, seq):
    has_past = s0 is not None
    tile = ROW_TILE // 2 if has_past else ROW_TILE
    tb = min(seq, tile)
    bb = tile // tb
    nt = seq // tb
    nb = batch // bb
    assert bb == 1 or nt == 1
    row_spec = lambda w: pl.BlockSpec((tile, w), lambda b, t: (b * nt + t, 0))
    if has_past:
        prev_spec = pl.BlockSpec((bb * WIN_ROWS, A_KV_WIDTH), lambda b, t: (b, 0))
    else:
        per_b = seq // WIN_ROWS
        step = tb // WIN_ROWS
        prev_spec = pl.BlockSpec((WIN_ROWS, A_KV_WIDTH),
                                 lambda b, t: (b * per_b + jnp.maximum(t * step - 1, 0), 0))
    state_spec = pl.BlockSpec((bb, B_HEADS, B_DK, B_DV), lambda b, t: (b, 0, 0, 0))
    in_specs = [pl.BlockSpec(memory_space=pltpu.SMEM),
                row_spec(A_WIDTH), row_spec(A_KV_WIDTH), row_spec(A_KV_WIDTH), prev_spec, prev_spec,
                row_spec(B_QK_WIDTH), row_spec(B_QK_WIDTH), row_spec(B_WIDTH), row_spec(B_QK_WIDTH),
                row_spec(MIX_EVEN), row_spec(D_MODEL),
                _const_spec(alibi.shape), _const_spec(gla_g.shape), _const_spec(w_out.shape)]
    args = [sinks, q, k, v, kprev, vprev, bq, bk, bv, gcum, gate, x2, alibi, gla_g, w_out]
    if has_past:
        in_specs.append(state_spec)
        args.append(s0)
    return pl.pallas_call(
        functools.partial(_even_mix_kernel, bb=bb, tb=tb, nt=nt, has_past=has_past),
        grid=(nb, nt),
        in_specs=in_specs,
        out_specs=[row_spec(D_MODEL), state_spec],
        out_shape=[jax.ShapeDtypeStruct((batch * seq, D_MODEL), F32),
                   jax.ShapeDtypeStruct((batch, B_HEADS, B_DK, B_DV), F32)],
        scratch_shapes=[pltpu.VMEM((bb, A_KV_HEADS, WIN_ROWS + tb, A_HEAD_DIM), BF16),
                        pltpu.VMEM((bb, A_KV_HEADS, WIN_ROWS + tb, A_HEAD_DIM), BF16),
                        pltpu.VMEM((bb, B_HEADS, B_DV, B_DK), F32),
                        pltpu.VMEM((tile, MIX_EVEN), BF16)],
        compiler_params=pltpu.CompilerParams(dimension_semantics=("arbitrary", "arbitrary"),
                                             vmem_limit_bytes=VMEM_LIMIT),
        name="even_mix",
    )(*args)


_PAIR = 2 * C_BLOCK
_HIST_PAD = SUBLANES


def _odd_kernel(x_ref, hist_ref, h0_ref, g_ref, win_ref, cw_ref, cb_ref, wra_ref, bra_ref,
                wri_ref, bri_ref, lam_ref, wout_ref, gfin_ref,
                out_ref, conv_ref, lru_ref,
                xs_ref, gate_ref, a_ref, b_ref, hcar_ref, *, bb, tb, nt):
    t = pl.program_id(1)
    rows = bb * tb
    n_hist = CONV_W - 1

    @pl.when(t == 0)
    def _init_state():
        for s in range(bb):
            xs_ref[s, _HIST_PAD - n_hist:_HIST_PAD, :] = hist_ref[s]
            hcar_ref[s] = h0_ref[s]

    x = x_ref[...]
    h = (_rms_scale(x) * g_ref[...]).astype(BF16)
    xb = _dot(h, win_ref[:, 0:C_WIDTH])
    gate_ref[...] = _dot(h, win_ref[:, C_WIDTH:2 * C_WIDTH])
    for s in range(bb):
        xs_ref[s, _HIST_PAD:_HIST_PAD + tb, :] = xb[s * tb:(s + 1) * tb]

    soft_lam = lam_ref[...]
    soft_lam = jnp.maximum(-soft_lam, 0.0) + jnp.log1p(jnp.exp(-jnp.abs(soft_lam)))
    for m in range(C_BLOCKS // 2):
        c0 = m * _PAIR
        us = []
        for s in range(bb):
            acc = cb_ref[:, c0:c0 + _PAIR]
            for j in range(CONV_W):
                r = _HIST_PAD - n_hist + j
                acc = acc + xs_ref[s, r:r + tb, c0:c0 + _PAIR] * cw_ref[j:j + 1, c0:c0 + _PAIR]
            us.append(acc)
        u = us[0] if bb == 1 else jnp.concatenate(us, axis=0)
        ub = u.astype(BF16)
        lo, hi = ub[:, 0:MXU_DIM], ub[:, LANES:LANES + MXU_DIM]

        def gate_pre(w_ref, bias_ref):
            e = _dot(lo, w_ref[2 * m])
            o = _dot(hi, w_ref[2 * m + 1])
            full = jnp.concatenate([e[:, 0:LANES], e[:, LANES:] + o[:, 0:LANES], o[:, LANES:]], axis=1)
            return full + bias_ref[:, c0:c0 + _PAIR]

        rg = _sigmoid(gate_pre(wra_ref, bra_ref))
        ig = _sigmoid(gate_pre(wri_ref, bri_ref))
        log_a = -LRU_C * rg * soft_lam[:, c0:c0 + _PAIR]
        a = jnp.exp(log_a)
        a_ref[:, c0:c0 + _PAIR] = a
        b_ref[:, c0:c0 + _PAIR] = jnp.sqrt(jnp.tanh(-log_a) * (a * a + 1.0)) * ig * u

    row_id = lax.broadcasted_iota(jnp.int32, (SUBLANES, C_WIDTH), 0)
    groups = tb // SUBLANES
    for s in range(bb):
        def group_body(r, hprev, s=s):
            r8 = pl.ds(pl.multiple_of(s * tb + r * SUBLANES, SUBLANES), SUBLANES)
            a8 = a_ref[r8, :]
            b8 = b_ref[r8, :]
            for sh in (1, 2, 4):
                keep = row_id >= sh
                a_sh = jnp.where(keep, pltpu.roll(a8, sh, axis=0), 1.0)
                b_sh = jnp.where(keep, pltpu.roll(b8, sh, axis=0), 0.0)
                b8 = a8 * b_sh + b8
                a8 = a8 * a_sh
            h8 = a8 * hprev + b8
            b_ref[r8, :] = h8
            return h8[SUBLANES - 1:SUBLANES, :]

        hcar_ref[s] = lax.fori_loop(0, groups, group_body, hcar_ref[s])
        xs_ref[s, _HIST_PAD - n_hist:_HIST_PAD, :] = xs_ref[s, _HIST_PAD + tb - n_hist:_HIST_PAD + tb, :]

    y = (b_ref[...] * _silu(gate_ref[...])).astype(BF16)
    res = x + _dot(y, wout_ref[...])
    out_ref[...] = _rms_scale(res) * gfin_ref[...]

    @pl.when(t == nt - 1)
    def _emit_state():
        for s in range(bb):
            conv_ref[s] = xs_ref[s, _HIST_PAD - n_hist:_HIST_PAD, :]
            lru_ref[s] = hcar_ref[s]


def _odd_layer(x2, hist, h0, norm_g, w_in, conv_w, conv_b, w_ra, b_ra, w_ri, b_ri, lam, w_out, g_fin,
               *, batch, seq):
    tb = min(seq, ROW_TILE)
    bb = ROW_TILE // tb
    nt = seq // tb
    nb = batch // bb
    assert bb == 1 or nt == 1
    n_hist = CONV_W - 1
    row_spec = pl.BlockSpec((ROW_TILE, D_MODEL), lambda b, t: (b * nt + t, 0))
    hist_spec = pl.BlockSpec((bb, n_hist, C_WIDTH), lambda b, t: (b, 0, 0))
    h_spec = pl.BlockSpec((bb, 1, C_WIDTH), lambda b, t: (b, 0, 0))
    consts = (norm_g, w_in, conv_w, conv_b, w_ra, b_ra, w_ri, b_ri, lam, w_out, g_fin)
    return pl.pallas_call(
        functools.partial(_odd_kernel, bb=bb, tb=tb, nt=nt),
        grid=(nb, nt),
        in_specs=[row_spec, hist_spec, h_spec] + [_const_spec(c.shape) for c in consts],
        out_specs=[row_spec, hist_spec, h_spec],
        out_shape=[jax.ShapeDtypeStruct((batch * seq, D_MODEL), F32),
                   jax.ShapeDtypeStruct((batch, n_hist, C_WIDTH), F32),
                   jax.ShapeDtypeStruct((batch, 1, C_WIDTH), F32)],
        scratch_shapes=[pltpu.VMEM((bb, _HIST_PAD + tb, C_WIDTH), F32),
                        pltpu.VMEM((ROW_TILE, C_WIDTH), F32),
                        pltpu.VMEM((ROW_TILE, C_WIDTH), F32),
                        pltpu.VMEM((ROW_TILE, C_WIDTH), F32),
                        pltpu.VMEM((bb, 1, C_WIDTH), F32)],
        compiler_params=pltpu.CompilerParams(dimension_semantics=("arbitrary", "arbitrary"),
                                             vmem_limit_bytes=VMEM_LIMIT),
        name="odd_layer",
    )(x2, hist, h0, *consts)


def _alibi_table():
    slopes = 2.0 ** (-8.0 * jnp.arange(1, A_HEADS + 1, dtype=F32) / A_HEADS)
    dist = jnp.abs(WIN_ROWS + jnp.arange(CHUNK)[:, None] - jnp.arange(WIN_KEYS)[None, :]).astype(F32)
    return slopes[:, None, None] * dist[None]


def _chunk_tri():
    r = np.arange(ROW_TILE)
    same_chunk = (r[:, None] // CHUNK) == (r[None, :] // CHUNK)
    return jnp.asarray(same_chunk & (r[:, None] >= r[None, :]), BF16)


def _pad_gate_blocks(w):
    pad = MXU_DIM - C_BLOCK
    even = jnp.pad(w[0::2], ((0, 0), (0, pad), (0, pad)))
    odd = jnp.pad(w[1::2], ((0, 0), (pad, 0), (pad, 0)))
    return jnp.stack([even, odd], axis=1).reshape(C_BLOCKS, MXU_DIM, MXU_DIM).astype(BF16)


def kernel(x_prompt, x_sample, cache_swa_k, cache_swa_v, state_gla, cache_conv, state_lru, norm_even, w_in_even, w_gate_lr, b_gate_lr, sinks, gla_norm, w_out_even, norm_odd, w_in_odd, conv_w, conv_b, w_rg_a, b_rg_a, w_rg_i, b_rg_i, lru_lambda, w_out_odd, norm_final):
    batch, seq, _ = x_prompt.shape
    dbatch, dseq, _ = x_sample.shape
    row = lambda v: v.reshape(1, -1)

    lr0 = _OFF_GATE
    w_e = w_in_even[0]
    w_main = jnp.concatenate([w_e[:, :lr0], w_e[:, lr0 + B_LOWRANK:]], axis=1).astype(BF16)
    w_lr1 = jnp.pad(w_e[:, lr0:lr0 + B_LOWRANK], ((0, 0), (0, LANES - B_LOWRANK))).astype(BF16)
    w_lr2 = jnp.pad(w_gate_lr[0], ((0, LANES - B_LOWRANK), (0, 0))).astype(BF16)
    tri = _chunk_tri()
    alibi = _alibi_table()
    w_out_e = w_out_even[0].astype(BF16)
    even_consts = (row(norm_even[0]), w_main, w_lr1, w_lr2, row(b_gate_lr[0]), tri)

    def even_layer(x, past):
        b, tl, _ = x.shape
        x2 = x.reshape(b * tl, D_MODEL)
        q, k, v, bq, bk, bv, gcum, gate = _even_in(x2, *even_consts)
        if past is None:
            kprev, vprev, s0 = k, v, None
        else:
            kprev = past[0].reshape(b * WIN_ROWS, A_KV_WIDTH)
            vprev = past[1].reshape(b * WIN_ROWS, A_KV_WIDTH)
            s0 = past[2]
        out, s_new = _even_mix(sinks[0], q, k, v, kprev, vprev, bq, bk, bv, gcum, gate, x2, alibi,
                               row(gla_norm[0]), w_out_e, s0, batch=b, seq=tl)
        k4 = k.reshape(b, tl, A_KV_HEADS, A_HEAD_DIM)
        v4 = v.reshape(b, tl, A_KV_HEADS, A_HEAD_DIM)
        return out, k4, v4, s_new

    xp, pk, pv, pg = even_layer(x_prompt, None)
    xs, sk, sv, sg = even_layer(x_sample, (cache_swa_k[0], cache_swa_v[0], state_gla[0]))

    odd_consts = (row(norm_odd[0]), w_in_odd[0].astype(BF16), conv_w[0], row(conv_b[0]),
                  _pad_gate_blocks(w_rg_a[0]), row(b_rg_a[0]), _pad_gate_blocks(w_rg_i[0]), row(b_rg_i[0]),
                  row(lru_lambda[0]), w_out_odd[0].astype(BF16), row(norm_final))
    zero_hist = jnp.zeros((batch, CONV_W - 1, C_WIDTH), F32)
    zero_h = jnp.zeros((batch, 1, C_WIDTH), F32)
    yp, pc, plru = _odd_layer(xp, zero_hist, zero_h, *odd_consts, batch=batch, seq=seq)
    ys, sc, slru = _odd_layer(xs, cache_conv[0], state_lru[0].reshape(dbatch, 1, C_WIDTH), *odd_consts,
                              batch=dbatch, seq=dseq)

    return (yp.reshape(batch, seq, D_MODEL), ys.reshape(dbatch, dseq, D_MODEL),
            pk[None, :, -WIN_ROWS:], pv[None, :, -WIN_ROWS:], pg[None], pc[None],
            plru.reshape(1, batch, C_WIDTH),
            sk[None], sv[None], sg[None], sc[None], slru.reshape(1, dbatch, C_WIDTH))
```

```python
import functools

import jax
import jax.numpy as jnp
import numpy as np
from jax import lax
from jax.experimental import pallas as pl
from jax.experimental.pallas import tpu as pltpu

F32 = jnp.float32
BF16 = jnp.bfloat16

D_MODEL = 1024
CHUNK = 64
EPS = 1e-6
A_HEADS = 16
A_KV_HEADS = 4
A_HEAD_DIM = 64
A_GROUP = A_HEADS // A_KV_HEADS
A_WIDTH = A_HEADS * A_HEAD_DIM
A_KV_WIDTH = A_KV_HEADS * A_HEAD_DIM
WIN_ROWS = 128
WIN_KEYS = WIN_ROWS + CHUNK
B_HEADS = 4
B_DK = 128
B_DV = 256
B_QK_WIDTH = B_HEADS * B_DK
B_WIDTH = B_HEADS * B_DV
B_LOWRANK = 16
B_GATE_NORM = 16.0
C_WIDTH = 1536
C_BLOCKS = 8
C_BLOCK = C_WIDTH // C_BLOCKS
CONV_W = 4
LRU_C = 8.0
MIX_EVEN = A_WIDTH + B_WIDTH

LANES = 128
SUBLANES = 8
MXU_DIM = 256
ROW_TILE = 512
VMEM_LIMIT = 48 * 1024 * 1024

_OFF_Q = 0
_OFF_K = _OFF_Q + A_WIDTH
_OFF_V = _OFF_K + A_KV_WIDTH
_OFF_BQ = _OFF_V + A_KV_WIDTH
_OFF_BK = _OFF_BQ + B_QK_WIDTH
_OFF_BV = _OFF_BK + B_QK_WIDTH
_OFF_GATE = _OFF_BV + B_WIDTH
_MAIN_WIDTH = _OFF_GATE + MIX_EVEN


def _const_spec(shape):
    nd = len(shape)
    return pl.BlockSpec(shape, lambda *_: (0,) * nd, pipeline_mode=pl.Buffered(1))


def _rms_scale(x):
    return x * lax.rsqrt(jnp.mean(x * x, axis=-1, keepdims=True) + EPS)


def _silu(x):
    return x * (1.0 / (1.0 + jnp.exp(-x)))


def _sigmoid(x):
    return 1.0 / (1.0 + jnp.exp(-x))


def _dot(a, b):
    return jnp.dot(a, b, preferred_element_type=F32)


def _dot_nt(a, b):
    return lax.dot_general(a, b, (((1,), (1,)), ((), ())), preferred_element_type=F32)


def _dot_tn(a, b):
    return lax.dot_general(a, b, (((0,), (0,)), ((), ())), preferred_element_type=F32)


def _even_in_kernel(x_ref, g_ref, w_ref, wlr1_ref, wlr2_ref, blr_ref, tri_ref,
                    q_ref, k_ref, v_ref, bq_ref, bk_ref, bv_ref, gcum_ref, gate_ref):
    h = (_rms_scale(x_ref[...]) * g_ref[...]).astype(BF16)

    def proj(off, width):
        return _dot(h, w_ref[:, off:off + width])

    q_ref[...] = proj(_OFF_Q, A_WIDTH).astype(BF16)
    k_ref[...] = proj(_OFF_K, A_KV_WIDTH)
    v_ref[...] = proj(_OFF_V, A_KV_WIDTH)
    bq_ref[...] = proj(_OFF_BQ, B_QK_WIDTH).astype(BF16)
    bk_ref[...] = proj(_OFF_BK, B_QK_WIDTH).astype(BF16)
    bv_ref[...] = proj(_OFF_BV, B_WIDTH).astype(BF16)
    gate_ref[...] = proj(_OFF_GATE, MIX_EVEN).astype(BF16)

    low = _dot(h, wlr1_ref[...])
    pre = _dot(low.astype(BF16), wlr2_ref[...]) + blr_ref[...]
    glog = (jnp.minimum(pre, 0.0) - jnp.log1p(jnp.exp(-jnp.abs(pre)))) * (1.0 / B_GATE_NORM)
    tri = tri_ref[...]
    g1 = glog.astype(BF16)
    r1 = glog - g1.astype(F32)
    g2 = r1.astype(BF16)
    g3 = (r1 - g2.astype(F32)).astype(BF16)
    gcum_ref[...] = _dot(tri, g1) + _dot(tri, g2) + _dot(tri, g3)


def _even_in(x2, norm_g, w_main, w_lr1, w_lr2, b_lr, tri):
    rows = x2.shape[0]
    nsteps = rows // ROW_TILE
    row_spec = lambda w: pl.BlockSpec((ROW_TILE, w), lambda i: (i, 0))
    out_widths = (A_WIDTH, A_KV_WIDTH, A_KV_WIDTH, B_QK_WIDTH, B_QK_WIDTH, B_WIDTH, B_QK_WIDTH, MIX_EVEN)
    out_dtypes = (BF16, F32, F32, BF16, BF16, BF16, F32, BF16)
    return pl.pallas_call(
        _even_in_kernel,
        grid=(nsteps,),
        in_specs=[row_spec(D_MODEL), _const_spec(norm_g.shape), _const_spec(w_main.shape),
                  _const_spec(w_lr1.shape), _const_spec(w_lr2.shape), _const_spec(b_lr.shape),
                  _const_spec(tri.shape)],
        out_specs=[row_spec(w) for w in out_widths],
        out_shape=[jax.ShapeDtypeStruct((rows, w), d) for w, d in zip(out_widths, out_dtypes)],
        compiler_params=pltpu.CompilerParams(dimension_semantics=("arbitrary",),
                                             vmem_limit_bytes=VMEM_LIMIT),
        name="even_in",
    )(x2, norm_g, w_main, w_lr1, w_lr2, b_lr, tri)


def _even_mix_kernel(*refs, bb, tb, nt, has_past):
    if has_past:
        (sinks_ref, q_ref, kc_ref, vc_ref, kp_ref, vp_ref, bq_ref, bk_ref, bv_ref, gcum_ref,
         gate_ref, x_ref, alibi_ref, glag_ref, wout_ref, s0_ref,
         out_ref, sfin_ref, kbuf, vbuf, st_ref, y_ref) = refs
    else:
        (sinks_ref, q_ref, kc_ref, vc_ref, kp_ref, vp_ref, bq_ref, bk_ref, bv_ref, gcum_ref,
         gate_ref, x_ref, alibi_ref, glag_ref, wout_ref,
         out_ref, sfin_ref, kbuf, vbuf, st_ref, y_ref) = refs
        s0_ref = None
    t = pl.program_id(1)
    tc = tb // CHUNK

    for bi in range(bb):
        for kh in range(A_KV_HEADS):
            cols = slice(kh * A_HEAD_DIM, (kh + 1) * A_HEAD_DIM)
            kbuf[bi, kh, 0:WIN_ROWS, :] = kp_ref[bi * WIN_ROWS:(bi + 1) * WIN_ROWS, cols].astype(BF16)
            vbuf[bi, kh, 0:WIN_ROWS, :] = vp_ref[bi * WIN_ROWS:(bi + 1) * WIN_ROWS, cols].astype(BF16)
            kbuf[bi, kh, WIN_ROWS:WIN_ROWS + tb, :] = kc_ref[bi * tb:(bi + 1) * tb, cols].astype(BF16)
            vbuf[bi, kh, WIN_ROWS:WIN_ROWS + tb, :] = vc_ref[bi * tb:(bi + 1) * tb, cols].astype(BF16)

    @pl.when(t == 0)
    def _init_state():
        for bi in range(bb):
            for hh in range(B_HEADS):
                if has_past:
                    st_ref[bi, hh] = s0_ref[bi, hh].T
                else:
                    st_ref[bi, hh] = jnp.zeros((B_DV, B_DK), F32)

    key_idx = lax.broadcasted_iota(jnp.int32, (CHUNK, WIN_KEYS), 1)
    tril = (lax.broadcasted_iota(jnp.int32, (CHUNK, CHUNK), 0)
            >= lax.broadcasted_iota(jnp.int32, (CHUNK, CHUNK), 1))
    glag = glag_ref[...]

    def chunk_body(n, carry):
        bi = n // tc
        c = n % tc
        r0 = pl.multiple_of(n * CHUNK, CHUNK)
        w0 = pl.multiple_of(c * CHUNK, CHUNK)
        rows = pl.ds(r0, CHUNK)

        if not has_past:
            key_ok = key_idx + (t * tc + c - WIN_ROWS // CHUNK) * CHUNK >= 0
        for kh in range(A_KV_HEADS):
            q4 = q_ref[rows, kh * A_GROUP * A_HEAD_DIM:(kh + 1) * A_GROUP * A_HEAD_DIM]
            qs = jnp.concatenate([q4[:, g * A_HEAD_DIM:(g + 1) * A_HEAD_DIM] for g in range(A_GROUP)],
                                 axis=0)
            kw = kbuf[bi, kh, pl.ds(w0, WIN_KEYS), :]
            vw = vbuf[bi, kh, pl.ds(w0, WIN_KEYS), :]
            s = _dot_nt(qs, kw)
            ps, inv = [], []
            for g in range(A_GROUP):
                hd = kh * A_GROUP + g
                sg = s[g * CHUNK:(g + 1) * CHUNK] * (A_HEAD_DIM ** -0.5) - alibi_ref[hd]
                if not has_past:
                    sg = jnp.where(key_ok, sg, -1e30)
                sink = sinks_ref[hd]
                m = jnp.maximum(jnp.max(sg, axis=-1, keepdims=True), sink)
                p = jnp.exp(sg - m)
                den = jnp.sum(p, axis=-1, keepdims=True) + jnp.exp(sink - m)
                ps.append((p * (1.0 / den)).astype(BF16))
            o = _dot(jnp.concatenate(ps, axis=0), vw)
            ao = jnp.concatenate([o[g * CHUNK:(g + 1) * CHUNK] for g in range(A_GROUP)], axis=1)
            cols = slice(kh * A_GROUP * A_HEAD_DIM, (kh + 1) * A_GROUP * A_HEAD_DIM)
            y_ref[rows, cols] = (ao * _silu(gate_ref[rows, cols].astype(F32))).astype(BF16)

        gc = gcum_ref[rows, :]
        gl = gcum_ref[pl.ds(r0 + CHUNK - 1, 1), :]
        bk = bk_ref[rows, :].astype(F32)
        qg = (bq_ref[rows, :].astype(F32) * (B_DK ** -0.5) * jnp.exp(gc)).astype(BF16)
        kg = (bk * jnp.exp(-gc)).astype(BF16)
        kd = (bk * jnp.exp(gl - gc)).astype(BF16)
        dec = jnp.exp(gl)
        for hh in range(B_HEADS):
            ks = slice(hh * B_DK, (hh + 1) * B_DK)
            vs = slice(hh * B_DV, (hh + 1) * B_DV)
            a = jnp.where(tril, _dot_nt(qg[:, ks], kg[:, ks]), 0.0).astype(BF16)
            vh = bv_ref[rows, vs]
            st = st_ref[bi, hh]
            o = _dot(a, vh) + _dot_nt(qg[:, ks], st.astype(BF16))
            st_ref[bi, hh] = st * dec[:, ks] + _dot_tn(vh, kd[:, ks])
            bo = _rms_scale(o) * glag
            cols = slice(A_WIDTH + hh * B_DV, A_WIDTH + (hh + 1) * B_DV)
            y_ref[rows, cols] = (bo * _silu(gate_ref[rows, cols].astype(F32))).astype(BF16)
        return carry

    lax.fori_loop(0, bb * tc, chunk_body, 0)

    out_ref[...] = x_ref[...] + _dot(y_ref[...], wout_ref[...])

    @pl.when(t == nt - 1)
    def _emit_state():
        for bi in range(bb):
            for hh in range(B_HEADS):
                sfin_ref[bi, hh] = st_ref[bi, hh].T


def _even_mix(sinks, q, k, v, kprev, vprev, bq, bk, bv, gcum, gate, x2, alibi, gla_g, w_out, s0,
              *, batch, seq):
    has_past = s0 is not None
    tile = ROW_TILE // 2 if has_past else ROW_TILE
    tb = min(seq, tile)
    bb = tile // tb
    nt = seq // tb
    nb = batch // bb
    assert bb == 1 or nt == 1
    row_spec = lambda w: pl.BlockSpec((tile, w), lambda b, t: (b * nt + t, 0))
    if has_past:
        prev_spec = pl.BlockSpec((bb * WIN_ROWS, A_KV_WIDTH), lambda b, t: (b, 0))
    else:
        per_b = seq // WIN_ROWS
        step = tb // WIN_ROWS
        prev_spec = pl.BlockSpec((WIN_ROWS, A_KV_WIDTH),
                                 lambda b, t: (b * per_b + jnp.maximum(t * step - 1, 0), 0))
    state_spec = pl.BlockSpec((bb, B_HEADS, B_DK, B_DV), lambda b, t: (b, 0, 0, 0))
    in_specs = [pl.BlockSpec(memory_space=pltpu.SMEM),
                row_spec(A_WIDTH), row_spec(A_KV_WIDTH), row_spec(A_KV_WIDTH), prev_spec, prev_spec,
                row_spec(B_QK_WIDTH), row_spec(B_QK_WIDTH), row_spec(B_WIDTH), row_spec(B_QK_WIDTH),
                row_spec(MIX_EVEN), row_spec(D_MODEL),
                _const_spec(alibi.shape), _const_spec(gla_g.shape), _const_spec(w_out.shape)]
    args = [sinks, q, k, v, kprev, vprev, bq, bk, bv, gcum, gate, x2, alibi, gla_g, w_out]
    if has_past:
        in_specs.append(state_spec)
        args.append(s0)
    return pl.pallas_call(
        functools.partial(_even_mix_kernel, bb=bb, tb=tb, nt=nt, has_past=has_past),
        grid=(nb, nt),
        in_specs=in_specs,
        out_specs=[row_spec(D_MODEL), state_spec],
        out_shape=[jax.ShapeDtypeStruct((batch * seq, D_MODEL), F32),
                   jax.ShapeDtypeStruct((batch, B_HEADS, B_DK, B_DV), F32)],
        scratch_shapes=[pltpu.VMEM((bb, A_KV_HEADS, WIN_ROWS + tb, A_HEAD_DIM), BF16),
                        pltpu.VMEM((bb, A_KV_HEADS, WIN_ROWS + tb, A_HEAD_DIM), BF16),
                        pltpu.VMEM((bb, B_HEADS, B_DV, B_DK), F32),
                        pltpu.VMEM((tile, MIX_EVEN), BF16)],
        compiler_params=pltpu.CompilerParams(dimension_semantics=("arbitrary", "arbitrary"),
                                             vmem_limit_bytes=VMEM_LIMIT),
        name="even_mix",
    )(*args)


_PAIR = 2 * C_BLOCK
ODD_BATCH = SUBLANES
ODD_STEPS = ROW_TILE // ODD_BATCH
_HIST_ROWS = (CONV_W - 1) * ODD_BATCH


def _odd_kernel(x_ref, hist_ref, h0_ref, g_ref, win_ref, cw_ref, cb_ref, wra_ref, bra_ref,
                wri_ref, bri_ref, lam_ref, wout_ref, gfin_ref,
                out_ref, conv_ref, lru_ref,
                xs_ref, gate_ref, a_ref, b_ref, y_ref, hcar_ref, *, nt):
    t = pl.program_id(1)
    half = C_WIDTH // 2

    @pl.when(t == 0)
    def _init_state():
        hist = hist_ref[...].reshape(_HIST_ROWS, C_WIDTH)
        for k in range(2):
            xs_ref[k, 0:_HIST_ROWS, :] = hist[:, k * half:(k + 1) * half]
        hcar_ref[...] = h0_ref[...]

    x = jnp.transpose(x_ref[...], (1, 0, 2)).reshape(ROW_TILE, D_MODEL)
    h = (_rms_scale(x) * g_ref[...]).astype(BF16)

    def project_half(k):
        xs_ref[k, _HIST_ROWS:_HIST_ROWS + ROW_TILE, :] = _dot(h, win_ref[:, k * half:(k + 1) * half])

    project_half(0)

    soft_lam = lam_ref[...]
    soft_lam = jnp.maximum(-soft_lam, 0.0) + jnp.log1p(jnp.exp(-jnp.abs(soft_lam)))
    for m in range(C_BLOCKS // 2):
        if m == 0:
            project_half(1)
        else:
            g0 = (m - 1) * _PAIR * 2
            g1 = min(g0 + 2 * _PAIR, C_WIDTH)
            if g0 < C_WIDTH:
                gate_ref[:, g0:g1] = _dot(h, win_ref[:, C_WIDTH + g0:C_WIDTH + g1])
        c0 = m * _PAIR
        k, ck = divmod(c0, half)
        u = cb_ref[:, c0:c0 + _PAIR]
        for j in range(CONV_W):
            r = j * ODD_BATCH
            u = u + xs_ref[k, r:r + ROW_TILE, ck:ck + _PAIR] * cw_ref[j:j + 1, c0:c0 + _PAIR]
        ub = u.astype(BF16)
        lo, hi = ub[:, 0:MXU_DIM], ub[:, LANES:LANES + MXU_DIM]

        def gate_pre(w_ref, bias_ref):
            e = _dot(lo, w_ref[2 * m])
            o = _dot(hi, w_ref[2 * m + 1])
            full = jnp.concatenate([e[:, 0:LANES], e[:, LANES:] + o[:, 0:LANES], o[:, LANES:]], axis=1)
            return full + bias_ref[:, c0:c0 + _PAIR]

        rg = _sigmoid(gate_pre(wra_ref, bra_ref))
        ig = _sigmoid(gate_pre(wri_ref, bri_ref))
        log_a = -LRU_C * rg * soft_lam[:, c0:c0 + _PAIR]
        a = jnp.exp(log_a)
        a_ref[:, c0:c0 + _PAIR] = a
        b_ref[:, c0:c0 + _PAIR] = jnp.sqrt(jnp.tanh(-log_a) * (a * a + 1.0)) * ig * u

    pair = 2 * ODD_BATCH

    def step_body(i, hprev):
        r0 = pl.multiple_of(i * pair, pair)
        lo8, hi8, both = pl.ds(r0, ODD_BATCH), pl.ds(r0 + ODD_BATCH, ODD_BATCH), pl.ds(r0, pair)
        h1 = a_ref[lo8, :] * hprev + b_ref[lo8, :]
        h2 = a_ref[hi8, :] * h1 + b_ref[hi8, :]
        y_ref[both, :] = (jnp.concatenate([h1, h2], axis=0) * _silu(gate_ref[both, :])).astype(BF16)
        return h2

    hcar_ref[...] = lax.fori_loop(0, ODD_STEPS // 2, step_body, hcar_ref[...])
    for k in range(2):
        xs_ref[k, 0:_HIST_ROWS, :] = xs_ref[k, ROW_TILE:ROW_TILE + _HIST_ROWS, :]

    res = x + _dot(y_ref[...], wout_ref[...])
    out = _rms_scale(res) * gfin_ref[...]
    out_ref[...] = jnp.transpose(out.reshape(ODD_STEPS, ODD_BATCH, D_MODEL), (1, 0, 2))

    @pl.when(t == nt - 1)
    def _emit_state():
        for k in range(2):
            conv_ref[:, :, k * half:(k + 1) * half] = (
                xs_ref[k, 0:_HIST_ROWS, :].reshape(CONV_W - 1, ODD_BATCH, half))
        lru_ref[...] = hcar_ref[...]


def _odd_layer(x3, hist_t, h0, norm_g, w_in, conv_w, conv_b, w_ra, b_ra, w_ri, b_ri, lam, w_out, g_fin):
    batch, seq, _ = x3.shape
    nt = seq // ODD_STEPS
    nb = batch // ODD_BATCH
    n_hist = CONV_W - 1
    x_spec = pl.BlockSpec((ODD_BATCH, ODD_STEPS, D_MODEL), lambda b, t: (b, t, 0))
    hist_spec = pl.BlockSpec((n_hist, ODD_BATCH, C_WIDTH), lambda b, t: (0, b, 0))
    h_spec = pl.BlockSpec((ODD_BATCH, C_WIDTH), lambda b, t: (b, 0))
    consts = (norm_g, w_in, conv_w, conv_b, w_ra, b_ra, w_ri, b_ri, lam, w_out, g_fin)
    return pl.pallas_call(
        functools.partial(_odd_kernel, nt=nt),
        grid=(nb, nt),
        in_specs=[x_spec, hist_spec, h_spec] + [_const_spec(c.shape) for c in consts],
        out_specs=[x_spec, hist_spec, h_spec],
        out_shape=[jax.ShapeDtypeStruct((batch, seq, D_MODEL), F32),
                   jax.ShapeDtypeStruct((n_hist, batch, C_WIDTH), F32),
                   jax.ShapeDtypeStruct((batch, C_WIDTH), F32)],
        scratch_shapes=[pltpu.VMEM((2, _HIST_ROWS + ROW_TILE, C_WIDTH // 2), F32),
                        pltpu.VMEM((ROW_TILE, C_WIDTH), F32),
                        pltpu.VMEM((ROW_TILE, C_WIDTH), F32),
                        pltpu.VMEM((ROW_TILE, C_WIDTH), F32),
                        pltpu.VMEM((ROW_TILE, C_WIDTH), BF16),
                        pltpu.VMEM((ODD_BATCH, C_WIDTH), F32)],
        compiler_params=pltpu.CompilerParams(dimension_semantics=("arbitrary", "arbitrary"),
                                             vmem_limit_bytes=VMEM_LIMIT),
        name="odd_layer",
    )(x3, hist_t, h0, *consts)


def _alibi_table():
    slopes = 2.0 ** (-8.0 * jnp.arange(1, A_HEADS + 1, dtype=F32) / A_HEADS)
    dist = jnp.abs(WIN_ROWS + jnp.arange(CHUNK)[:, None] - jnp.arange(WIN_KEYS)[None, :]).astype(F32)
    return slopes[:, None, None] * dist[None]


def _chunk_tri():
    r = np.arange(ROW_TILE)
    same_chunk = (r[:, None] // CHUNK) == (r[None, :] // CHUNK)
    return jnp.asarray(same_chunk & (r[:, None] >= r[None, :]), BF16)


def _pad_gate_blocks(w):
    pad = MXU_DIM - C_BLOCK
    even = jnp.pad(w[0::2], ((0, 0), (0, pad), (0, pad)))
    odd = jnp.pad(w[1::2], ((0, 0), (pad, 0), (pad, 0)))
    return jnp.stack([even, odd], axis=1).reshape(C_BLOCKS, MXU_DIM, MXU_DIM).astype(BF16)


def kernel(x_prompt, x_sample, cache_swa_k, cache_swa_v, state_gla, cache_conv, state_lru, norm_even, w_in_even, w_gate_lr, b_gate_lr, sinks, gla_norm, w_out_even, norm_odd, w_in_odd, conv_w, conv_b, w_rg_a, b_rg_a, w_rg_i, b_rg_i, lru_lambda, w_out_odd, norm_final):
    batch, seq, _ = x_prompt.shape
    dbatch, dseq, _ = x_sample.shape
    row = lambda v: v.reshape(1, -1)

    lr0 = _OFF_GATE
    w_e = w_in_even[0]
    w_main = jnp.concatenate([w_e[:, :lr0], w_e[:, lr0 + B_LOWRANK:]], axis=1).astype(BF16)
    w_lr1 = jnp.pad(w_e[:, lr0:lr0 + B_LOWRANK], ((0, 0), (0, LANES - B_LOWRANK))).astype(BF16)
    w_lr2 = jnp.pad(w_gate_lr[0], ((0, LANES - B_LOWRANK), (0, 0))).astype(BF16)
    tri = _chunk_tri()
    alibi = _alibi_table()
    w_out_e = w_out_even[0].astype(BF16)
    even_consts = (row(norm_even[0]), w_main, w_lr1, w_lr2, row(b_gate_lr[0]), tri)

    def even_layer(x, past):
        b, tl, _ = x.shape
        x2 = x.reshape(b * tl, D_MODEL)
        q, k, v, bq, bk, bv, gcum, gate = _even_in(x2, *even_consts)
        if past is None:
            kprev, vprev, s0 = k, v, None
        else:
            kprev = past[0].reshape(b * WIN_ROWS, A_KV_WIDTH)
            vprev = past[1].reshape(b * WIN_ROWS, A_KV_WIDTH)
            s0 = past[2]
        out, s_new = _even_mix(sinks[0], q, k, v, kprev, vprev, bq, bk, bv, gcum, gate, x2, alibi,
                               row(gla_norm[0]), w_out_e, s0, batch=b, seq=tl)
        k4 = k.reshape(b, tl, A_KV_HEADS, A_HEAD_DIM)
        v4 = v.reshape(b, tl, A_KV_HEADS, A_HEAD_DIM)
        return out, k4, v4, s_new

    xp, pk, pv, pg = even_layer(x_prompt, None)
    xs, sk, sv, sg = even_layer(x_sample, (cache_swa_k[0], cache_swa_v[0], state_gla[0]))

    odd_consts = (row(norm_odd[0]), w_in_odd[0].astype(BF16), conv_w[0], row(conv_b[0]),
                  _pad_gate_blocks(w_rg_a[0]), row(b_rg_a[0]), _pad_gate_blocks(w_rg_i[0]), row(b_rg_i[0]),
                  row(lru_lambda[0]), w_out_odd[0].astype(BF16), row(norm_final))
    zero_hist = jnp.zeros((CONV_W - 1, batch, C_WIDTH), F32)
    zero_h = jnp.zeros((batch, C_WIDTH), F32)
    yp, pc, plru = _odd_layer(xp.reshape(batch, seq, D_MODEL), zero_hist, zero_h, *odd_consts)
    ys, sc, slru = _odd_layer(xs.reshape(dbatch, dseq, D_MODEL), jnp.swapaxes(cache_conv[0], 0, 1),
                              state_lru[0], *odd_consts)

    return (yp, ys, pk[None, :, -WIN_ROWS:], pv[None, :, -WIN_ROWS:], pg[None],
            jnp.swapaxes(pc, 0, 1)[None], plru[None],
            sk[None], sv[None], sg[None], jnp.swapaxes(sc, 0, 1)[None], slru[None])
```

```python
import functools

import jax
import jax.numpy as jnp
import numpy as np
from jax import lax
from jax.experimental import pallas as pl
from jax.experimental.pallas import tpu as pltpu

F32 = jnp.float32
BF16 = jnp.bfloat16

D_MODEL = 1024
CHUNK = 64
EPS = 1e-6
A_HEADS = 16
A_KV_HEADS = 4
A_HEAD_DIM = 64
A_GROUP = A_HEADS // A_KV_HEADS
A_WIDTH = A_HEADS * A_HEAD_DIM
A_KV_WIDTH = A_KV_HEADS * A_HEAD_DIM
WIN_ROWS = 128
WIN_KEYS = WIN_ROWS + CHUNK
B_HEADS = 4
B_DK = 128
B_DV = 256
B_QK_WIDTH = B_HEADS * B_DK
B_WIDTH = B_HEADS * B_DV
B_LOWRANK = 16
B_GATE_NORM = 16.0
C_WIDTH = 1536
C_BLOCKS = 8
C_BLOCK = C_WIDTH // C_BLOCKS
CONV_W = 4
LRU_C = 8.0
MIX_EVEN = A_WIDTH + B_WIDTH

LANES = 128
SUBLANES = 8
MXU_DIM = 256
ROW_TILE = 512
VMEM_LIMIT = 48 * 1024 * 1024

_OFF_Q = 0
_OFF_K = _OFF_Q + A_WIDTH
_OFF_V = _OFF_K + A_KV_WIDTH
_OFF_BQ = _OFF_V + A_KV_WIDTH
_OFF_BK = _OFF_BQ + B_QK_WIDTH
_OFF_BV = _OFF_BK + B_QK_WIDTH
_OFF_GATE = _OFF_BV + B_WIDTH
_MAIN_WIDTH = _OFF_GATE + MIX_EVEN


def _const_spec(shape):
    nd = len(shape)
    return pl.BlockSpec(shape, lambda *_: (0,) * nd, pipeline_mode=pl.Buffered(1))


def _rms_scale(x):
    return x * lax.rsqrt(jnp.mean(x * x, axis=-1, keepdims=True) + EPS)


def _silu(x):
    return x * (1.0 / (1.0 + jnp.exp(-x)))


def _sigmoid(x):
    return 1.0 / (1.0 + jnp.exp(-x))


def _dot(a, b):
    return jnp.dot(a, b, preferred_element_type=F32)


def _dot_nt(a, b):
    return lax.dot_general(a, b, (((1,), (1,)), ((), ())), preferred_element_type=F32)


def _dot_tn(a, b):
    return lax.dot_general(a, b, (((0,), (0,)), ((), ())), preferred_element_type=F32)


def _even_in_kernel(x_ref, g_ref, w_ref, wlr1_ref, wlr2_ref, blr_ref, tri_ref,
                    q_ref, k_ref, v_ref, qg_ref, kg_ref, kd_ref, bv_ref, glast_ref, sgate_ref):
    h = (_rms_scale(x_ref[...]) * g_ref[...]).astype(BF16)

    def proj(off, width):
        return _dot(h, w_ref[:, off:off + width])

    q_ref[...] = (proj(_OFF_Q, A_WIDTH) * (A_HEAD_DIM ** -0.5)).astype(BF16)
    k_ref[...] = proj(_OFF_K, A_KV_WIDTH)
    v_ref[...] = proj(_OFF_V, A_KV_WIDTH)
    bv_ref[...] = proj(_OFF_BV, B_WIDTH).astype(BF16)
    sgate_ref[...] = _silu(proj(_OFF_GATE, MIX_EVEN)).astype(BF16)

    low = _dot(h, wlr1_ref[...])
    pre = _dot(low.astype(BF16), wlr2_ref[...]) + blr_ref[...]
    glog = (jnp.minimum(pre, 0.0) - jnp.log1p(jnp.exp(-jnp.abs(pre)))) * (1.0 / B_GATE_NORM)
    tri = tri_ref[...]
    g1 = glog.astype(BF16)
    r1 = glog - g1.astype(F32)
    g2 = r1.astype(BF16)
    g3 = (r1 - g2.astype(F32)).astype(BF16)
    gcum = _dot(tri, g1) + _dot(tri, g2) + _dot(tri, g3)

    bk = proj(_OFF_BK, B_QK_WIDTH)
    qg_ref[...] = (proj(_OFF_BQ, B_QK_WIDTH) * (B_DK ** -0.5) * jnp.exp(gcum)).astype(BF16)
    kg_ref[...] = (bk * jnp.exp(-gcum)).astype(BF16)
    for c in range(ROW_TILE // CHUNK):
        rows = slice(c * CHUNK, (c + 1) * CHUNK)
        glast = gcum[(c + 1) * CHUNK - 1:(c + 1) * CHUNK, :]
        glast_ref[c:c + 1, :] = glast
        kd_ref[rows, :] = (bk[rows] * jnp.exp(glast - gcum[rows])).astype(BF16)


def _even_in(x2, norm_g, w_main, w_lr1, w_lr2, b_lr, tri):
    rows = x2.shape[0]
    nsteps = rows // ROW_TILE
    row_spec = lambda w: pl.BlockSpec((ROW_TILE, w), lambda i: (i, 0))
    chunks = ROW_TILE // CHUNK
    out_widths = (A_WIDTH, A_KV_WIDTH, A_KV_WIDTH, B_QK_WIDTH, B_QK_WIDTH, B_QK_WIDTH, B_WIDTH,
                  B_QK_WIDTH, MIX_EVEN)
    out_dtypes = (BF16, F32, F32, BF16, BF16, BF16, BF16, F32, BF16)
    out_rows = [ROW_TILE] * 7 + [chunks, ROW_TILE]
    return pl.pallas_call(
        _even_in_kernel,
        grid=(nsteps,),
        in_specs=[row_spec(D_MODEL), _const_spec(norm_g.shape), _const_spec(w_main.shape),
                  _const_spec(w_lr1.shape), _const_spec(w_lr2.shape), _const_spec(b_lr.shape),
                  _const_spec(tri.shape)],
        out_specs=[pl.BlockSpec((r, w), lambda i: (i, 0)) for r, w in zip(out_rows, out_widths)],
        out_shape=[jax.ShapeDtypeStruct((nsteps * r, w), d)
                   for r, w, d in zip(out_rows, out_widths, out_dtypes)],
        compiler_params=pltpu.CompilerParams(dimension_semantics=("arbitrary",),
                                             vmem_limit_bytes=VMEM_LIMIT),
        name="even_in",
    )(x2, norm_g, w_main, w_lr1, w_lr2, b_lr, tri)


def _even_mix_kernel(*refs, bb, tb, nt, has_past):
    if has_past:
        (q_ref, kc_ref, vc_ref, kp_ref, vp_ref, qg_ref, kg_ref, kd_ref, bv_ref, glast_ref,
         sgate_ref, x_ref, bias_ref, sink_ref, glag_ref, wout_ref, s0_ref,
         out_ref, sfin_ref, kbuf, vbuf, st_ref, y_ref) = refs
    else:
        (q_ref, kc_ref, vc_ref, kp_ref, vp_ref, qg_ref, kg_ref, kd_ref, bv_ref, glast_ref,
         sgate_ref, x_ref, bias_ref, sink_ref, glag_ref, wout_ref,
         out_ref, sfin_ref, kbuf, vbuf, st_ref, y_ref) = refs
        s0_ref = None
    t = pl.program_id(1)
    tc = tb // CHUNK

    one_lane = jnp.where(lax.broadcasted_iota(jnp.int32, (WIN_ROWS + tb, A_HEAD_DIM), 1) == 0,
                         1.0, 0.0).astype(BF16)
    for bi in range(bb):
        for kh in range(A_KV_HEADS):
            cols = slice(kh * A_HEAD_DIM, (kh + 1) * A_HEAD_DIM)
            kbuf[bi, kh, 0:WIN_ROWS, :] = kp_ref[bi * WIN_ROWS:(bi + 1) * WIN_ROWS, cols].astype(BF16)
            kbuf[bi, kh, WIN_ROWS:WIN_ROWS + tb, :] = kc_ref[bi * tb:(bi + 1) * tb, cols].astype(BF16)
            vbuf[bi, kh, 0:WIN_ROWS, 0:A_HEAD_DIM] = (
                vp_ref[bi * WIN_ROWS:(bi + 1) * WIN_ROWS, cols].astype(BF16))
            vbuf[bi, kh, WIN_ROWS:WIN_ROWS + tb, 0:A_HEAD_DIM] = (
                vc_ref[bi * tb:(bi + 1) * tb, cols].astype(BF16))
            vbuf[bi, kh, :, A_HEAD_DIM:2 * A_HEAD_DIM] = one_lane

    @pl.when(t == 0)
    def _init_state():
        for bi in range(bb):
            for hh in range(B_HEADS):
                if has_past:
                    st_ref[bi, hh] = s0_ref[bi, hh].T
                else:
                    st_ref[bi, hh] = jnp.zeros((B_DV, B_DK), F32)

    tril = (lax.broadcasted_iota(jnp.int32, (CHUNK, CHUNK), 0)
            >= lax.broadcasted_iota(jnp.int32, (CHUNK, CHUNK), 1))
    glag = glag_ref[...]
    tail_rows = 16
    tail_row0 = lax.broadcasted_iota(jnp.int32, (tail_rows, A_GROUP * CHUNK), 0) == 0
    tail_iota = lax.broadcasted_iota(jnp.int32, (tail_rows, 2 * A_HEAD_DIM), 0)
    tail_lane = lax.broadcasted_iota(jnp.int32, (tail_rows, 2 * A_HEAD_DIM), 1)
    v_tail = jnp.where((tail_iota == 0) & (tail_lane == A_HEAD_DIM), 1.0, 0.0).astype(BF16)

    def chunk_body(n, carry):
        bi = n // tc
        c = n % tc
        r0 = pl.multiple_of(n * CHUNK, CHUNK)
        w0 = pl.multiple_of(c * CHUNK, CHUNK)
        rows = pl.ds(r0, CHUNK)

        variant = 0 if has_past else jnp.minimum(t * tc + c, WIN_ROWS // CHUNK)
        for kh in range(A_KV_HEADS):
            cols = slice(kh * A_GROUP * A_HEAD_DIM, (kh + 1) * A_GROUP * A_HEAD_DIM)
            q4 = q_ref[rows, cols]
            qs = jnp.concatenate([q4[:, g * A_HEAD_DIM:(g + 1) * A_HEAD_DIM] for g in range(A_GROUP)],
                                 axis=0)
            kw = kbuf[bi, kh, pl.ds(w0, WIN_KEYS), :]
            vw = vbuf[bi, kh, pl.ds(w0, WIN_KEYS), :]
            sg = _dot_nt(kw, qs) - bias_ref[variant, kh]
            sink = sink_ref[kh]
            m = jnp.maximum(jnp.max(sg, axis=0, keepdims=True), sink)
            p = jnp.exp(sg - m).astype(BF16)
            p_sink = jnp.where(tail_row0, jnp.exp(sink - m), 0.0).astype(BF16)
            oe = _dot_tn(jnp.concatenate([p, p_sink], axis=0),
                         jnp.concatenate([vw, v_tail], axis=0))
            ao = jnp.concatenate(
                [oe[g * CHUNK:(g + 1) * CHUNK, 0:A_HEAD_DIM]
                 * (1.0 / oe[g * CHUNK:(g + 1) * CHUNK, A_HEAD_DIM:A_HEAD_DIM + 1])
                 for g in range(A_GROUP)], axis=1)
            y_ref[rows, cols] = (ao * sgate_ref[rows, cols].astype(F32)).astype(BF16)

        dec = jnp.exp(glast_ref[n])
        for hh in range(B_HEADS):
            ks = slice(hh * B_DK, (hh + 1) * B_DK)
            vs = slice(hh * B_DV, (hh + 1) * B_DV)
            qg = qg_ref[rows, ks]
            a = jnp.where(tril, _dot_nt(qg, kg_ref[rows, ks]), 0.0).astype(BF16)
            vh = bv_ref[rows, vs]
            st = st_ref[bi, hh]
            o = _dot(a, vh) + _dot_nt(qg, st.astype(BF16))
            st_ref[bi, hh] = st * dec[:, ks] + _dot_tn(vh, kd_ref[rows, ks])
            bo = _rms_scale(o) * glag
            cols = slice(A_WIDTH + hh * B_DV, A_WIDTH + (hh + 1) * B_DV)
            y_ref[rows, cols] = (bo * sgate_ref[rows, cols].astype(F32)).astype(BF16)
        return carry

    lax.fori_loop(0, bb * tc, chunk_body, 0)

    out_ref[...] = x_ref[...] + _dot(y_ref[...], wout_ref[...])

    @pl.when(t == nt - 1)
    def _emit_state():
        for bi in range(bb):
            for hh in range(B_HEADS):
                sfin_ref[bi, hh] = st_ref[bi, hh].T


def _even_mix(q, k, v, kprev, vprev, qg, kg, kd, bv, glast, sgate, x2, bias, sink_rows, gla_g, w_out, s0,
              *, batch, seq):
    has_past = s0 is not None
    tile = ROW_TILE // 2 if has_past else ROW_TILE
    tb = min(seq, tile)
    bb = tile // tb
    nt = seq // tb
    nb = batch // bb
    assert bb == 1 or nt == 1
    row_spec = lambda w: pl.BlockSpec((tile, w), lambda b, t: (b * nt + t, 0))
    if has_past:
        prev_spec = pl.BlockSpec((bb * WIN_ROWS, A_KV_WIDTH), lambda b, t: (b, 0))
    else:
        per_b = seq // WIN_ROWS
        step = tb // WIN_ROWS
        prev_spec = pl.BlockSpec((WIN_ROWS, A_KV_WIDTH),
                                 lambda b, t: (b * per_b + jnp.maximum(t * step - 1, 0), 0))
    state_spec = pl.BlockSpec((bb, B_HEADS, B_DK, B_DV), lambda b, t: (b, 0, 0, 0))
    glast_spec = pl.BlockSpec((tile // CHUNK, 1, B_QK_WIDTH), lambda b, t: (b * nt + t, 0, 0))
    in_specs = [row_spec(A_WIDTH), row_spec(A_KV_WIDTH), row_spec(A_KV_WIDTH), prev_spec, prev_spec,
                row_spec(B_QK_WIDTH), row_spec(B_QK_WIDTH), row_spec(B_QK_WIDTH), row_spec(B_WIDTH),
                glast_spec, row_spec(MIX_EVEN), row_spec(D_MODEL),
                _const_spec(bias.shape), _const_spec(sink_rows.shape), _const_spec(gla_g.shape),
                _const_spec(w_out.shape)]
    args = [q, k, v, kprev, vprev, qg, kg, kd, bv, glast, sgate, x2, bias, sink_rows, gla_g, w_out]
    if has_past:
        in_specs.append(state_spec)
        args.append(s0)
    return pl.pallas_call(
        functools.partial(_even_mix_kernel, bb=bb, tb=tb, nt=nt, has_past=has_past),
        grid=(nb, nt),
        in_specs=in_specs,
        out_specs=[row_spec(D_MODEL), state_spec],
        out_shape=[jax.ShapeDtypeStruct((batch * seq, D_MODEL), F32),
                   jax.ShapeDtypeStruct((batch, B_HEADS, B_DK, B_DV), F32)],
        scratch_shapes=[pltpu.VMEM((bb, A_KV_HEADS, WIN_ROWS + tb, A_HEAD_DIM), BF16),
                        pltpu.VMEM((bb, A_KV_HEADS, WIN_ROWS + tb, 2 * A_HEAD_DIM), BF16),
                        pltpu.VMEM((bb, B_HEADS, B_DV, B_DK), F32),
                        pltpu.VMEM((tile, MIX_EVEN), BF16)],
        compiler_params=pltpu.CompilerParams(dimension_semantics=("arbitrary", "arbitrary"),
                                             vmem_limit_bytes=VMEM_LIMIT),
        name="even_mix",
    )(*args)


_PAIR = 2 * C_BLOCK
ODD_BATCH = SUBLANES
ODD_STEPS = ROW_TILE // ODD_BATCH
_HIST_ROWS = (CONV_W - 1) * ODD_BATCH


def _odd_kernel(x_ref, hist_ref, h0_ref, g_ref, win_ref, cw_ref, cb_ref, wra_ref, bra_ref,
                wri_ref, bri_ref, lam_ref, wout_ref, gfin_ref,
                out_ref, conv_ref, lru_ref,
                xs_ref, gate_ref, a_ref, b_ref, y_ref, hcar_ref, *, nt):
    t = pl.program_id(1)
    half = C_WIDTH // 2

    @pl.when(t == 0)
    def _init_state():
        hist = hist_ref[...].reshape(_HIST_ROWS, C_WIDTH)
        for k in range(2):
            xs_ref[k, 0:_HIST_ROWS, :] = hist[:, k * half:(k + 1) * half]
        hcar_ref[...] = h0_ref[...]

    x = jnp.transpose(x_ref[...], (1, 0, 2)).reshape(ROW_TILE, D_MODEL)
    h = (_rms_scale(x) * g_ref[...]).astype(BF16)

    def project_half(k):
        xs_ref[k, _HIST_ROWS:_HIST_ROWS + ROW_TILE, :] = _dot(h, win_ref[:, k * half:(k + 1) * half])

    project_half(0)

    soft_lam = lam_ref[...]
    soft_lam = jnp.maximum(-soft_lam, 0.0) + jnp.log1p(jnp.exp(-jnp.abs(soft_lam)))
    for m in range(C_BLOCKS // 2):
        if m == 0:
            project_half(1)
        else:
            g0 = (m - 1) * _PAIR * 2
            g1 = min(g0 + 2 * _PAIR, C_WIDTH)
            if g0 < C_WIDTH:
                gate_ref[:, g0:g1] = _dot(h, win_ref[:, C_WIDTH + g0:C_WIDTH + g1])
        c0 = m * _PAIR
        k, ck = divmod(c0, half)
        u = cb_ref[:, c0:c0 + _PAIR]
        for j in range(CONV_W):
            r = j * ODD_BATCH
            u = u + xs_ref[k, r:r + ROW_TILE, ck:ck + _PAIR] * cw_ref[j:j + 1, c0:c0 + _PAIR]
        ub = u.astype(BF16)
        lo, hi = ub[:, 0:MXU_DIM], ub[:, LANES:LANES + MXU_DIM]

        def gate_pre(w_ref, bias_ref):
            e = _dot(lo, w_ref[2 * m])
            o = _dot(hi, w_ref[2 * m + 1])
            full = jnp.concatenate([e[:, 0:LANES], e[:, LANES:] + o[:, 0:LANES], o[:, LANES:]], axis=1)
            return full + bias_ref[:, c0:c0 + _PAIR]

        rg = _sigmoid(gate_pre(wra_ref, bra_ref))
        ig = _sigmoid(gate_pre(wri_ref, bri_ref))
        log_a = -LRU_C * rg * soft_lam[:, c0:c0 + _PAIR]
        a = jnp.exp(log_a)
        a_ref[:, c0:c0 + _PAIR] = a
        b_ref[:, c0:c0 + _PAIR] = jnp.sqrt(jnp.tanh(-log_a) * (a * a + 1.0)) * ig * u

    pair = 2 * ODD_BATCH

    def step_body(i, hprev):
        r0 = pl.multiple_of(i * pair, pair)
        lo8, hi8, both = pl.ds(r0, ODD_BATCH), pl.ds(r0 + ODD_BATCH, ODD_BATCH), pl.ds(r0, pair)
        h1 = a_ref[lo8, :] * hprev + b_ref[lo8, :]
        h2 = a_ref[hi8, :] * h1 + b_ref[hi8, :]
        y_ref[both, :] = (jnp.concatenate([h1, h2], axis=0) * _silu(gate_ref[both, :])).astype(BF16)
        return h2

    hcar_ref[...] = lax.fori_loop(0, ODD_STEPS // 2, step_body, hcar_ref[...])
    for k in range(2):
        xs_ref[k, 0:_HIST_ROWS, :] = xs_ref[k, ROW_TILE:ROW_TILE + _HIST_ROWS, :]

    res = x + _dot(y_ref[...], wout_ref[...])
    out = _rms_scale(res) * gfin_ref[...]
    out_ref[...] = jnp.transpose(out.reshape(ODD_STEPS, ODD_BATCH, D_MODEL), (1, 0, 2))

    @pl.when(t == nt - 1)
    def _emit_state():
        for k in range(2):
            conv_ref[:, :, k * half:(k + 1) * half] = (
                xs_ref[k, 0:_HIST_ROWS, :].reshape(CONV_W - 1, ODD_BATCH, half))
        lru_ref[...] = hcar_ref[...]


def _odd_layer(x3, hist_t, h0, norm_g, w_in, conv_w, conv_b, w_ra, b_ra, w_ri, b_ri, lam, w_out, g_fin):
    batch, seq, _ = x3.shape
    nt = seq // ODD_STEPS
    nb = batch // ODD_BATCH
    n_hist = CONV_W - 1
    x_spec = pl.BlockSpec((ODD_BATCH, ODD_STEPS, D_MODEL), lambda b, t: (b, t, 0))
    hist_spec = pl.BlockSpec((n_hist, ODD_BATCH, C_WIDTH), lambda b, t: (0, b, 0))
    h_spec = pl.BlockSpec((ODD_BATCH, C_WIDTH), lambda b, t: (b, 0))
    consts = (norm_g, w_in, conv_w, conv_b, w_ra, b_ra, w_ri, b_ri, lam, w_out, g_fin)
    return pl.pallas_call(
        functools.partial(_odd_kernel, nt=nt),
        grid=(nb, nt),
        in_specs=[x_spec, hist_spec, h_spec] + [_const_spec(c.shape) for c in consts],
        out_specs=[x_spec, hist_spec, h_spec],
        out_shape=[jax.ShapeDtypeStruct((batch, seq, D_MODEL), F32),
                   jax.ShapeDtypeStruct((n_hist, batch, C_WIDTH), F32),
                   jax.ShapeDtypeStruct((batch, C_WIDTH), F32)],
        scratch_shapes=[pltpu.VMEM((2, _HIST_ROWS + ROW_TILE, C_WIDTH // 2), F32),
                        pltpu.VMEM((ROW_TILE, C_WIDTH), F32),
                        pltpu.VMEM((ROW_TILE, C_WIDTH), F32),
                        pltpu.VMEM((ROW_TILE, C_WIDTH), F32),
                        pltpu.VMEM((ROW_TILE, C_WIDTH), BF16),
                        pltpu.VMEM((ODD_BATCH, C_WIDTH), F32)],
        compiler_params=pltpu.CompilerParams(dimension_semantics=("arbitrary", "arbitrary"),
                                             vmem_limit_bytes=VMEM_LIMIT),
        name="odd_layer",
    )(x3, hist_t, h0, *consts)


def _bias_table(masked):
    slopes = 2.0 ** (-8.0 * jnp.arange(1, A_HEADS + 1, dtype=F32) / A_HEADS)
    dist = jnp.abs(WIN_ROWS + jnp.arange(CHUNK)[None, :] - jnp.arange(WIN_KEYS)[:, None]).astype(F32)
    alibi = slopes.reshape(A_KV_HEADS, 1, A_GROUP, 1) * dist[None, :, None, :]
    alibi = alibi.reshape(1, A_KV_HEADS, WIN_KEYS, A_GROUP * CHUNK)
    if not masked:
        return alibi
    first_chunk = jnp.arange(WIN_ROWS // CHUNK + 1)[:, None]
    key_pos = (first_chunk - WIN_ROWS // CHUNK) * CHUNK + jnp.arange(WIN_KEYS)[None, :]
    mask = jnp.where(key_pos < 0, F32(1e30), F32(0.0))
    return alibi + mask[:, None, :, None]


def _chunk_tri():
    r = np.arange(ROW_TILE)
    same_chunk = (r[:, None] // CHUNK) == (r[None, :] // CHUNK)
    return jnp.asarray(same_chunk & (r[:, None] >= r[None, :]), BF16)


def _pad_gate_blocks(w):
    pad = MXU_DIM - C_BLOCK
    even = jnp.pad(w[0::2], ((0, 0), (0, pad), (0, pad)))
    odd = jnp.pad(w[1::2], ((0, 0), (pad, 0), (pad, 0)))
    return jnp.stack([even, odd], axis=1).reshape(C_BLOCKS, MXU_DIM, MXU_DIM).astype(BF16)


def kernel(x_prompt, x_sample, cache_swa_k, cache_swa_v, state_gla, cache_conv, state_lru, norm_even, w_in_even, w_gate_lr, b_gate_lr, sinks, gla_norm, w_out_even, norm_odd, w_in_odd, conv_w, conv_b, w_rg_a, b_rg_a, w_rg_i, b_rg_i, lru_lambda, w_out_odd, norm_final):
    batch, seq, _ = x_prompt.shape
    dbatch, dseq, _ = x_sample.shape
    row = lambda v: v.reshape(1, -1)

    lr0 = _OFF_GATE
    w_e = w_in_even[0]
    w_main = jnp.concatenate([w_e[:, :lr0], w_e[:, lr0 + B_LOWRANK:]], axis=1).astype(BF16)
    w_lr1 = jnp.pad(w_e[:, lr0:lr0 + B_LOWRANK], ((0, 0), (0, LANES - B_LOWRANK))).astype(BF16)
    w_lr2 = jnp.pad(w_gate_lr[0], ((0, LANES - B_LOWRANK), (0, 0))).astype(BF16)
    tri = _chunk_tri()
    w_out_e = w_out_even[0].astype(BF16)
    even_consts = (row(norm_even[0]), w_main, w_lr1, w_lr2, row(b_gate_lr[0]), tri)
    sink_rows = jnp.repeat(sinks[0].reshape(A_KV_HEADS, 1, A_GROUP), CHUNK, axis=2)

    def even_layer(x, past):
        b, tl, _ = x.shape
        x2 = x.reshape(b * tl, D_MODEL)
        q, k, v, qg, kg, kd, bv, glast, sgate = _even_in(x2, *even_consts)
        if past is None:
            kprev, vprev, s0 = k, v, None
        else:
            kprev = past[0].reshape(b * WIN_ROWS, A_KV_WIDTH)
            vprev = past[1].reshape(b * WIN_ROWS, A_KV_WIDTH)
            s0 = past[2]
        out, s_new = _even_mix(q, k, v, kprev, vprev, qg, kg, kd, bv,
                               glast.reshape(-1, 1, B_QK_WIDTH), sgate, x2,
                               _bias_table(masked=past is None), sink_rows,
                               row(gla_norm[0]), w_out_e, s0, batch=b, seq=tl)
        k4 = k.reshape(b, tl, A_KV_HEADS, A_HEAD_DIM)
        v4 = v.reshape(b, tl, A_KV_HEADS, A_HEAD_DIM)
        return out, k4, v4, s_new

    xp, pk, pv, pg = even_layer(x_prompt, None)
    xs, sk, sv, sg = even_layer(x_sample, (cache_swa_k[0], cache_swa_v[0], state_gla[0]))

    odd_consts = (row(norm_odd[0]), w_in_odd[0].astype(BF16), conv_w[0], row(conv_b[0]),
                  _pad_gate_blocks(w_rg_a[0]), row(b_rg_a[0]), _pad_gate_blocks(w_rg_i[0]), row(b_rg_i[0]),
                  row(lru_lambda[0]), w_out_odd[0].astype(BF16), row(norm_final))
    zero_hist = jnp.zeros((CONV_W - 1, batch, C_WIDTH), F32)
    zero_h = jnp.zeros((batch, C_WIDTH), F32)
    yp, pc, plru = _odd_layer(xp.reshape(batch, seq, D_MODEL), zero_hist, zero_h, *odd_consts)
    ys, sc, slru = _odd_layer(xs.reshape(dbatch, dseq, D_MODEL), jnp.swapaxes(cache_conv[0], 0, 1),
                              state_lru[0], *odd_consts)

    return (yp, ys, pk[None, :, -WIN_ROWS:], pv[None, :, -WIN_ROWS:], pg[None],
            jnp.swapaxes(pc, 0, 1)[None], plru[None],
            sk[None], sv[None], sg[None], jnp.swapaxes(sc, 0, 1)[None], slru[None])
```

```python
import functools

import jax
import jax.numpy as jnp
import numpy as np
from jax import lax
from jax.experimental import pallas as pl
from jax.experimental.pallas import tpu as pltpu

F32 = jnp.float32
BF16 = jnp.bfloat16

D_MODEL = 1024
CHUNK = 64
EPS = 1e-6
A_HEADS = 16
A_KV_HEADS = 4
A_HEAD_DIM = 64
A_GROUP = A_HEADS // A_KV_HEADS
A_WIDTH = A_HEADS * A_HEAD_DIM
A_KV_WIDTH = A_KV_HEADS * A_HEAD_DIM
WIN_ROWS = 128
WIN_KEYS = WIN_ROWS + CHUNK
B_HEADS = 4
B_DK = 128
B_DV = 256
B_QK_WIDTH = B_HEADS * B_DK
B_WIDTH = B_HEADS * B_DV
B_LOWRANK = 16
B_GATE_NORM = 16.0
C_WIDTH = 1536
C_BLOCKS = 8
C_BLOCK = C_WIDTH // C_BLOCKS
CONV_W = 4
LRU_C = 8.0
MIX_EVEN = A_WIDTH + B_WIDTH

LANES = 128
SUBLANES = 8
MXU_DIM = 256
ROW_TILE = 512
VMEM_LIMIT = 48 * 1024 * 1024

_OFF_Q = 0
_OFF_K = _OFF_Q + A_WIDTH
_OFF_V = _OFF_K + A_KV_WIDTH
_OFF_BQ = _OFF_V + A_KV_WIDTH
_OFF_BK = _OFF_BQ + B_QK_WIDTH
_OFF_BV = _OFF_BK + B_QK_WIDTH
_OFF_GATE = _OFF_BV + B_WIDTH
_MAIN_WIDTH = _OFF_GATE + MIX_EVEN


def _const_spec(shape):
    nd = len(shape)
    return pl.BlockSpec(shape, lambda *_: (0,) * nd, pipeline_mode=pl.Buffered(1))


def _rms_scale(x):
    return x * lax.rsqrt(jnp.mean(x * x, axis=-1, keepdims=True) + EPS)


def _silu(x):
    return x * (1.0 / (1.0 + jnp.exp(-x)))


def _sigmoid(x):
    return 1.0 / (1.0 + jnp.exp(-x))


def _dot(a, b):
    return jnp.dot(a, b, preferred_element_type=F32)


def _dot_nt(a, b):
    return lax.dot_general(a, b, (((1,), (1,)), ((), ())), preferred_element_type=F32)


def _dot_tn(a, b):
    return lax.dot_general(a, b, (((0,), (0,)), ((), ())), preferred_element_type=F32)


def _even_in_kernel(x_ref, g_ref, w_ref, wlr1_ref, wlr2_ref, blr_ref, tri_ref,
                    q_ref, k_ref, v_ref, qg_ref, kg_ref, kd_ref, bv_ref, glast_ref, sgate_ref):
    h = (_rms_scale(x_ref[...]) * g_ref[...]).astype(BF16)

    def proj(off, width):
        return _dot(h, w_ref[:, off:off + width])

    q_ref[...] = (proj(_OFF_Q, A_WIDTH) * (A_HEAD_DIM ** -0.5)).astype(BF16)
    k_ref[...] = proj(_OFF_K, A_KV_WIDTH)
    v_ref[...] = proj(_OFF_V, A_KV_WIDTH)
    bv_ref[...] = proj(_OFF_BV, B_WIDTH).astype(BF16)
    sgate_ref[...] = _silu(proj(_OFF_GATE, MIX_EVEN)).astype(BF16)

    low = _dot(h, wlr1_ref[...])
    pre = _dot(low.astype(BF16), wlr2_ref[...]) + blr_ref[...]
    glog = (jnp.minimum(pre, 0.0) - jnp.log1p(jnp.exp(-jnp.abs(pre)))) * (1.0 / B_GATE_NORM)
    tri = tri_ref[...]
    g1 = glog.astype(BF16)
    r1 = glog - g1.astype(F32)
    g2 = r1.astype(BF16)
    g3 = (r1 - g2.astype(F32)).astype(BF16)
    gcum = _dot(tri, g1) + _dot(tri, g2) + _dot(tri, g3)

    bk = proj(_OFF_BK, B_QK_WIDTH)
    qg_ref[...] = (proj(_OFF_BQ, B_QK_WIDTH) * (B_DK ** -0.5) * jnp.exp(gcum)).astype(BF16)
    kg_ref[...] = (bk * jnp.exp(-gcum)).astype(BF16)
    for c in range(ROW_TILE // CHUNK):
        rows = slice(c * CHUNK, (c + 1) * CHUNK)
        glast = gcum[(c + 1) * CHUNK - 1:(c + 1) * CHUNK, :]
        glast_ref[c:c + 1, :] = glast
        kd_ref[rows, :] = (bk[rows] * jnp.exp(glast - gcum[rows])).astype(BF16)


def _even_in(x2, norm_g, w_main, w_lr1, w_lr2, b_lr, tri):
    rows = x2.shape[0]
    nsteps = rows // ROW_TILE
    row_spec = lambda w: pl.BlockSpec((ROW_TILE, w), lambda i: (i, 0))
    chunks = ROW_TILE // CHUNK
    out_widths = (A_WIDTH, A_KV_WIDTH, A_KV_WIDTH, B_QK_WIDTH, B_QK_WIDTH, B_QK_WIDTH, B_WIDTH,
                  B_QK_WIDTH, MIX_EVEN)
    out_dtypes = (BF16, F32, F32, BF16, BF16, BF16, BF16, F32, BF16)
    out_rows = [ROW_TILE] * 7 + [chunks, ROW_TILE]
    return pl.pallas_call(
        _even_in_kernel,
        grid=(nsteps,),
        in_specs=[row_spec(D_MODEL), _const_spec(norm_g.shape), _const_spec(w_main.shape),
                  _const_spec(w_lr1.shape), _const_spec(w_lr2.shape), _const_spec(b_lr.shape),
                  _const_spec(tri.shape)],
        out_specs=[pl.BlockSpec((r, w), lambda i: (i, 0)) for r, w in zip(out_rows, out_widths)],
        out_shape=[jax.ShapeDtypeStruct((nsteps * r, w), d)
                   for r, w, d in zip(out_rows, out_widths, out_dtypes)],
        compiler_params=pltpu.CompilerParams(dimension_semantics=("arbitrary",),
                                             vmem_limit_bytes=VMEM_LIMIT),
        name="even_in",
    )(x2, norm_g, w_main, w_lr1, w_lr2, b_lr, tri)


def _even_mix_kernel(*refs, bb, tb, nt, has_past):
    if has_past:
        (q_ref, kc_ref, vc_ref, kp_ref, vp_ref, qg_ref, kg_ref, kd_ref, bv_ref, glast_ref,
         sgate_ref, x_ref, bias_ref, sink_ref, glag_ref, wout_ref, s0_ref,
         out_ref, sfin_ref, kbuf, vbuf, st_ref, y_ref) = refs
    else:
        (q_ref, kc_ref, vc_ref, kp_ref, vp_ref, qg_ref, kg_ref, kd_ref, bv_ref, glast_ref,
         sgate_ref, x_ref, bias_ref, sink_ref, glag_ref, wout_ref,
         out_ref, sfin_ref, kbuf, vbuf, st_ref, y_ref) = refs
        s0_ref = None
    t = pl.program_id(1)
    tc = tb // CHUNK

    one_lane = jnp.where(lax.broadcasted_iota(jnp.int32, (WIN_ROWS + tb, A_HEAD_DIM), 1) == 0,
                         1.0, 0.0).astype(BF16)
    for bi in range(bb):
        for kh in range(A_KV_HEADS):
            cols = slice(kh * A_HEAD_DIM, (kh + 1) * A_HEAD_DIM)
            kbuf[bi, kh, 0:WIN_ROWS, :] = kp_ref[bi * WIN_ROWS:(bi + 1) * WIN_ROWS, cols].astype(BF16)
            kbuf[bi, kh, WIN_ROWS:WIN_ROWS + tb, :] = kc_ref[bi * tb:(bi + 1) * tb, cols].astype(BF16)
            vbuf[bi, kh, 0:WIN_ROWS, 0:A_HEAD_DIM] = (
                vp_ref[bi * WIN_ROWS:(bi + 1) * WIN_ROWS, cols].astype(BF16))
            vbuf[bi, kh, WIN_ROWS:WIN_ROWS + tb, 0:A_HEAD_DIM] = (
                vc_ref[bi * tb:(bi + 1) * tb, cols].astype(BF16))
            vbuf[bi, kh, :, A_HEAD_DIM:2 * A_HEAD_DIM] = one_lane

    @pl.when(t == 0)
    def _init_state():
        for bi in range(bb):
            for hh in range(B_HEADS):
                if has_past:
                    st_ref[bi, hh] = s0_ref[bi, hh].T
                else:
                    st_ref[bi, hh] = jnp.zeros((B_DV, B_DK), F32)

    tril = (lax.broadcasted_iota(jnp.int32, (CHUNK, CHUNK), 0)
            >= lax.broadcasted_iota(jnp.int32, (CHUNK, CHUNK), 1))
    glag = glag_ref[...]
    tail_rows = 16
    tail_row0 = lax.broadcasted_iota(jnp.int32, (tail_rows, A_GROUP * CHUNK), 0) == 0
    tail_iota = lax.broadcasted_iota(jnp.int32, (tail_rows, 2 * A_HEAD_DIM), 0)
    tail_lane = lax.broadcasted_iota(jnp.int32, (tail_rows, 2 * A_HEAD_DIM), 1)
    v_tail = jnp.where((tail_iota == 0) & (tail_lane == A_HEAD_DIM), 1.0, 0.0).astype(BF16)

    def chunk_body(n, carry):
        bi = n // tc
        c = n % tc
        r0 = pl.multiple_of(n * CHUNK, CHUNK)
        w0 = pl.multiple_of(c * CHUNK, CHUNK)
        rows = pl.ds(r0, CHUNK)

        variant = 0 if has_past else jnp.minimum(t * tc + c, WIN_ROWS // CHUNK)
        def attn_scores(kh):
            q4 = q_ref[rows, kh * A_GROUP * A_HEAD_DIM:(kh + 1) * A_GROUP * A_HEAD_DIM]
            qs = jnp.concatenate([q4[:, g * A_HEAD_DIM:(g + 1) * A_HEAD_DIM] for g in range(A_GROUP)],
                                 axis=0)
            kw = kbuf[bi, kh, pl.ds(w0, WIN_KEYS), :]
            return _dot_nt(kw, qs) - bias_ref[variant, kh]

        def attn_finish(kh, sg):
            cols = slice(kh * A_GROUP * A_HEAD_DIM, (kh + 1) * A_GROUP * A_HEAD_DIM)
            vw = vbuf[bi, kh, pl.ds(w0, WIN_KEYS), :]
            sink = sink_ref[kh]
            m = jnp.maximum(jnp.max(sg, axis=0, keepdims=True), sink)
            p = jnp.exp(sg - m).astype(BF16)
            p_sink = jnp.where(tail_row0, jnp.exp(sink - m), 0.0).astype(BF16)
            oe = _dot_tn(jnp.concatenate([p, p_sink], axis=0),
                         jnp.concatenate([vw, v_tail], axis=0))
            ao = jnp.concatenate(
                [oe[g * CHUNK:(g + 1) * CHUNK, 0:A_HEAD_DIM]
                 * (1.0 / oe[g * CHUNK:(g + 1) * CHUNK, A_HEAD_DIM:A_HEAD_DIM + 1])
                 for g in range(A_GROUP)], axis=1)
            y_ref[rows, cols] = (ao * sgate_ref[rows, cols].astype(F32)).astype(BF16)

        dec = jnp.exp(glast_ref[n])

        def gla_head(hh):
            ks = slice(hh * B_DK, (hh + 1) * B_DK)
            vs = slice(hh * B_DV, (hh + 1) * B_DV)
            qg = qg_ref[rows, ks]
            a = jnp.where(tril, _dot_nt(qg, kg_ref[rows, ks]), 0.0).astype(BF16)
            vh = bv_ref[rows, vs]
            st = st_ref[bi, hh]
            o = _dot(a, vh) + _dot_nt(qg, st.astype(BF16))
            st_ref[bi, hh] = st * dec[:, ks] + _dot_tn(vh, kd_ref[rows, ks])
            bo = _rms_scale(o) * glag
            cols = slice(A_WIDTH + hh * B_DV, A_WIDTH + (hh + 1) * B_DV)
            y_ref[rows, cols] = (bo * sgate_ref[rows, cols].astype(F32)).astype(BF16)

        scores = attn_scores(0)
        for kh in range(A_KV_HEADS):
            nxt = attn_scores(kh + 1) if kh + 1 < A_KV_HEADS else None
            gla_head(kh)
            attn_finish(kh, scores)
            scores = nxt
        return carry

    lax.fori_loop(0, bb * tc, chunk_body, 0, unroll=4)

    out_ref[...] = x_ref[...] + _dot(y_ref[...], wout_ref[...])

    @pl.when(t == nt - 1)
    def _emit_state():
        for bi in range(bb):
            for hh in range(B_HEADS):
                sfin_ref[bi, hh] = st_ref[bi, hh].T


def _even_mix(q, k, v, kprev, vprev, qg, kg, kd, bv, glast, sgate, x2, bias, sink_rows, gla_g, w_out, s0,
              *, batch, seq):
    has_past = s0 is not None
    tile = ROW_TILE // 2 if has_past else ROW_TILE
    tb = min(seq, tile)
    bb = tile // tb
    nt = seq // tb
    nb = batch // bb
    assert bb == 1 or nt == 1
    row_spec = lambda w: pl.BlockSpec((tile, w), lambda b, t: (b * nt + t, 0))
    if has_past:
        prev_spec = pl.BlockSpec((bb * WIN_ROWS, A_KV_WIDTH), lambda b, t: (b, 0))
    else:
        per_b = seq // WIN_ROWS
        step = tb // WIN_ROWS
        prev_spec = pl.BlockSpec((WIN_ROWS, A_KV_WIDTH),
                                 lambda b, t: (b * per_b + jnp.maximum(t * step - 1, 0), 0))
    state_spec = pl.BlockSpec((bb, B_HEADS, B_DK, B_DV), lambda b, t: (b, 0, 0, 0))
    glast_spec = pl.BlockSpec((tile // CHUNK, 1, B_QK_WIDTH), lambda b, t: (b * nt + t, 0, 0))
    in_specs = [row_spec(A_WIDTH), row_spec(A_KV_WIDTH), row_spec(A_KV_WIDTH), prev_spec, prev_spec,
                row_spec(B_QK_WIDTH), row_spec(B_QK_WIDTH), row_spec(B_QK_WIDTH), row_spec(B_WIDTH),
                glast_spec, row_spec(MIX_EVEN), row_spec(D_MODEL),
                _const_spec(bias.shape), _const_spec(sink_rows.shape), _const_spec(gla_g.shape),
                _const_spec(w_out.shape)]
    args = [q, k, v, kprev, vprev, qg, kg, kd, bv, glast, sgate, x2, bias, sink_rows, gla_g, w_out]
    if has_past:
        in_specs.append(state_spec)
        args.append(s0)
    return pl.pallas_call(
        functools.partial(_even_mix_kernel, bb=bb, tb=tb, nt=nt, has_past=has_past),
        grid=(nb, nt),
        in_specs=in_specs,
        out_specs=[row_spec(D_MODEL), state_spec],
        out_shape=[jax.ShapeDtypeStruct((batch * seq, D_MODEL), F32),
                   jax.ShapeDtypeStruct((batch, B_HEADS, B_DK, B_DV), F32)],
        scratch_shapes=[pltpu.VMEM((bb, A_KV_HEADS, WIN_ROWS + tb, A_HEAD_DIM), BF16),
                        pltpu.VMEM((bb, A_KV_HEADS, WIN_ROWS + tb, 2 * A_HEAD_DIM), BF16),
                        pltpu.VMEM((bb, B_HEADS, B_DV, B_DK), F32),
                        pltpu.VMEM((tile, MIX_EVEN), BF16)],
        compiler_params=pltpu.CompilerParams(dimension_semantics=("arbitrary", "arbitrary"),
                                             vmem_limit_bytes=VMEM_LIMIT),
        name="even_mix",
    )(*args)


_PAIR = 2 * C_BLOCK
ODD_BATCH = SUBLANES
ODD_STEPS = ROW_TILE // ODD_BATCH
_HIST_ROWS = (CONV_W - 1) * ODD_BATCH


def _odd_kernel(x_ref, hist_ref, h0_ref, g_ref, win_ref, cw_ref, cb_ref, wra_ref, bra_ref,
                wri_ref, bri_ref, lam_ref, wout_ref, gfin_ref,
                out_ref, conv_ref, lru_ref,
                xs_ref, gate_ref, a_ref, b_ref, y_ref, hcar_ref, *, nt):
    t = pl.program_id(1)
    half = C_WIDTH // 2

    @pl.when(t == 0)
    def _init_state():
        hist = hist_ref[...].reshape(_HIST_ROWS, C_WIDTH)
        for k in range(2):
            xs_ref[k, 0:_HIST_ROWS, :] = hist[:, k * half:(k + 1) * half]
        hcar_ref[...] = h0_ref[...]

    x = jnp.transpose(x_ref[...], (1, 0, 2)).reshape(ROW_TILE, D_MODEL)
    h = (_rms_scale(x) * g_ref[...]).astype(BF16)

    def project_half(k):
        xs_ref[k, _HIST_ROWS:_HIST_ROWS + ROW_TILE, :] = _dot(h, win_ref[:, k * half:(k + 1) * half])

    project_half(0)

    soft_lam = lam_ref[...]
    soft_lam = jnp.maximum(-soft_lam, 0.0) + jnp.log1p(jnp.exp(-jnp.abs(soft_lam)))
    for m in range(C_BLOCKS // 2):
        if m == 0:
            project_half(1)
        else:
            g0 = (m - 1) * _PAIR * 2
            g1 = min(g0 + 2 * _PAIR, C_WIDTH)
            if g0 < C_WIDTH:
                gate_ref[:, g0:g1] = _dot(h, win_ref[:, C_WIDTH + g0:C_WIDTH + g1])
        c0 = m * _PAIR
        k, ck = divmod(c0, half)
        u = cb_ref[:, c0:c0 + _PAIR]
        for j in range(CONV_W):
            r = j * ODD_BATCH
            u = u + xs_ref[k, r:r + ROW_TILE, ck:ck + _PAIR] * cw_ref[j:j + 1, c0:c0 + _PAIR]
        ub = u.astype(BF16)
        lo, hi = ub[:, 0:MXU_DIM], ub[:, LANES:LANES + MXU_DIM]

        def gate_pre(w_ref, bias_ref):
            e = _dot(lo, w_ref[2 * m])
            o = _dot(hi, w_ref[2 * m + 1])
            full = jnp.concatenate([e[:, 0:LANES], e[:, LANES:] + o[:, 0:LANES], o[:, LANES:]], axis=1)
            return full + bias_ref[:, c0:c0 + _PAIR]

        rg = _sigmoid(gate_pre(wra_ref, bra_ref))
        ig = _sigmoid(gate_pre(wri_ref, bri_ref))
        log_a = -LRU_C * rg * soft_lam[:, c0:c0 + _PAIR]
        a = jnp.exp(log_a)
        a_ref[:, c0:c0 + _PAIR] = a
        b_ref[:, c0:c0 + _PAIR] = jnp.sqrt(jnp.tanh(-log_a) * (a * a + 1.0)) * ig * u

    pair = 2 * ODD_BATCH

    def step_body(i, hprev):
        r0 = pl.multiple_of(i * pair, pair)
        lo8, hi8, both = pl.ds(r0, ODD_BATCH), pl.ds(r0 + ODD_BATCH, ODD_BATCH), pl.ds(r0, pair)
        h1 = a_ref[lo8, :] * hprev + b_ref[lo8, :]
        h2 = a_ref[hi8, :] * h1 + b_ref[hi8, :]
        y_ref[both, :] = (jnp.concatenate([h1, h2], axis=0) * _silu(gate_ref[both, :])).astype(BF16)
        return h2

    hcar_ref[...] = lax.fori_loop(0, ODD_STEPS // 2, step_body, hcar_ref[...])
    for k in range(2):
        xs_ref[k, 0:_HIST_ROWS, :] = xs_ref[k, ROW_TILE:ROW_TILE + _HIST_ROWS, :]

    res = x + _dot(y_ref[...], wout_ref[...])
    out = _rms_scale(res) * gfin_ref[...]
    out_ref[...] = jnp.transpose(out.reshape(ODD_STEPS, ODD_BATCH, D_MODEL), (1, 0, 2))

    @pl.when(t == nt - 1)
    def _emit_state():
        for k in range(2):
            conv_ref[:, :, k * half:(k + 1) * half] = (
                xs_ref[k, 0:_HIST_ROWS, :].reshape(CONV_W - 1, ODD_BATCH, half))
        lru_ref[...] = hcar_ref[...]


def _odd_layer(x3, hist_t, h0, norm_g, w_in, conv_w, conv_b, w_ra, b_ra, w_ri, b_ri, lam, w_out, g_fin):
    batch, seq, _ = x3.shape
    nt = seq // ODD_STEPS
    nb = batch // ODD_BATCH
    n_hist = CONV_W - 1
    x_spec = pl.BlockSpec((ODD_BATCH, ODD_STEPS, D_MODEL), lambda b, t: (b, t, 0))
    hist_spec = pl.BlockSpec((n_hist, ODD_BATCH, C_WIDTH), lambda b, t: (0, b, 0))
    h_spec = pl.BlockSpec((ODD_BATCH, C_WIDTH), lambda b, t: (b, 0))
    consts = (norm_g, w_in, conv_w, conv_b, w_ra, b_ra, w_ri, b_ri, lam, w_out, g_fin)
    return pl.pallas_call(
        functools.partial(_odd_kernel, nt=nt),
        grid=(nb, nt),
        in_specs=[x_spec, hist_spec, h_spec] + [_const_spec(c.shape) for c in consts],
        out_specs=[x_spec, hist_spec, h_spec],
        out_shape=[jax.ShapeDtypeStruct((batch, seq, D_MODEL), F32),
                   jax.ShapeDtypeStruct((n_hist, batch, C_WIDTH), F32),
                   jax.ShapeDtypeStruct((batch, C_WIDTH), F32)],
        scratch_shapes=[pltpu.VMEM((2, _HIST_ROWS + ROW_TILE, C_WIDTH // 2), F32),
                        pltpu.VMEM((ROW_TILE, C_WIDTH), F32),
                        pltpu.VMEM((ROW_TILE, C_WIDTH), F32),
                        pltpu.VMEM((ROW_TILE, C_WIDTH), F32),
                        pltpu.VMEM((ROW_TILE, C_WIDTH), BF16),
                        pltpu.VMEM((ODD_BATCH, C_WIDTH), F32)],
        compiler_params=pltpu.CompilerParams(dimension_semantics=("arbitrary", "arbitrary"),
                                             vmem_limit_bytes=VMEM_LIMIT),
        name="odd_layer",
    )(x3, hist_t, h0, *consts)


def _bias_table(masked):
    slopes = 2.0 ** (-8.0 * jnp.arange(1, A_HEADS + 1, dtype=F32) / A_HEADS)
    dist = jnp.abs(WIN_ROWS + jnp.arange(CHUNK)[None, :] - jnp.arange(WIN_KEYS)[:, None]).astype(F32)
    alibi = slopes.reshape(A_KV_HEADS, 1, A_GROUP, 1) * dist[None, :, None, :]
    alibi = alibi.reshape(1, A_KV_HEADS, WIN_KEYS, A_GROUP * CHUNK)
    if not masked:
        return alibi
    first_chunk = jnp.arange(WIN_ROWS // CHUNK + 1)[:, None]
    key_pos = (first_chunk - WIN_ROWS // CHUNK) * CHUNK + jnp.arange(WIN_KEYS)[None, :]
    mask = jnp.where(key_pos < 0, F32(1e30), F32(0.0))
    return alibi + mask[:, None, :, None]


def _chunk_tri():
    r = np.arange(ROW_TILE)
    same_chunk = (r[:, None] // CHUNK) == (r[None, :] // CHUNK)
    return jnp.asarray(same_chunk & (r[:, None] >= r[None, :]), BF16)


def _pad_gate_blocks(w):
    pad = MXU_DIM - C_BLOCK
    even = jnp.pad(w[0::2], ((0, 0), (0, pad), (0, pad)))
    odd = jnp.pad(w[1::2], ((0, 0), (pad, 0), (pad, 0)))
    return jnp.stack([even, odd], axis=1).reshape(C_BLOCKS, MXU_DIM, MXU_DIM).astype(BF16)


def kernel(x_prompt, x_sample, cache_swa_k, cache_swa_v, state_gla, cache_conv, state_lru, norm_even, w_in_even, w_gate_lr, b_gate_lr, sinks, gla_norm, w_out_even, norm_odd, w_in_odd, conv_w, conv_b, w_rg_a, b_rg_a, w_rg_i, b_rg_i, lru_lambda, w_out_odd, norm_final):
    batch, seq, _ = x_prompt.shape
    dbatch, dseq, _ = x_sample.shape
    row = lambda v: v.reshape(1, -1)

    lr0 = _OFF_GATE
    w_e = w_in_even[0]
    w_main = jnp.concatenate([w_e[:, :lr0], w_e[:, lr0 + B_LOWRANK:]], axis=1).astype(BF16)
    w_lr1 = jnp.pad(w_e[:, lr0:lr0 + B_LOWRANK], ((0, 0), (0, LANES - B_LOWRANK))).astype(BF16)
    w_lr2 = jnp.pad(w_gate_lr[0], ((0, LANES - B_LOWRANK), (0, 0))).astype(BF16)
    tri = _chunk_tri()
    w_out_e = w_out_even[0].astype(BF16)
    even_consts = (row(norm_even[0]), w_main, w_lr1, w_lr2, row(b_gate_lr[0]), tri)
    sink_rows = jnp.repeat(sinks[0].reshape(A_KV_HEADS, 1, A_GROUP), CHUNK, axis=2)

    def even_layer(x, past):
        b, tl, _ = x.shape
        x2 = x.reshape(b * tl, D_MODEL)
        q, k, v, qg, kg, kd, bv, glast, sgate = _even_in(x2, *even_consts)
        if past is None:
            kprev, vprev, s0 = k, v, None
        else:
            kprev = past[0].reshape(b * WIN_ROWS, A_KV_WIDTH)
            vprev = past[1].reshape(b * WIN_ROWS, A_KV_WIDTH)
            s0 = past[2]
        out, s_new = _even_mix(q, k, v, kprev, vprev, qg, kg, kd, bv,
                               glast.reshape(-1, 1, B_QK_WIDTH), sgate, x2,
                               _bias_table(masked=past is None), sink_rows,
                               row(gla_norm[0]), w_out_e, s0, batch=b, seq=tl)
        k4 = k.reshape(b, tl, A_KV_HEADS, A_HEAD_DIM)
        v4 = v.reshape(b, tl, A_KV_HEADS, A_HEAD_DIM)
        return out, k4, v4, s_new

    xp, pk, pv, pg = even_layer(x_prompt, None)
    xs, sk, sv, sg = even_layer(x_sample, (cache_swa_k[0], cache_swa_v[0], state_gla[0]))

    odd_consts = (row(norm_odd[0]), w_in_odd[0].astype(BF16), conv_w[0], row(conv_b[0]),
                  _pad_gate_blocks(w_rg_a[0]), row(b_rg_a[0]), _pad_gate_blocks(w_rg_i[0]), row(b_rg_i[0]),
                  row(lru_lambda[0]), w_out_odd[0].astype(BF16), row(norm_final))
    zero_hist = jnp.zeros((CONV_W - 1, batch, C_WIDTH), F32)
    zero_h = jnp.zeros((batch, C_WIDTH), F32)
    yp, pc, plru = _odd_layer(xp.reshape(batch, seq, D_MODEL), zero_hist, zero_h, *odd_consts)
    ys, sc, slru = _odd_layer(xs.reshape(dbatch, dseq, D_MODEL), jnp.swapaxes(cache_conv[0], 0, 1),
                              state_lru[0], *odd_consts)

    return (yp, ys, pk[None, :, -WIN_ROWS:], pv[None, :, -WIN_ROWS:], pg[None],
            jnp.swapaxes(pc, 0, 1)[None], plru[None],
            sk[None], sv[None], sg[None], jnp.swapaxes(sc, 0, 1)[None], slru[None])
```

```python
import functools

import jax
import jax.numpy as jnp
import numpy as np
from jax import lax
from jax.experimental import pallas as pl
from jax.experimental.pallas import tpu as pltpu

F32 = jnp.float32
BF16 = jnp.bfloat16

D_MODEL = 1024
CHUNK = 64
EPS = 1e-6
A_HEADS = 16
A_KV_HEADS = 4
A_HEAD_DIM = 64
A_GROUP = A_HEADS // A_KV_HEADS
A_WIDTH = A_HEADS * A_HEAD_DIM
A_KV_WIDTH = A_KV_HEADS * A_HEAD_DIM
WIN_ROWS = 128
WIN_KEYS = WIN_ROWS + CHUNK
B_HEADS = 4
B_DK = 128
B_DV = 256
B_QK_WIDTH = B_HEADS * B_DK
B_WIDTH = B_HEADS * B_DV
B_LOWRANK = 16
B_GATE_NORM = 16.0
C_WIDTH = 1536
C_BLOCKS = 8
C_BLOCK = C_WIDTH // C_BLOCKS
CONV_W = 4
LRU_C = 8.0
MIX_EVEN = A_WIDTH + B_WIDTH

LANES = 128
SUBLANES = 8
MXU_DIM = 256
ROW_TILE = 512
VMEM_LIMIT = 48 * 1024 * 1024

_OFF_Q = 0
_OFF_K = _OFF_Q + A_WIDTH
_OFF_V = _OFF_K + A_KV_WIDTH
_OFF_BQ = _OFF_V + A_KV_WIDTH
_OFF_BK = _OFF_BQ + B_QK_WIDTH
_OFF_BV = _OFF_BK + B_QK_WIDTH
_OFF_GATE = _OFF_BV + B_WIDTH
_MAIN_WIDTH = _OFF_GATE + MIX_EVEN


def _const_spec(shape):
    nd = len(shape)
    return pl.BlockSpec(shape, lambda *_: (0,) * nd, pipeline_mode=pl.Buffered(1))


def _rms_scale(x):
    return x * lax.rsqrt(jnp.mean(x * x, axis=-1, keepdims=True) + EPS)


def _silu(x):
    return x * (1.0 / (1.0 + jnp.exp(-x)))


def _sigmoid(x):
    return 1.0 / (1.0 + jnp.exp(-x))


def _dot(a, b):
    return jnp.dot(a, b, preferred_element_type=F32)


def _dot_nt(a, b):
    return lax.dot_general(a, b, (((1,), (1,)), ((), ())), preferred_element_type=F32)


def _dot_tn(a, b):
    return lax.dot_general(a, b, (((0,), (0,)), ((), ())), preferred_element_type=F32)


def _even_in_kernel(x_ref, g_ref, w_ref, wlr1_ref, wlr2_ref, blr_ref, tri_ref,
                    q_ref, k_ref, v_ref, qg_ref, kg_ref, kd_ref, bv_ref, glast_ref, sgate_ref):
    h = (_rms_scale(x_ref[...]) * g_ref[...]).astype(BF16)

    def proj(off, width):
        return _dot(h, w_ref[:, off:off + width])

    q_ref[...] = (proj(_OFF_Q, A_WIDTH) * (A_HEAD_DIM ** -0.5)).astype(BF16)
    k_ref[...] = proj(_OFF_K, A_KV_WIDTH)
    v_ref[...] = proj(_OFF_V, A_KV_WIDTH)
    bv_ref[...] = proj(_OFF_BV, B_WIDTH).astype(BF16)
    sgate_ref[...] = _silu(proj(_OFF_GATE, MIX_EVEN)).astype(BF16)

    low = _dot(h, wlr1_ref[...])
    pre = _dot(low.astype(BF16), wlr2_ref[...]) + blr_ref[...]
    glog = (jnp.minimum(pre, 0.0) - jnp.log1p(jnp.exp(-jnp.abs(pre)))) * (1.0 / B_GATE_NORM)
    tri = tri_ref[...]
    g1 = glog.astype(BF16)
    r1 = glog - g1.astype(F32)
    g2 = r1.astype(BF16)
    g3 = (r1 - g2.astype(F32)).astype(BF16)
    gcum = _dot(tri, g1) + _dot(tri, g2) + _dot(tri, g3)

    bk = proj(_OFF_BK, B_QK_WIDTH)
    qg_ref[...] = (proj(_OFF_BQ, B_QK_WIDTH) * (B_DK ** -0.5) * jnp.exp(gcum)).astype(BF16)
    kg_ref[...] = (bk * jnp.exp(-gcum)).astype(BF16)
    for c in range(ROW_TILE // CHUNK):
        rows = slice(c * CHUNK, (c + 1) * CHUNK)
        glast = gcum[(c + 1) * CHUNK - 1:(c + 1) * CHUNK, :]
        glast_ref[c:c + 1, :] = glast
        kd_ref[rows, :] = (bk[rows] * jnp.exp(glast - gcum[rows])).astype(BF16)


def _even_in(x2, norm_g, w_main, w_lr1, w_lr2, b_lr, tri):
    rows = x2.shape[0]
    nsteps = rows // ROW_TILE
    row_spec = lambda w: pl.BlockSpec((ROW_TILE, w), lambda i: (i, 0))
    chunks = ROW_TILE // CHUNK
    out_widths = (A_WIDTH, A_KV_WIDTH, A_KV_WIDTH, B_QK_WIDTH, B_QK_WIDTH, B_QK_WIDTH, B_WIDTH,
                  B_QK_WIDTH, MIX_EVEN)
    out_dtypes = (BF16, F32, F32, BF16, BF16, BF16, BF16, F32, BF16)
    out_rows = [ROW_TILE] * 7 + [chunks, ROW_TILE]
    return pl.pallas_call(
        _even_in_kernel,
        grid=(nsteps,),
        in_specs=[row_spec(D_MODEL), _const_spec(norm_g.shape), _const_spec(w_main.shape),
                  _const_spec(w_lr1.shape), _const_spec(w_lr2.shape), _const_spec(b_lr.shape),
                  _const_spec(tri.shape)],
        out_specs=[pl.BlockSpec((r, w), lambda i: (i, 0)) for r, w in zip(out_rows, out_widths)],
        out_shape=[jax.ShapeDtypeStruct((nsteps * r, w), d)
                   for r, w, d in zip(out_rows, out_widths, out_dtypes)],
        compiler_params=pltpu.CompilerParams(dimension_semantics=("arbitrary",),
                                             vmem_limit_bytes=VMEM_LIMIT),
        name="even_in",
    )(x2, norm_g, w_main, w_lr1, w_lr2, b_lr, tri)


def _even_mix_kernel(*refs, bb, tb, nt, has_past):
    if has_past:
        (q_ref, kc_ref, vc_ref, kp_ref, vp_ref, qg_ref, kg_ref, kd_ref, bv_ref, glast_ref,
         sgate_ref, x_ref, bias_ref, sink_ref, glag_ref, wout_ref, s0_ref,
         out_ref, sfin_ref, kbuf, vbuf, st_ref, y_ref) = refs
    else:
        (q_ref, kc_ref, vc_ref, kp_ref, vp_ref, qg_ref, kg_ref, kd_ref, bv_ref, glast_ref,
         sgate_ref, x_ref, bias_ref, sink_ref, glag_ref, wout_ref,
         out_ref, sfin_ref, kbuf, vbuf, st_ref, y_ref) = refs
        s0_ref = None
    t = pl.program_id(1)
    tc = tb // CHUNK

    one_lane = jnp.where(lax.broadcasted_iota(jnp.int32, (WIN_ROWS + tb, A_HEAD_DIM), 1) == 0,
                         1.0, 0.0).astype(BF16)
    for bi in range(bb):
        for kh in range(A_KV_HEADS):
            cols = slice(kh * A_HEAD_DIM, (kh + 1) * A_HEAD_DIM)
            kbuf[bi, kh, 0:WIN_ROWS, :] = kp_ref[bi * WIN_ROWS:(bi + 1) * WIN_ROWS, cols].astype(BF16)
            kbuf[bi, kh, WIN_ROWS:WIN_ROWS + tb, :] = kc_ref[bi * tb:(bi + 1) * tb, cols].astype(BF16)
            vbuf[bi, kh, 0:WIN_ROWS, 0:A_HEAD_DIM] = (
                vp_ref[bi * WIN_ROWS:(bi + 1) * WIN_ROWS, cols].astype(BF16))
            vbuf[bi, kh, WIN_ROWS:WIN_ROWS + tb, 0:A_HEAD_DIM] = (
                vc_ref[bi * tb:(bi + 1) * tb, cols].astype(BF16))
            vbuf[bi, kh, :, A_HEAD_DIM:2 * A_HEAD_DIM] = one_lane

    @pl.when(t == 0)
    def _init_state():
        for bi in range(bb):
            for hh in range(B_HEADS):
                if has_past:
                    st_ref[bi, hh] = s0_ref[bi, hh].T
                else:
                    st_ref[bi, hh] = jnp.zeros((B_DV, B_DK), F32)

    tril = (lax.broadcasted_iota(jnp.int32, (CHUNK, CHUNK), 0)
            >= lax.broadcasted_iota(jnp.int32, (CHUNK, CHUNK), 1))
    glag = glag_ref[...]
    tail_rows = 16
    tail_row0 = lax.broadcasted_iota(jnp.int32, (tail_rows, A_GROUP * CHUNK), 0) == 0
    tail_iota = lax.broadcasted_iota(jnp.int32, (tail_rows, 2 * A_HEAD_DIM), 0)
    tail_lane = lax.broadcasted_iota(jnp.int32, (tail_rows, 2 * A_HEAD_DIM), 1)
    v_tail = jnp.where((tail_iota == 0) & (tail_lane == A_HEAD_DIM), 1.0, 0.0).astype(BF16)

    def chunk_body(n, carry):
        bi = n // tc
        c = n % tc
        r0 = pl.multiple_of(n * CHUNK, CHUNK)
        w0 = pl.multiple_of(c * CHUNK, CHUNK)
        rows = pl.ds(r0, CHUNK)

        variant = 0 if has_past else jnp.minimum(t * tc + c, WIN_ROWS // CHUNK)
        def attn_scores(kh):
            q4 = q_ref[rows, kh * A_GROUP * A_HEAD_DIM:(kh + 1) * A_GROUP * A_HEAD_DIM]
            qs = jnp.concatenate([q4[:, g * A_HEAD_DIM:(g + 1) * A_HEAD_DIM] for g in range(A_GROUP)],
                                 axis=0)
            kw = kbuf[bi, kh, pl.ds(w0, WIN_KEYS), :]
            return _dot_nt(kw, qs) - bias_ref[variant, kh]

        def attn_finish(kh, sg):
            cols = slice(kh * A_GROUP * A_HEAD_DIM, (kh + 1) * A_GROUP * A_HEAD_DIM)
            vw = vbuf[bi, kh, pl.ds(w0, WIN_KEYS), :]
            sink = sink_ref[kh]
            m = jnp.maximum(jnp.max(sg, axis=0, keepdims=True), sink)
            p = jnp.exp(sg - m).astype(BF16)
            p_sink = jnp.where(tail_row0, jnp.exp(sink - m), 0.0).astype(BF16)
            oe = _dot_tn(jnp.concatenate([p, p_sink], axis=0),
                         jnp.concatenate([vw, v_tail], axis=0))
            ao = jnp.concatenate(
                [oe[g * CHUNK:(g + 1) * CHUNK, 0:A_HEAD_DIM]
                 * (1.0 / oe[g * CHUNK:(g + 1) * CHUNK, A_HEAD_DIM:A_HEAD_DIM + 1])
                 for g in range(A_GROUP)], axis=1)
            y_ref[rows, cols] = (ao * sgate_ref[rows, cols].astype(F32)).astype(BF16)

        dec = jnp.exp(glast_ref[n])

        def gla_head(hh):
            ks = slice(hh * B_DK, (hh + 1) * B_DK)
            vs = slice(hh * B_DV, (hh + 1) * B_DV)
            qg = qg_ref[rows, ks]
            a = jnp.where(tril, _dot_nt(qg, kg_ref[rows, ks]), 0.0).astype(BF16)
            vh = bv_ref[rows, vs]
            st = st_ref[bi, hh]
            o = _dot(a, vh) + _dot_nt(qg, st.astype(BF16))
            st_ref[bi, hh] = st * dec[:, ks] + _dot_tn(vh, kd_ref[rows, ks])
            bo = _rms_scale(o) * glag
            cols = slice(A_WIDTH + hh * B_DV, A_WIDTH + (hh + 1) * B_DV)
            y_ref[rows, cols] = (bo * sgate_ref[rows, cols].astype(F32)).astype(BF16)

        scores = attn_scores(0)
        for kh in range(A_KV_HEADS):
            nxt = attn_scores(kh + 1) if kh + 1 < A_KV_HEADS else None
            gla_head(kh)
            attn_finish(kh, scores)
            scores = nxt
        return carry

    lax.fori_loop(0, bb * tc, chunk_body, 0, unroll=4)

    out_ref[...] = x_ref[...] + _dot(y_ref[...], wout_ref[...])

    @pl.when(t == nt - 1)
    def _emit_state():
        for bi in range(bb):
            for hh in range(B_HEADS):
                sfin_ref[bi, hh] = st_ref[bi, hh].T


def _even_mix(q, k, v, kprev, vprev, qg, kg, kd, bv, glast, sgate, x2, bias, sink_rows, gla_g, w_out, s0,
              *, batch, seq):
    has_past = s0 is not None
    tile = ROW_TILE // 2 if has_past else ROW_TILE
    tb = min(seq, tile)
    bb = tile // tb
    nt = seq // tb
    nb = batch // bb
    assert bb == 1 or nt == 1
    row_spec = lambda w: pl.BlockSpec((tile, w), lambda b, t: (b * nt + t, 0))
    if has_past:
        prev_spec = pl.BlockSpec((bb * WIN_ROWS, A_KV_WIDTH), lambda b, t: (b, 0))
    else:
        per_b = seq // WIN_ROWS
        step = tb // WIN_ROWS
        prev_spec = pl.BlockSpec((WIN_ROWS, A_KV_WIDTH),
                                 lambda b, t: (b * per_b + jnp.maximum(t * step - 1, 0), 0))
    state_spec = pl.BlockSpec((bb, B_HEADS, B_DK, B_DV), lambda b, t: (b, 0, 0, 0))
    glast_spec = pl.BlockSpec((tile // CHUNK, 1, B_QK_WIDTH), lambda b, t: (b * nt + t, 0, 0))
    in_specs = [row_spec(A_WIDTH), row_spec(A_KV_WIDTH), row_spec(A_KV_WIDTH), prev_spec, prev_spec,
                row_spec(B_QK_WIDTH), row_spec(B_QK_WIDTH), row_spec(B_QK_WIDTH), row_spec(B_WIDTH),
                glast_spec, row_spec(MIX_EVEN), row_spec(D_MODEL),
                _const_spec(bias.shape), _const_spec(sink_rows.shape), _const_spec(gla_g.shape),
                _const_spec(w_out.shape)]
    args = [q, k, v, kprev, vprev, qg, kg, kd, bv, glast, sgate, x2, bias, sink_rows, gla_g, w_out]
    if has_past:
        in_specs.append(state_spec)
        args.append(s0)
    return pl.pallas_call(
        functools.partial(_even_mix_kernel, bb=bb, tb=tb, nt=nt, has_past=has_past),
        grid=(nb, nt),
        in_specs=in_specs,
        out_specs=[row_spec(D_MODEL), state_spec],
        out_shape=[jax.ShapeDtypeStruct((batch * seq, D_MODEL), F32),
                   jax.ShapeDtypeStruct((batch, B_HEADS, B_DK, B_DV), F32)],
        scratch_shapes=[pltpu.VMEM((bb, A_KV_HEADS, WIN_ROWS + tb, A_HEAD_DIM), BF16),
                        pltpu.VMEM((bb, A_KV_HEADS, WIN_ROWS + tb, 2 * A_HEAD_DIM), BF16),
                        pltpu.VMEM((bb, B_HEADS, B_DV, B_DK), F32),
                        pltpu.VMEM((tile, MIX_EVEN), BF16)],
        compiler_params=pltpu.CompilerParams(dimension_semantics=("arbitrary", "arbitrary"),
                                             vmem_limit_bytes=VMEM_LIMIT),
        name="even_mix",
    )(*args)


_PAIR = 2 * C_BLOCK
ODD_BATCH = SUBLANES
ODD_STEPS = ROW_TILE // ODD_BATCH
_HIST_ROWS = (CONV_W - 1) * ODD_BATCH


def _odd_kernel(x_ref, hist_ref, h0_ref, g_ref, win_ref, cw_ref, cb_ref, wra_ref, bra_ref,
                wri_ref, bri_ref, lam_ref, wout_ref, gfin_ref,
                out_ref, conv_ref, lru_ref,
                xs_ref, gate_ref, a_ref, b_ref, y_ref, hcar_ref, *, nt):
    t = pl.program_id(1)
    half = C_WIDTH // 2

    @pl.when(t == 0)
    def _init_state():
        hist = hist_ref[...].reshape(_HIST_ROWS, C_WIDTH)
        for k in range(2):
            xs_ref[k, 0:_HIST_ROWS, :] = hist[:, k * half:(k + 1) * half]
        hcar_ref[...] = h0_ref[...]

    steps_h = ODD_STEPS // 2
    rows_h = ROW_TILE // 2
    xh, hh = [None, None], [None, None]

    def load_half(hf):
        x3 = x_ref[:, hf * steps_h:(hf + 1) * steps_h, :]
        xh[hf] = jnp.transpose(x3, (1, 0, 2)).reshape(rows_h, D_MODEL)
        hh[hf] = (_rms_scale(xh[hf]) * g_ref[...]).astype(BF16)

    def project_x(hf, k):
        r0 = _HIST_ROWS + hf * rows_h
        xs_ref[k, r0:r0 + rows_h, :] = _dot(hh[hf], win_ref[:, k * half:(k + 1) * half])

    def project_gate(hf):
        gate_ref[hf * rows_h:(hf + 1) * rows_h, :] = _dot(hh[hf], win_ref[:, C_WIDTH:2 * C_WIDTH])

    soft_lam = lam_ref[...]
    soft_lam = jnp.maximum(-soft_lam, 0.0) + jnp.log1p(jnp.exp(-jnp.abs(soft_lam)))

    def gate_pair(hf, m):
        c0 = m * _PAIR
        k, ck = divmod(c0, half)
        u = cb_ref[:, c0:c0 + _PAIR]
        for j in range(CONV_W):
            r = j * ODD_BATCH + hf * rows_h
            u = u + xs_ref[k, r:r + rows_h, ck:ck + _PAIR] * cw_ref[j:j + 1, c0:c0 + _PAIR]
        ub = u.astype(BF16)
        lo, hi = ub[:, 0:MXU_DIM], ub[:, LANES:LANES + MXU_DIM]

        def gate_pre(w_ref, bias_ref):
            e = _dot(lo, w_ref[2 * m])
            o = _dot(hi, w_ref[2 * m + 1])
            full = jnp.concatenate([e[:, 0:LANES], e[:, LANES:] + o[:, 0:LANES], o[:, LANES:]], axis=1)
            return full + bias_ref[:, c0:c0 + _PAIR]

        rg = _sigmoid(gate_pre(wra_ref, bra_ref))
        ig = _sigmoid(gate_pre(wri_ref, bri_ref))
        log_a = -LRU_C * rg * soft_lam[:, c0:c0 + _PAIR]
        a = jnp.exp(log_a)
        rows = slice(hf * rows_h, (hf + 1) * rows_h)
        a_ref[rows, c0:c0 + _PAIR] = a
        b_ref[rows, c0:c0 + _PAIR] = jnp.sqrt(jnp.tanh(-log_a) * (a * a + 1.0)) * ig * u

    def scan_half(hf, hprev):
        pair = 2 * ODD_BATCH
        for i in range(rows_h // pair):
            r0 = hf * rows_h + i * pair
            lo8, hi8, both = slice(r0, r0 + ODD_BATCH), slice(r0 + ODD_BATCH, r0 + pair), slice(r0, r0 + pair)
            h1 = a_ref[lo8, :] * hprev + b_ref[lo8, :]
            hprev = a_ref[hi8, :] * h1 + b_ref[hi8, :]
            y_ref[both, :] = (jnp.concatenate([h1, hprev], axis=0) * _silu(gate_ref[both, :])).astype(BF16)
        return hprev

    def project_out(hf):
        return _dot(y_ref[hf * rows_h:(hf + 1) * rows_h, :], wout_ref[...])

    def finish_half(hf, proj):
        out = _rms_scale(xh[hf] + proj) * gfin_ref[...]
        out_ref[:, hf * steps_h:(hf + 1) * steps_h, :] = jnp.transpose(
            out.reshape(steps_h, ODD_BATCH, D_MODEL), (1, 0, 2))

    load_half(0)
    project_x(0, 0)
    load_half(1)
    project_x(0, 1)
    gate_pair(0, 0)
    project_x(1, 0)
    gate_pair(0, 1)
    project_x(1, 1)
    gate_pair(0, 2)
    project_gate(0)
    gate_pair(0, 3)
    gate_pair(1, 0)
    project_gate(1)
    gate_pair(1, 1)
    h_mid = scan_half(0, hcar_ref[...])
    proj0 = project_out(0)
    gate_pair(1, 2)
    gate_pair(1, 3)
    hcar_ref[...] = scan_half(1, h_mid)
    proj1 = project_out(1)
    finish_half(0, proj0)
    finish_half(1, proj1)
    for k in range(2):
        xs_ref[k, 0:_HIST_ROWS, :] = xs_ref[k, ROW_TILE:ROW_TILE + _HIST_ROWS, :]

    @pl.when(t == nt - 1)
    def _emit_state():
        for k in range(2):
            conv_ref[:, :, k * half:(k + 1) * half] = (
                xs_ref[k, 0:_HIST_ROWS, :].reshape(CONV_W - 1, ODD_BATCH, half))
        lru_ref[...] = hcar_ref[...]


def _odd_layer(x3, hist_t, h0, norm_g, w_in, conv_w, conv_b, w_ra, b_ra, w_ri, b_ri, lam, w_out, g_fin):
    batch, seq, _ = x3.shape
    nt = seq // ODD_STEPS
    nb = batch // ODD_BATCH
    n_hist = CONV_W - 1
    x_spec = pl.BlockSpec((ODD_BATCH, ODD_STEPS, D_MODEL), lambda b, t: (b, t, 0))
    hist_spec = pl.BlockSpec((n_hist, ODD_BATCH, C_WIDTH), lambda b, t: (0, b, 0))
    h_spec = pl.BlockSpec((ODD_BATCH, C_WIDTH), lambda b, t: (b, 0))
    consts = (norm_g, w_in, conv_w, conv_b, w_ra, b_ra, w_ri, b_ri, lam, w_out, g_fin)
    return pl.pallas_call(
        functools.partial(_odd_kernel, nt=nt),
        grid=(nb, nt),
        in_specs=[x_spec, hist_spec, h_spec] + [_const_spec(c.shape) for c in consts],
        out_specs=[x_spec, hist_spec, h_spec],
        out_shape=[jax.ShapeDtypeStruct((batch, seq, D_MODEL), F32),
                   jax.ShapeDtypeStruct((n_hist, batch, C_WIDTH), F32),
                   jax.ShapeDtypeStruct((batch, C_WIDTH), F32)],
        scratch_shapes=[pltpu.VMEM((2, _HIST_ROWS + ROW_TILE, C_WIDTH // 2), F32),
                        pltpu.VMEM((ROW_TILE, C_WIDTH), F32),
                        pltpu.VMEM((ROW_TILE, C_WIDTH), F32),
                        pltpu.VMEM((ROW_TILE, C_WIDTH), F32),
                        pltpu.VMEM((ROW_TILE, C_WIDTH), BF16),
                        pltpu.VMEM((ODD_BATCH, C_WIDTH), F32)],
        compiler_params=pltpu.CompilerParams(dimension_semantics=("arbitrary", "arbitrary"),
                                             vmem_limit_bytes=VMEM_LIMIT),
        name="odd_layer",
    )(x3, hist_t, h0, *consts)


def _bias_table(masked):
    slopes = 2.0 ** (-8.0 * jnp.arange(1, A_HEADS + 1, dtype=F32) / A_HEADS)
    dist = jnp.abs(WIN_ROWS + jnp.arange(CHUNK)[None, :] - jnp.arange(WIN_KEYS)[:, None]).astype(F32)
    alibi = slopes.reshape(A_KV_HEADS, 1, A_GROUP, 1) * dist[None, :, None, :]
    alibi = alibi.reshape(1, A_KV_HEADS, WIN_KEYS, A_GROUP * CHUNK)
    if not masked:
        return alibi
    first_chunk = jnp.arange(WIN_ROWS // CHUNK + 1)[:, None]
    key_pos = (first_chunk - WIN_ROWS // CHUNK) * CHUNK + jnp.arange(WIN_KEYS)[None, :]
    mask = jnp.where(key_pos < 0, F32(1e30), F32(0.0))
    return alibi + mask[:, None, :, None]


def _chunk_tri():
    r = np.arange(ROW_TILE)
    same_chunk = (r[:, None] // CHUNK) == (r[None, :] // CHUNK)
    return jnp.asarray(same_chunk & (r[:, None] >= r[None, :]), BF16)


def _pad_gate_blocks(w):
    pad = MXU_DIM - C_BLOCK
    even = jnp.pad(w[0::2], ((0, 0), (0, pad), (0, pad)))
    odd = jnp.pad(w[1::2], ((0, 0), (pad, 0), (pad, 0)))
    return jnp.stack([even, odd], axis=1).reshape(C_BLOCKS, MXU_DIM, MXU_DIM).astype(BF16)


def kernel(x_prompt, x_sample, cache_swa_k, cache_swa_v, state_gla, cache_conv, state_lru, norm_even, w_in_even, w_gate_lr, b_gate_lr, sinks, gla_norm, w_out_even, norm_odd, w_in_odd, conv_w, conv_b, w_rg_a, b_rg_a, w_rg_i, b_rg_i, lru_lambda, w_out_odd, norm_final):
    batch, seq, _ = x_prompt.shape
    dbatch, dseq, _ = x_sample.shape
    row = lambda v: v.reshape(1, -1)

    lr0 = _OFF_GATE
    w_e = w_in_even[0]
    w_main = jnp.concatenate([w_e[:, :lr0], w_e[:, lr0 + B_LOWRANK:]], axis=1).astype(BF16)
    w_lr1 = jnp.pad(w_e[:, lr0:lr0 + B_LOWRANK], ((0, 0), (0, LANES - B_LOWRANK))).astype(BF16)
    w_lr2 = jnp.pad(w_gate_lr[0], ((0, LANES - B_LOWRANK), (0, 0))).astype(BF16)
    tri = _chunk_tri()
    w_out_e = w_out_even[0].astype(BF16)
    even_consts = (row(norm_even[0]), w_main, w_lr1, w_lr2, row(b_gate_lr[0]), tri)
    sink_rows = jnp.repeat(sinks[0].reshape(A_KV_HEADS, 1, A_GROUP), CHUNK, axis=2)

    def even_layer(x, past):
        b, tl, _ = x.shape
        x2 = x.reshape(b * tl, D_MODEL)
        q, k, v, qg, kg, kd, bv, glast, sgate = _even_in(x2, *even_consts)
        if past is None:
            kprev, vprev, s0 = k, v, None
        else:
            kprev = past[0].reshape(b * WIN_ROWS, A_KV_WIDTH)
            vprev = past[1].reshape(b * WIN_ROWS, A_KV_WIDTH)
            s0 = past[2]
        out, s_new = _even_mix(q, k, v, kprev, vprev, qg, kg, kd, bv,
                               glast.reshape(-1, 1, B_QK_WIDTH), sgate, x2,
                               _bias_table(masked=past is None), sink_rows,
                               row(gla_norm[0]), w_out_e, s0, batch=b, seq=tl)
        k4 = k.reshape(b, tl, A_KV_HEADS, A_HEAD_DIM)
        v4 = v.reshape(b, tl, A_KV_HEADS, A_HEAD_DIM)
        return out, k4, v4, s_new

    xp, pk, pv, pg = even_layer(x_prompt, None)
    xs, sk, sv, sg = even_layer(x_sample, (cache_swa_k[0], cache_swa_v[0], state_gla[0]))

    odd_consts = (row(norm_odd[0]), w_in_odd[0].astype(BF16), conv_w[0], row(conv_b[0]),
                  _pad_gate_blocks(w_rg_a[0]), row(b_rg_a[0]), _pad_gate_blocks(w_rg_i[0]), row(b_rg_i[0]),
                  row(lru_lambda[0]), w_out_odd[0].astype(BF16), row(norm_final))
    zero_hist = jnp.zeros((CONV_W - 1, batch, C_WIDTH), F32)
    zero_h = jnp.zeros((batch, C_WIDTH), F32)
    yp, pc, plru = _odd_layer(xp.reshape(batch, seq, D_MODEL), zero_hist, zero_h, *odd_consts)
    ys, sc, slru = _odd_layer(xs.reshape(dbatch, dseq, D_MODEL), jnp.swapaxes(cache_conv[0], 0, 1),
                              state_lru[0], *odd_consts)

    return (yp, ys, pk[None, :, -WIN_ROWS:], pv[None, :, -WIN_ROWS:], pg[None],
            jnp.swapaxes(pc, 0, 1)[None], plru[None],
            sk[None], sv[None], sg[None], jnp.swapaxes(sc, 0, 1)[None], slru[None])
```

```python
import functools

import jax
import jax.numpy as jnp
import numpy as np
from jax import lax
from jax.experimental import pallas as pl
from jax.experimental.pallas import tpu as pltpu

F32 = jnp.float32
BF16 = jnp.bfloat16

D_MODEL = 1024
CHUNK = 64
EPS = 1e-6
A_HEADS = 16
A_KV_HEADS = 4
A_HEAD_DIM = 64
A_GROUP = A_HEADS // A_KV_HEADS
A_WIDTH = A_HEADS * A_HEAD_DIM
A_KV_WIDTH = A_KV_HEADS * A_HEAD_DIM
WIN_ROWS = 128
WIN_KEYS = WIN_ROWS + CHUNK
B_HEADS = 4
B_DK = 128
B_DV = 256
B_QK_WIDTH = B_HEADS * B_DK
B_WIDTH = B_HEADS * B_DV
B_LOWRANK = 16
B_GATE_NORM = 16.0
C_WIDTH = 1536
C_BLOCKS = 8
C_BLOCK = C_WIDTH // C_BLOCKS
CONV_W = 4
LRU_C = 8.0
MIX_EVEN = A_WIDTH + B_WIDTH

LANES = 128
SUBLANES = 8
MXU_DIM = 256
ROW_TILE = 512
VMEM_LIMIT = 48 * 1024 * 1024

_OFF_Q = 0
_OFF_K = _OFF_Q + A_WIDTH
_OFF_V = _OFF_K + A_KV_WIDTH
_OFF_BQ = _OFF_V + A_KV_WIDTH
_OFF_BK = _OFF_BQ + B_QK_WIDTH
_OFF_BV = _OFF_BK + B_QK_WIDTH
_OFF_GATE = _OFF_BV + B_WIDTH
_MAIN_WIDTH = _OFF_GATE + MIX_EVEN


def _const_spec(shape):
    nd = len(shape)
    return pl.BlockSpec(shape, lambda *_: (0,) * nd, pipeline_mode=pl.Buffered(1))


def _rms_scale(x):
    return x * lax.rsqrt(jnp.mean(x * x, axis=-1, keepdims=True) + EPS)


def _silu(x):
    half = 0.5 * x
    return half * jnp.tanh(half) + half


def _dot(a, b):
    return jnp.dot(a, b, preferred_element_type=F32)


def _dot_nt(a, b):
    return lax.dot_general(a, b, (((1,), (1,)), ((), ())), preferred_element_type=F32)


def _dot_tn(a, b):
    return lax.dot_general(a, b, (((0,), (0,)), ((), ())), preferred_element_type=F32)


def _even_in_kernel(x_ref, g_ref, w_ref, wlr1_ref, wlr2_ref, blr_ref, tri_ref,
                    q_ref, k_ref, v_ref, qg_ref, kg_ref, kd_ref, bv_ref, glast_ref, sgate_ref):
    h = (_rms_scale(x_ref[...]) * g_ref[...]).astype(BF16)

    def proj(off, width):
        return _dot(h, w_ref[:, off:off + width])

    q_ref[...] = (proj(_OFF_Q, A_WIDTH) * (A_HEAD_DIM ** -0.5)).astype(BF16)
    k_ref[...] = proj(_OFF_K, A_KV_WIDTH)
    v_ref[...] = proj(_OFF_V, A_KV_WIDTH)
    bv_ref[...] = proj(_OFF_BV, B_WIDTH).astype(BF16)
    sgate_ref[...] = _silu(proj(_OFF_GATE, MIX_EVEN)).astype(BF16)

    low = _dot(h, wlr1_ref[...])
    pre = _dot(low.astype(BF16), wlr2_ref[...]) + blr_ref[...]
    glog = (jnp.minimum(pre, 0.0) - jnp.log1p(jnp.exp(-jnp.abs(pre)))) * (1.0 / B_GATE_NORM)
    tri = tri_ref[...]
    g1 = glog.astype(BF16)
    r1 = glog - g1.astype(F32)
    g2 = r1.astype(BF16)
    g3 = (r1 - g2.astype(F32)).astype(BF16)
    gcum = _dot(tri, g1) + _dot(tri, g2) + _dot(tri, g3)

    bk = proj(_OFF_BK, B_QK_WIDTH)
    qg_ref[...] = (proj(_OFF_BQ, B_QK_WIDTH) * (B_DK ** -0.5) * jnp.exp(gcum)).astype(BF16)
    kg_ref[...] = (bk * jnp.exp(-gcum)).astype(BF16)
    for c in range(ROW_TILE // CHUNK):
        rows = slice(c * CHUNK, (c + 1) * CHUNK)
        glast = gcum[(c + 1) * CHUNK - 1:(c + 1) * CHUNK, :]
        glast_ref[c:c + 1, :] = glast
        kd_ref[rows, :] = (bk[rows] * jnp.exp(glast - gcum[rows])).astype(BF16)


def _even_in(x2, norm_g, w_main, w_lr1, w_lr2, b_lr, tri):
    rows = x2.shape[0]
    nsteps = rows // ROW_TILE
    row_spec = lambda w: pl.BlockSpec((ROW_TILE, w), lambda i: (i, 0))
    chunks = ROW_TILE // CHUNK
    out_widths = (A_WIDTH, A_KV_WIDTH, A_KV_WIDTH, B_QK_WIDTH, B_QK_WIDTH, B_QK_WIDTH, B_WIDTH,
                  B_QK_WIDTH, MIX_EVEN)
    out_dtypes = (BF16, F32, F32, BF16, BF16, BF16, BF16, F32, BF16)
    out_rows = [ROW_TILE] * 7 + [chunks, ROW_TILE]
    return pl.pallas_call(
        _even_in_kernel,
        grid=(nsteps,),
        in_specs=[row_spec(D_MODEL), _const_spec(norm_g.shape), _const_spec(w_main.shape),
                  _const_spec(w_lr1.shape), _const_spec(w_lr2.shape), _const_spec(b_lr.shape),
                  _const_spec(tri.shape)],
        out_specs=[pl.BlockSpec((r, w), lambda i: (i, 0)) for r, w in zip(out_rows, out_widths)],
        out_shape=[jax.ShapeDtypeStruct((nsteps * r, w), d)
                   for r, w, d in zip(out_rows, out_widths, out_dtypes)],
        compiler_params=pltpu.CompilerParams(dimension_semantics=("arbitrary",),
                                             vmem_limit_bytes=VMEM_LIMIT),
        name="even_in",
    )(x2, norm_g, w_main, w_lr1, w_lr2, b_lr, tri)


def _even_mix_kernel(*refs, bb, tb, nt, has_past):
    if has_past:
        (q_ref, kc_ref, vc_ref, kp_ref, vp_ref, qg_ref, kg_ref, kd_ref, bv_ref, glast_ref,
         sgate_ref, x_ref, bias_ref, sink_ref, glag_ref, wout_ref, s0_ref,
         out_ref, sfin_ref, kbuf, vbuf, st_ref, y_ref) = refs
    else:
        (q_ref, kc_ref, vc_ref, kp_ref, vp_ref, qg_ref, kg_ref, kd_ref, bv_ref, glast_ref,
         sgate_ref, x_ref, bias_ref, sink_ref, glag_ref, wout_ref,
         out_ref, sfin_ref, kbuf, vbuf, st_ref, y_ref) = refs
        s0_ref = None
    t = pl.program_id(1)
    tc = tb // CHUNK

    one_lane = jnp.where(lax.broadcasted_iota(jnp.int32, (WIN_ROWS + tb, A_HEAD_DIM), 1) == 0,
                         1.0, 0.0).astype(BF16)
    for bi in range(bb):
        for kh in range(A_KV_HEADS):
            cols = slice(kh * A_HEAD_DIM, (kh + 1) * A_HEAD_DIM)
            kbuf[bi, kh, 0:WIN_ROWS, :] = kp_ref[bi * WIN_ROWS:(bi + 1) * WIN_ROWS, cols].astype(BF16)
            kbuf[bi, kh, WIN_ROWS:WIN_ROWS + tb, :] = kc_ref[bi * tb:(bi + 1) * tb, cols].astype(BF16)
            vbuf[bi, kh, 0:WIN_ROWS, 0:A_HEAD_DIM] = (
                vp_ref[bi * WIN_ROWS:(bi + 1) * WIN_ROWS, cols].astype(BF16))
            vbuf[bi, kh, WIN_ROWS:WIN_ROWS + tb, 0:A_HEAD_DIM] = (
                vc_ref[bi * tb:(bi + 1) * tb, cols].astype(BF16))
            vbuf[bi, kh, :, A_HEAD_DIM:2 * A_HEAD_DIM] = one_lane

    @pl.when(t == 0)
    def _init_state():
        for bi in range(bb):
            for hh in range(B_HEADS):
                if has_past:
                    st_ref[bi, hh] = s0_ref[bi, hh].T
                else:
                    st_ref[bi, hh] = jnp.zeros((B_DV, B_DK), F32)

    tril = (lax.broadcasted_iota(jnp.int32, (CHUNK, CHUNK), 0)
            >= lax.broadcasted_iota(jnp.int32, (CHUNK, CHUNK), 1))
    glag = glag_ref[...]
    tail_rows = 16
    tail_row0 = lax.broadcasted_iota(jnp.int32, (tail_rows, A_GROUP * CHUNK), 0) == 0
    tail_iota = lax.broadcasted_iota(jnp.int32, (tail_rows, 2 * A_HEAD_DIM), 0)
    tail_lane = lax.broadcasted_iota(jnp.int32, (tail_rows, 2 * A_HEAD_DIM), 1)
    v_tail = jnp.where((tail_iota == 0) & (tail_lane == A_HEAD_DIM), 1.0, 0.0).astype(BF16)

    def chunk_body(n):
        bi, c = divmod(n, tc)
        w0 = c * CHUNK
        rows = slice(n * CHUNK, (n + 1) * CHUNK)

        variant = 0 if has_past else jnp.minimum(t * tc + c, WIN_ROWS // CHUNK)
        def attn_scores(kh):
            q4 = q_ref[rows, kh * A_GROUP * A_HEAD_DIM:(kh + 1) * A_GROUP * A_HEAD_DIM]
            qs = jnp.concatenate([q4[:, g * A_HEAD_DIM:(g + 1) * A_HEAD_DIM] for g in range(A_GROUP)],
                                 axis=0)
            kw = kbuf[bi, kh, w0:w0 + WIN_KEYS, :]
            return _dot_nt(kw, qs) - bias_ref[variant, kh]

        def attn_finish(kh, sg):
            cols = slice(kh * A_GROUP * A_HEAD_DIM, (kh + 1) * A_GROUP * A_HEAD_DIM)
            vw = vbuf[bi, kh, w0:w0 + WIN_KEYS, :]
            sink = sink_ref[kh]
            m = jnp.maximum(jnp.max(sg, axis=0, keepdims=True), sink)
            p = jnp.exp(sg - m).astype(BF16)
            p_sink = jnp.where(tail_row0, jnp.exp(sink - m), 0.0).astype(BF16)
            oe = _dot_tn(jnp.concatenate([p, p_sink], axis=0),
                         jnp.concatenate([vw, v_tail], axis=0))
            ao = jnp.concatenate(
                [oe[g * CHUNK:(g + 1) * CHUNK, 0:A_HEAD_DIM]
                 * (1.0 / oe[g * CHUNK:(g + 1) * CHUNK, A_HEAD_DIM:A_HEAD_DIM + 1])
                 for g in range(A_GROUP)], axis=1)
            y_ref[rows, cols] = (ao * sgate_ref[rows, cols].astype(F32)).astype(BF16)

        dec = jnp.exp(glast_ref[n])

        def gla_head(hh):
            ks = slice(hh * B_DK, (hh + 1) * B_DK)
            vs = slice(hh * B_DV, (hh + 1) * B_DV)
            qg = qg_ref[rows, ks]
            a = jnp.where(tril, _dot_nt(qg, kg_ref[rows, ks]), 0.0).astype(BF16)
            vh = bv_ref[rows, vs]
            st = st_ref[bi, hh]
            o = _dot(a, vh) + _dot_nt(qg, st.astype(BF16))
            st_ref[bi, hh] = st * dec[:, ks] + _dot_tn(vh, kd_ref[rows, ks])
            bo = _rms_scale(o) * glag
            cols = slice(A_WIDTH + hh * B_DV, A_WIDTH + (hh + 1) * B_DV)
            y_ref[rows, cols] = (bo * sgate_ref[rows, cols].astype(F32)).astype(BF16)

        scores = attn_scores(0)
        for kh in range(A_KV_HEADS):
            nxt = attn_scores(kh + 1) if kh + 1 < A_KV_HEADS else None
            gla_head(kh)
            attn_finish(kh, scores)
            scores = nxt

    n_chunks = bb * tc
    rows_h = n_chunks // 2 * CHUNK
    for n in range(n_chunks // 2):
        chunk_body(n)
    proj0 = _dot(y_ref[0:rows_h, :], wout_ref[...])
    for n in range(n_chunks // 2, n_chunks):
        chunk_body(n)
    out_ref[0:rows_h, :] = x_ref[0:rows_h, :] + proj0
    out_ref[rows_h:, :] = x_ref[rows_h:, :] + _dot(y_ref[rows_h:, :], wout_ref[...])

    @pl.when(t == nt - 1)
    def _emit_state():
        for bi in range(bb):
            for hh in range(B_HEADS):
                sfin_ref[bi, hh] = st_ref[bi, hh].T


def _even_mix(q, k, v, kprev, vprev, qg, kg, kd, bv, glast, sgate, x2, bias, sink_rows, gla_g, w_out, s0,
              *, batch, seq):
    has_past = s0 is not None
    tile = ROW_TILE // 2 if has_past else ROW_TILE
    tb = min(seq, tile)
    bb = tile // tb
    nt = seq // tb
    nb = batch // bb
    assert bb == 1 or nt == 1
    row_spec = lambda w: pl.BlockSpec((tile, w), lambda b, t: (b * nt + t, 0))
    if has_past:
        prev_spec = pl.BlockSpec((bb * WIN_ROWS, A_KV_WIDTH), lambda b, t: (b, 0))
    else:
        per_b = seq // WIN_ROWS
        step = tb // WIN_ROWS
        prev_spec = pl.BlockSpec((WIN_ROWS, A_KV_WIDTH),
                                 lambda b, t: (b * per_b + jnp.maximum(t * step - 1, 0), 0))
    state_spec = pl.BlockSpec((bb, B_HEADS, B_DK, B_DV), lambda b, t: (b, 0, 0, 0))
    glast_spec = pl.BlockSpec((tile // CHUNK, 1, B_QK_WIDTH), lambda b, t: (b * nt + t, 0, 0))
    in_specs = [row_spec(A_WIDTH), row_spec(A_KV_WIDTH), row_spec(A_KV_WIDTH), prev_spec, prev_spec,
                row_spec(B_QK_WIDTH), row_spec(B_QK_WIDTH), row_spec(B_QK_WIDTH), row_spec(B_WIDTH),
                glast_spec, row_spec(MIX_EVEN), row_spec(D_MODEL),
                _const_spec(bias.shape), _const_spec(sink_rows.shape), _const_spec(gla_g.shape),
                _const_spec(w_out.shape)]
    args = [q, k, v, kprev, vprev, qg, kg, kd, bv, glast, sgate, x2, bias, sink_rows, gla_g, w_out]
    if has_past:
        in_specs.append(state_spec)
        args.append(s0)
    return pl.pallas_call(
        functools.partial(_even_mix_kernel, bb=bb, tb=tb, nt=nt, has_past=has_past),
        grid=(nb, nt),
        in_specs=in_specs,
        out_specs=[row_spec(D_MODEL), state_spec],
        out_shape=[jax.ShapeDtypeStruct((batch * seq, D_MODEL), F32),
                   jax.ShapeDtypeStruct((batch, B_HEADS, B_DK, B_DV), F32)],
        scratch_shapes=[pltpu.VMEM((bb, A_KV_HEADS, WIN_ROWS + tb, A_HEAD_DIM), BF16),
                        pltpu.VMEM((bb, A_KV_HEADS, WIN_ROWS + tb, 2 * A_HEAD_DIM), BF16),
                        pltpu.VMEM((bb, B_HEADS, B_DV, B_DK), F32),
                        pltpu.VMEM((tile, MIX_EVEN), BF16)],
        compiler_params=pltpu.CompilerParams(dimension_semantics=("arbitrary", "arbitrary"),
                                             vmem_limit_bytes=VMEM_LIMIT),
        name="even_mix",
    )(*args)


_PAIR = 2 * C_BLOCK
ODD_BATCH = SUBLANES
ODD_STEPS = ROW_TILE // ODD_BATCH
_HIST_ROWS = (CONV_W - 1) * ODD_BATCH


def _odd_kernel(x_ref, hist_ref, h0_ref, g_ref, win_ref, cw_ref, cb_ref, wra_ref, bra_ref,
                wri_ref, bri_ref, lam_ref, wout_ref, gfin_ref,
                out_ref, conv_ref, lru_ref,
                xs_ref, gate_ref, a_ref, b_ref, y_ref, hcar_ref, *, nt):
    t = pl.program_id(1)
    half = C_WIDTH // 2

    @pl.when(t == 0)
    def _init_state():
        hist = hist_ref[...].reshape(_HIST_ROWS, C_WIDTH)
        for k in range(2):
            xs_ref[k, 0:_HIST_ROWS, :] = hist[:, k * half:(k + 1) * half]
        hcar_ref[...] = h0_ref[...]

    steps_h = ODD_STEPS // 2
    rows_h = ROW_TILE // 2
    xh, hh = [None, None], [None, None]

    def load_half(hf):
        x3 = x_ref[:, hf * steps_h:(hf + 1) * steps_h, :]
        xh[hf] = jnp.transpose(x3, (1, 0, 2)).reshape(rows_h, D_MODEL)
        hh[hf] = (_rms_scale(xh[hf]) * g_ref[...]).astype(BF16)

    def project_x(hf, k):
        r0 = _HIST_ROWS + hf * rows_h
        xs_ref[k, r0:r0 + rows_h, :] = _dot(hh[hf], win_ref[:, k * half:(k + 1) * half])

    def project_gate(hf):
        gate_ref[hf * rows_h:(hf + 1) * rows_h, :] = _dot(hh[hf], win_ref[:, C_WIDTH:2 * C_WIDTH])

    soft_lam = lam_ref[...]
    soft_lam = jnp.maximum(-soft_lam, 0.0) + jnp.log1p(jnp.exp(-jnp.abs(soft_lam)))

    def gate_pair(hf, m):
        c0 = m * _PAIR
        k, ck = divmod(c0, half)
        u = cb_ref[:, c0:c0 + _PAIR]
        for j in range(CONV_W):
            r = j * ODD_BATCH + hf * rows_h
            u = u + xs_ref[k, r:r + rows_h, ck:ck + _PAIR] * cw_ref[j:j + 1, c0:c0 + _PAIR]
        ub = u.astype(BF16)
        lo, hi = ub[:, 0:MXU_DIM], ub[:, LANES:LANES + MXU_DIM]

        def gate_tanh(w_ref, bias_ref):
            e = _dot(lo, w_ref[2 * m])
            o = _dot(hi, w_ref[2 * m + 1])
            full = jnp.concatenate([e[:, 0:LANES], e[:, LANES:] + o[:, 0:LANES], o[:, LANES:]], axis=1)
            return jnp.tanh(full + bias_ref[:, c0:c0 + _PAIR])

        t_r = gate_tanh(wra_ref, bra_ref)
        t_i = gate_tanh(wri_ref, bri_ref)
        c4 = (0.5 * LRU_C) * soft_lam[:, c0:c0 + _PAIR]
        neg_log_a = c4 * t_r + c4
        a = jnp.exp(-neg_log_a)
        rows = slice(hf * rows_h, (hf + 1) * rows_h)
        a_ref[rows, c0:c0 + _PAIR] = a
        x1 = jnp.tanh(neg_log_a) * (a * a + 1.0)
        root = jnp.where(x1 > 0.0, x1 * lax.rsqrt(x1), 0.0)
        hu = 0.5 * u
        b_ref[rows, c0:c0 + _PAIR] = root * (hu * t_i + hu)

    def scan_half(hf, hprev):
        pair = 2 * ODD_BATCH
        for i in range(rows_h // pair):
            r0 = hf * rows_h + i * pair
            lo8, hi8, both = slice(r0, r0 + ODD_BATCH), slice(r0 + ODD_BATCH, r0 + pair), slice(r0, r0 + pair)
            h1 = a_ref[lo8, :] * hprev + b_ref[lo8, :]
            hprev = a_ref[hi8, :] * h1 + b_ref[hi8, :]
            y_ref[both, :] = (jnp.concatenate([h1, hprev], axis=0) * _silu(gate_ref[both, :])).astype(BF16)
        return hprev

    def project_out(hf):
        return _dot(y_ref[hf * rows_h:(hf + 1) * rows_h, :], wout_ref[...])

    def finish_half(hf, proj):
        out = _rms_scale(xh[hf] + proj) * gfin_ref[...]
        out_ref[:, hf * steps_h:(hf + 1) * steps_h, :] = jnp.transpose(
            out.reshape(steps_h, ODD_BATCH, D_MODEL), (1, 0, 2))

    load_half(0)
    project_x(0, 0)
    load_half(1)
    project_x(0, 1)
    gate_pair(0, 0)
    project_x(1, 0)
    gate_pair(0, 1)
    project_x(1, 1)
    gate_pair(0, 2)
    project_gate(0)
    gate_pair(0, 3)
    gate_pair(1, 0)
    project_gate(1)
    gate_pair(1, 1)
    h_mid = scan_half(0, hcar_ref[...])
    proj0 = project_out(0)
    gate_pair(1, 2)
    gate_pair(1, 3)
    hcar_ref[...] = scan_half(1, h_mid)
    proj1 = project_out(1)
    finish_half(0, proj0)
    finish_half(1, proj1)
    for k in range(2):
        xs_ref[k, 0:_HIST_ROWS, :] = xs_ref[k, ROW_TILE:ROW_TILE + _HIST_ROWS, :]

    @pl.when(t == nt - 1)
    def _emit_state():
        for k in range(2):
            conv_ref[:, :, k * half:(k + 1) * half] = (
                xs_ref[k, 0:_HIST_ROWS, :].reshape(CONV_W - 1, ODD_BATCH, half))
        lru_ref[...] = hcar_ref[...]


def _odd_layer(x3, hist_t, h0, norm_g, w_in, conv_w, conv_b, w_ra, b_ra, w_ri, b_ri, lam, w_out, g_fin):
    batch, seq, _ = x3.shape
    nt = seq // ODD_STEPS
    nb = batch // ODD_BATCH
    n_hist = CONV_W - 1
    x_spec = pl.BlockSpec((ODD_BATCH, ODD_STEPS, D_MODEL), lambda b, t: (b, t, 0))
    hist_spec = pl.BlockSpec((n_hist, ODD_BATCH, C_WIDTH), lambda b, t: (0, b, 0))
    h_spec = pl.BlockSpec((ODD_BATCH, C_WIDTH), lambda b, t: (b, 0))
    consts = (norm_g, w_in, conv_w, conv_b, w_ra, b_ra, w_ri, b_ri, lam, w_out, g_fin)
    return pl.pallas_call(
        functools.partial(_odd_kernel, nt=nt),
        grid=(nb, nt),
        in_specs=[x_spec, hist_spec, h_spec] + [_const_spec(c.shape) for c in consts],
        out_specs=[x_spec, hist_spec, h_spec],
        out_shape=[jax.ShapeDtypeStruct((batch, seq, D_MODEL), F32),
                   jax.ShapeDtypeStruct((n_hist, batch, C_WIDTH), F32),
                   jax.ShapeDtypeStruct((batch, C_WIDTH), F32)],
        scratch_shapes=[pltpu.VMEM((2, _HIST_ROWS + ROW_TILE, C_WIDTH // 2), F32),
                        pltpu.VMEM((ROW_TILE, C_WIDTH), F32),
                        pltpu.VMEM((ROW_TILE, C_WIDTH), F32),
                        pltpu.VMEM((ROW_TILE, C_WIDTH), F32),
                        pltpu.VMEM((ROW_TILE, C_WIDTH), BF16),
                        pltpu.VMEM((ODD_BATCH, C_WIDTH), F32)],
        compiler_params=pltpu.CompilerParams(dimension_semantics=("arbitrary", "arbitrary"),
                                             vmem_limit_bytes=VMEM_LIMIT),
        name="odd_layer",
    )(x3, hist_t, h0, *consts)


def _bias_table(masked):
    slopes = 2.0 ** (-8.0 * jnp.arange(1, A_HEADS + 1, dtype=F32) / A_HEADS)
    dist = jnp.abs(WIN_ROWS + jnp.arange(CHUNK)[None, :] - jnp.arange(WIN_KEYS)[:, None]).astype(F32)
    alibi = slopes.reshape(A_KV_HEADS, 1, A_GROUP, 1) * dist[None, :, None, :]
    alibi = alibi.reshape(1, A_KV_HEADS, WIN_KEYS, A_GROUP * CHUNK)
    if not masked:
        return alibi
    first_chunk = jnp.arange(WIN_ROWS // CHUNK + 1)[:, None]
    key_pos = (first_chunk - WIN_ROWS // CHUNK) * CHUNK + jnp.arange(WIN_KEYS)[None, :]
    mask = jnp.where(key_pos < 0, F32(1e30), F32(0.0))
    return alibi + mask[:, None, :, None]


def _chunk_tri():
    r = np.arange(ROW_TILE)
    same_chunk = (r[:, None] // CHUNK) == (r[None, :] // CHUNK)
    return jnp.asarray(same_chunk & (r[:, None] >= r[None, :]), BF16)


def _pad_gate_blocks(w):
    pad = MXU_DIM - C_BLOCK
    even = jnp.pad(w[0::2], ((0, 0), (0, pad), (0, pad)))
    odd = jnp.pad(w[1::2], ((0, 0), (pad, 0), (pad, 0)))
    return jnp.stack([even, odd], axis=1).reshape(C_BLOCKS, MXU_DIM, MXU_DIM).astype(BF16)


def kernel(x_prompt, x_sample, cache_swa_k, cache_swa_v, state_gla, cache_conv, state_lru, norm_even, w_in_even, w_gate_lr, b_gate_lr, sinks, gla_norm, w_out_even, norm_odd, w_in_odd, conv_w, conv_b, w_rg_a, b_rg_a, w_rg_i, b_rg_i, lru_lambda, w_out_odd, norm_final):
    batch, seq, _ = x_prompt.shape
    dbatch, dseq, _ = x_sample.shape
    row = lambda v: v.reshape(1, -1)

    lr0 = _OFF_GATE
    w_e = w_in_even[0]
    w_main = jnp.concatenate([w_e[:, :lr0].astype(BF16), w_e[:, lr0 + B_LOWRANK:].astype(BF16)], axis=1)
    w_lr1 = jnp.pad(w_e[:, lr0:lr0 + B_LOWRANK], ((0, 0), (0, LANES - B_LOWRANK))).astype(BF16)
    w_lr2 = jnp.pad(w_gate_lr[0], ((0, LANES - B_LOWRANK), (0, 0))).astype(BF16)
    tri = _chunk_tri()
    w_out_e = w_out_even[0].astype(BF16)
    even_consts = (row(norm_even[0]), w_main, w_lr1, w_lr2, row(b_gate_lr[0]), tri)
    sink_rows = jnp.repeat(sinks[0].reshape(A_KV_HEADS, 1, A_GROUP), CHUNK, axis=2)

    def even_layer(x, past):
        b, tl, _ = x.shape
        x2 = x.reshape(b * tl, D_MODEL)
        q, k, v, qg, kg, kd, bv, glast, sgate = _even_in(x2, *even_consts)
        if past is None:
            kprev, vprev, s0 = k, v, None
        else:
            kprev = past[0].reshape(b * WIN_ROWS, A_KV_WIDTH)
            vprev = past[1].reshape(b * WIN_ROWS, A_KV_WIDTH)
            s0 = past[2]
        out, s_new = _even_mix(q, k, v, kprev, vprev, qg, kg, kd, bv,
                               glast.reshape(-1, 1, B_QK_WIDTH), sgate, x2,
                               _bias_table(masked=past is None), sink_rows,
                               row(gla_norm[0]), w_out_e, s0, batch=b, seq=tl)
        keep = min(tl, WIN_ROWS)
        k4 = k.reshape(b, tl, A_KV_WIDTH)[:, tl - keep:].reshape(b, keep, A_KV_HEADS, A_HEAD_DIM)
        v4 = v.reshape(b, tl, A_KV_WIDTH)[:, tl - keep:].reshape(b, keep, A_KV_HEADS, A_HEAD_DIM)
        return out, k4, v4, s_new

    xp, pk, pv, pg = even_layer(x_prompt, None)
    xs, sk, sv, sg = even_layer(x_sample, (cache_swa_k[0], cache_swa_v[0], state_gla[0]))

    odd_consts = (row(norm_odd[0]), w_in_odd[0].astype(BF16), conv_w[0], row(conv_b[0]),
                  _pad_gate_blocks(0.5 * w_rg_a[0]), row(0.5 * b_rg_a[0]),
                  _pad_gate_blocks(0.5 * w_rg_i[0]), row(0.5 * b_rg_i[0]),
                  row(lru_lambda[0]), w_out_odd[0].astype(BF16), row(norm_final))
    zero_hist = jnp.zeros((CONV_W - 1, batch, C_WIDTH), F32)
    zero_h = jnp.zeros((batch, C_WIDTH), F32)
    yp, pc, plru = _odd_layer(xp.reshape(batch, seq, D_MODEL), zero_hist, zero_h, *odd_consts)
    ys, sc, slru = _odd_layer(xs.reshape(dbatch, dseq, D_MODEL), jnp.swapaxes(cache_conv[0], 0, 1),
                              state_lru[0], *odd_consts)

    return (yp, ys, pk[None], pv[None], pg[None],
            jnp.swapaxes(pc, 0, 1)[None], plru[None],
            sk[None], sv[None], sg[None], jnp.swapaxes(sc, 0, 1)[None], slru[None])
```

```python
import functools

import jax
import jax.numpy as jnp
import numpy as np
from jax import lax
from jax.experimental import pallas as pl
from jax.experimental.pallas import tpu as pltpu

F32 = jnp.float32
BF16 = jnp.bfloat16

D_MODEL = 1024
CHUNK = 64
EPS = 1e-6
A_HEADS = 16
A_KV_HEADS = 4
A_HEAD_DIM = 64
A_GROUP = A_HEADS // A_KV_HEADS
A_WIDTH = A_HEADS * A_HEAD_DIM
A_KV_WIDTH = A_KV_HEADS * A_HEAD_DIM
WIN_ROWS = 128
WIN_KEYS = WIN_ROWS + CHUNK
B_HEADS = 4
B_DK = 128
B_DV = 256
B_QK_WIDTH = B_HEADS * B_DK
B_WIDTH = B_HEADS * B_DV
B_LOWRANK = 16
B_GATE_NORM = 16.0
C_WIDTH = 1536
C_BLOCKS = 8
C_BLOCK = C_WIDTH // C_BLOCKS
CONV_W = 4
LRU_C = 8.0
MIX_EVEN = A_WIDTH + B_WIDTH

LANES = 128
SUBLANES = 8
MXU_DIM = 256
ROW_TILE = 512
VMEM_LIMIT = 48 * 1024 * 1024

_OFF_Q = 0
_OFF_K = _OFF_Q + A_WIDTH
_OFF_V = _OFF_K + A_KV_WIDTH
_OFF_BQ = _OFF_V + A_KV_WIDTH
_OFF_BK = _OFF_BQ + B_QK_WIDTH
_OFF_BV = _OFF_BK + B_QK_WIDTH
_OFF_GATE = _OFF_BV + B_WIDTH
_MAIN_WIDTH = _OFF_GATE + MIX_EVEN


def _const_spec(shape):
    nd = len(shape)
    return pl.BlockSpec(shape, lambda *_: (0,) * nd, pipeline_mode=pl.Buffered(1))


def _rms_scale(x):
    return x * lax.rsqrt(jnp.mean(x * x, axis=-1, keepdims=True) + EPS)


def _silu(x):
    half = 0.5 * x
    return half * jnp.tanh(half) + half


def _dot(a, b):
    return jnp.dot(a, b, preferred_element_type=F32)


def _dot_nt(a, b):
    return lax.dot_general(a, b, (((1,), (1,)), ((), ())), preferred_element_type=F32)


def _dot_tn(a, b):
    return lax.dot_general(a, b, (((0,), (0,)), ((), ())), preferred_element_type=F32)


def _even_in_kernel(x_ref, g_ref, w_ref, wlr1_ref, wlr2_ref, blr_ref,
                    q_ref, k_ref, v_ref, qg_ref, kg_ref, kd_ref, bv_ref, glast_ref, sgate_ref):
    h = (_rms_scale(x_ref[...]) * g_ref[...]).astype(BF16)

    def proj(off, width):
        return _dot(h, w_ref[:, off:off + width])

    q_ref[...] = (proj(_OFF_Q, A_WIDTH) * (A_HEAD_DIM ** -0.5)).astype(BF16)
    k_ref[...] = proj(_OFF_K, A_KV_WIDTH)
    v_ref[...] = proj(_OFF_V, A_KV_WIDTH)
    bv_ref[...] = proj(_OFF_BV, B_WIDTH).astype(BF16)
    sgate_ref[...] = _silu(proj(_OFF_GATE, MIX_EVEN)).astype(BF16)

    low = _dot(h, wlr1_ref[...])
    pre = _dot(low.astype(BF16), wlr2_ref[...]) + blr_ref[...]
    glog = (jnp.minimum(pre, 0.0) - jnp.log1p(jnp.exp(-jnp.abs(pre)))) * (1.0 / B_GATE_NORM)
    row_id = lax.broadcasted_iota(jnp.int32, (SUBLANES, B_QK_WIDTH), 0)
    groups = []
    for r in range(ROW_TILE // SUBLANES):
        g8 = glog[r * SUBLANES:(r + 1) * SUBLANES, :]
        for sh in (1, 2, 4):
            g8 = g8 + jnp.where(row_id >= sh, pltpu.roll(g8, sh, axis=0), 0.0)
        if r % (CHUNK // SUBLANES) != 0:
            g8 = g8 + groups[-1][SUBLANES - 1:SUBLANES, :]
        groups.append(g8)
    gcum = jnp.concatenate(groups, axis=0)

    bk = proj(_OFF_BK, B_QK_WIDTH)
    qg_ref[...] = (proj(_OFF_BQ, B_QK_WIDTH) * (B_DK ** -0.5) * jnp.exp(gcum)).astype(BF16)
    kg_ref[...] = (bk * jnp.exp(-gcum)).astype(BF16)
    for c in range(ROW_TILE // CHUNK):
        rows = slice(c * CHUNK, (c + 1) * CHUNK)
        glast = gcum[(c + 1) * CHUNK - 1:(c + 1) * CHUNK, :]
        glast_ref[c:c + 1, :] = glast
        kd_ref[rows, :] = (bk[rows] * jnp.exp(glast - gcum[rows])).astype(BF16)


def _even_in(x2, norm_g, w_main, w_lr1, w_lr2, b_lr):
    rows = x2.shape[0]
    nsteps = rows // ROW_TILE
    row_spec = lambda w: pl.BlockSpec((ROW_TILE, w), lambda i: (i, 0))
    chunks = ROW_TILE // CHUNK
    out_widths = (A_WIDTH, A_KV_WIDTH, A_KV_WIDTH, B_QK_WIDTH, B_QK_WIDTH, B_QK_WIDTH, B_WIDTH,
                  B_QK_WIDTH, MIX_EVEN)
    out_dtypes = (BF16, F32, F32, BF16, BF16, BF16, BF16, F32, BF16)
    out_rows = [ROW_TILE] * 7 + [chunks, ROW_TILE]
    return pl.pallas_call(
        _even_in_kernel,
        grid=(nsteps,),
        in_specs=[row_spec(D_MODEL), _const_spec(norm_g.shape), _const_spec(w_main.shape),
                  _const_spec(w_lr1.shape), _const_spec(w_lr2.shape), _const_spec(b_lr.shape)],
        out_specs=[pl.BlockSpec((r, w), lambda i: (i, 0)) for r, w in zip(out_rows, out_widths)],
        out_shape=[jax.ShapeDtypeStruct((nsteps * r, w), d)
                   for r, w, d in zip(out_rows, out_widths, out_dtypes)],
        compiler_params=pltpu.CompilerParams(dimension_semantics=("arbitrary",),
                                             vmem_limit_bytes=VMEM_LIMIT),
        name="even_in",
    )(x2, norm_g, w_main, w_lr1, w_lr2, b_lr)


def _even_mix_kernel(*refs, bb, tb, nt, has_past):
    if has_past:
        (q_ref, kc_ref, vc_ref, kp_ref, vp_ref, qg_ref, kg_ref, kd_ref, bv_ref, glast_ref,
         sgate_ref, x_ref, bias_ref, sink_ref, glag_ref, wout_ref, s0_ref,
         out_ref, sfin_ref, kbuf, vbuf, st_ref, y_ref) = refs
    else:
        (q_ref, kc_ref, vc_ref, kp_ref, vp_ref, qg_ref, kg_ref, kd_ref, bv_ref, glast_ref,
         sgate_ref, x_ref, bias_ref, sink_ref, glag_ref, wout_ref,
         out_ref, sfin_ref, kbuf, vbuf, st_ref, y_ref) = refs
        s0_ref = None
    t = pl.program_id(1)
    tc = tb // CHUNK

    one_lane = jnp.where(lax.broadcasted_iota(jnp.int32, (WIN_ROWS + tb, A_HEAD_DIM), 1) == 0,
                         1.0, 0.0).astype(BF16)
    for bi in range(bb):
        for kh in range(A_KV_HEADS):
            cols = slice(kh * A_HEAD_DIM, (kh + 1) * A_HEAD_DIM)
            if has_past:
                kbuf[bi, kh, 0:WIN_ROWS, :] = kp_ref[bi, kh].T.astype(BF16)
                vbuf[bi, kh, 0:WIN_ROWS, 0:A_HEAD_DIM] = vp_ref[bi, kh].T.astype(BF16)
            else:
                kbuf[bi, kh, 0:WIN_ROWS, :] = kp_ref[:, cols].astype(BF16)
                vbuf[bi, kh, 0:WIN_ROWS, 0:A_HEAD_DIM] = vp_ref[:, cols].astype(BF16)
            kbuf[bi, kh, WIN_ROWS:WIN_ROWS + tb, :] = kc_ref[bi * tb:(bi + 1) * tb, cols].astype(BF16)
            vbuf[bi, kh, WIN_ROWS:WIN_ROWS + tb, 0:A_HEAD_DIM] = (
                vc_ref[bi * tb:(bi + 1) * tb, cols].astype(BF16))
            vbuf[bi, kh, :, A_HEAD_DIM:2 * A_HEAD_DIM] = one_lane

    @pl.when(t == 0)
    def _init_state():
        for bi in range(bb):
            for hh in range(B_HEADS):
                if has_past:
                    st_ref[bi, hh] = s0_ref[bi, hh].T
                else:
                    st_ref[bi, hh] = jnp.zeros((B_DV, B_DK), F32)

    tril = (lax.broadcasted_iota(jnp.int32, (CHUNK, CHUNK), 0)
            >= lax.broadcasted_iota(jnp.int32, (CHUNK, CHUNK), 1))
    glag = glag_ref[...]
    tail_rows = 16
    tail_row0 = lax.broadcasted_iota(jnp.int32, (tail_rows, A_GROUP * CHUNK), 0) == 0
    tail_iota = lax.broadcasted_iota(jnp.int32, (tail_rows, 2 * A_HEAD_DIM), 0)
    tail_lane = lax.broadcasted_iota(jnp.int32, (tail_rows, 2 * A_HEAD_DIM), 1)
    v_tail = jnp.where((tail_iota == 0) & (tail_lane == A_HEAD_DIM), 1.0, 0.0).astype(BF16)

    def chunk_body(n, carry):
        bi = n // tc
        c = n % tc
        w0 = pl.multiple_of(c * CHUNK, CHUNK)
        rows = pl.ds(pl.multiple_of(n * CHUNK, CHUNK), CHUNK)

        variant = 0 if has_past else jnp.minimum(t * tc + c, WIN_ROWS // CHUNK)
        def attn_scores(kh):
            q4 = q_ref[rows, kh * A_GROUP * A_HEAD_DIM:(kh + 1) * A_GROUP * A_HEAD_DIM]
            qs = jnp.concatenate([q4[:, g * A_HEAD_DIM:(g + 1) * A_HEAD_DIM] for g in range(A_GROUP)],
                                 axis=0)
            kw = kbuf[bi, kh, pl.ds(w0, WIN_KEYS), :]
            return _dot_nt(kw, qs) - bias_ref[variant, kh]

        def attn_finish(kh, sg):
            cols = slice(kh * A_GROUP * A_HEAD_DIM, (kh + 1) * A_GROUP * A_HEAD_DIM)
            vw = vbuf[bi, kh, pl.ds(w0, WIN_KEYS), :]
            sink = sink_ref[kh]
            m = jnp.maximum(jnp.max(sg, axis=0, keepdims=True), sink)
            p = jnp.exp(sg - m).astype(BF16)
            p_sink = jnp.where(tail_row0, jnp.exp(sink - m), 0.0).astype(BF16)
            oe = _dot_tn(jnp.concatenate([p, p_sink], axis=0),
                         jnp.concatenate([vw, v_tail], axis=0))
            ao = jnp.concatenate(
                [oe[g * CHUNK:(g + 1) * CHUNK, 0:A_HEAD_DIM]
                 * (1.0 / oe[g * CHUNK:(g + 1) * CHUNK, A_HEAD_DIM:A_HEAD_DIM + 1])
                 for g in range(A_GROUP)], axis=1)
            y_ref[rows, cols] = (ao * sgate_ref[rows, cols].astype(F32)).astype(BF16)

        dec = jnp.exp(glast_ref[n])

        def gla_head(hh):
            ks = slice(hh * B_DK, (hh + 1) * B_DK)
            vs = slice(hh * B_DV, (hh + 1) * B_DV)
            qg = qg_ref[rows, ks]
            a = jnp.where(tril, _dot_nt(qg, kg_ref[rows, ks]), 0.0).astype(BF16)
            vh = bv_ref[rows, vs]
            st = st_ref[bi, hh]
            o = _dot(a, vh) + _dot_nt(qg, st.astype(BF16))
            st_ref[bi, hh] = st * dec[:, ks] + _dot_tn(vh, kd_ref[rows, ks])
            bo = _rms_scale(o) * glag
            cols = slice(A_WIDTH + hh * B_DV, A_WIDTH + (hh + 1) * B_DV)
            y_ref[rows, cols] = (bo * sgate_ref[rows, cols].astype(F32)).astype(BF16)

        scores = attn_scores(0)
        for kh in range(A_KV_HEADS):
            nxt = attn_scores(kh + 1) if kh + 1 < A_KV_HEADS else None
            gla_head(kh)
            attn_finish(kh, scores)
            scores = nxt
        return carry

    lax.fori_loop(0, bb * tc, chunk_body, 0, unroll=4)

    out_ref[...] = x_ref[...] + _dot(y_ref[...], wout_ref[...])

    @pl.when(t == nt - 1)
    def _emit_state():
        for bi in range(bb):
            for hh in range(B_HEADS):
                sfin_ref[bi, hh] = st_ref[bi, hh].T


def _even_mix(q, k, v, kprev, vprev, qg, kg, kd, bv, glast, sgate, x2, bias, sink_rows, gla_g, w_out, s0,
              *, batch, seq):
    has_past = s0 is not None
    tile = ROW_TILE // 2 if has_past else ROW_TILE
    tb = min(seq, tile)
    bb = tile // tb
    nt = seq // tb
    nb = batch // bb
    assert bb == 1 or nt == 1
    row_spec = lambda w: pl.BlockSpec((tile, w), lambda b, t: (b * nt + t, 0))
    if has_past:
        prev_spec = pl.BlockSpec((bb, A_KV_HEADS, A_HEAD_DIM, WIN_ROWS), lambda b, t: (b, 0, 0, 0))
    else:
        assert bb == 1
        per_b = seq // WIN_ROWS
        step = tb // WIN_ROWS
        prev_spec = pl.BlockSpec((WIN_ROWS, A_KV_WIDTH),
                                 lambda b, t: (b * per_b + jnp.maximum(t * step - 1, 0), 0))
    state_spec = pl.BlockSpec((bb, B_HEADS, B_DK, B_DV), lambda b, t: (b, 0, 0, 0))
    glast_spec = pl.BlockSpec((tile // CHUNK, 1, B_QK_WIDTH), lambda b, t: (b * nt + t, 0, 0))
    in_specs = [row_spec(A_WIDTH), row_spec(A_KV_WIDTH), row_spec(A_KV_WIDTH), prev_spec, prev_spec,
                row_spec(B_QK_WIDTH), row_spec(B_QK_WIDTH), row_spec(B_QK_WIDTH), row_spec(B_WIDTH),
                glast_spec, row_spec(MIX_EVEN), row_spec(D_MODEL),
                _const_spec(bias.shape), _const_spec(sink_rows.shape), _const_spec(gla_g.shape),
                _const_spec(w_out.shape)]
    args = [q, k, v, kprev, vprev, qg, kg, kd, bv, glast, sgate, x2, bias, sink_rows, gla_g, w_out]
    if has_past:
        in_specs.append(state_spec)
        args.append(s0)
    return pl.pallas_call(
        functools.partial(_even_mix_kernel, bb=bb, tb=tb, nt=nt, has_past=has_past),
        grid=(nb, nt),
        in_specs=in_specs,
        out_specs=[row_spec(D_MODEL), state_spec],
        out_shape=[jax.ShapeDtypeStruct((batch * seq, D_MODEL), F32),
                   jax.ShapeDtypeStruct((batch, B_HEADS, B_DK, B_DV), F32)],
        scratch_shapes=[pltpu.VMEM((bb, A_KV_HEADS, WIN_ROWS + tb, A_HEAD_DIM), BF16),
                        pltpu.VMEM((bb, A_KV_HEADS, WIN_ROWS + tb, 2 * A_HEAD_DIM), BF16),
                        pltpu.VMEM((bb, B_HEADS, B_DV, B_DK), F32),
                        pltpu.VMEM((tile, MIX_EVEN), BF16)],
        compiler_params=pltpu.CompilerParams(dimension_semantics=("arbitrary", "arbitrary"),
                                             vmem_limit_bytes=VMEM_LIMIT),
        name="even_mix",
    )(*args)


_PAIR = 2 * C_BLOCK
ODD_BATCH = SUBLANES
ODD_STEPS = ROW_TILE // ODD_BATCH
_HIST_ROWS = (CONV_W - 1) * ODD_BATCH


def _odd_kernel(x_ref, hist_ref, h0_ref, g_ref, win_ref, cw_ref, cb_ref, wra_ref, bra_ref,
                wri_ref, bri_ref, lam_ref, wout_ref, gfin_ref,
                out_ref, conv_ref, lru_ref,
                xs_ref, gate_ref, a_ref, b_ref, y_ref, hcar_ref, *, nt):
    t = pl.program_id(1)
    half = C_WIDTH // 2

    @pl.when(t == 0)
    def _init_state():
        hist = hist_ref[...].reshape(_HIST_ROWS, C_WIDTH)
        for k in range(2):
            xs_ref[k, 0:_HIST_ROWS, :] = hist[:, k * half:(k + 1) * half]
        hcar_ref[...] = h0_ref[...]

    steps_h = ODD_STEPS // 2
    rows_h = ROW_TILE // 2
    xh, hh = [None, None], [None, None]

    def load_half(hf):
        x3 = x_ref[:, hf * steps_h:(hf + 1) * steps_h, :]
        xh[hf] = jnp.transpose(x3, (1, 0, 2)).reshape(rows_h, D_MODEL)
        hh[hf] = (_rms_scale(xh[hf]) * g_ref[...]).astype(BF16)

    def project_x(hf, k):
        r0 = _HIST_ROWS + hf * rows_h
        xs_ref[k, r0:r0 + rows_h, :] = _dot(hh[hf], win_ref[:, k * half:(k + 1) * half])

    def project_gate(hf):
        gate_ref[hf * rows_h:(hf + 1) * rows_h, :] = _dot(hh[hf], win_ref[:, C_WIDTH:2 * C_WIDTH])

    soft_lam = lam_ref[...]
    soft_lam = jnp.maximum(-soft_lam, 0.0) + jnp.log1p(jnp.exp(-jnp.abs(soft_lam)))

    def gate_pair(hf, m):
        c0 = m * _PAIR
        k, ck = divmod(c0, half)
        u = cb_ref[:, c0:c0 + _PAIR]
        for j in range(CONV_W):
            r = j * ODD_BATCH + hf * rows_h
            u = u + xs_ref[k, r:r + rows_h, ck:ck + _PAIR] * cw_ref[j:j + 1, c0:c0 + _PAIR]
        ub = u.astype(BF16)
        lo, hi = ub[:, 0:MXU_DIM], ub[:, LANES:LANES + MXU_DIM]

        def gate_tanh(w_ref, bias_ref):
            e = _dot(lo, w_ref[2 * m])
            o = _dot(hi, w_ref[2 * m + 1])
            full = jnp.concatenate([e[:, 0:LANES], e[:, LANES:] + o[:, 0:LANES], o[:, LANES:]], axis=1)
            return jnp.tanh(full + bias_ref[:, c0:c0 + _PAIR])

        t_r = gate_tanh(wra_ref, bra_ref)
        t_i = gate_tanh(wri_ref, bri_ref)
        c4 = (0.5 * LRU_C) * soft_lam[:, c0:c0 + _PAIR]
        neg_log_a = c4 * t_r + c4
        a = jnp.exp(-neg_log_a)
        rows = slice(hf * rows_h, (hf + 1) * rows_h)
        a_ref[rows, c0:c0 + _PAIR] = a
        x1 = jnp.tanh(neg_log_a) * (a * a + 1.0)
        root = jnp.where(x1 > 0.0, x1 * lax.rsqrt(x1), 0.0)
        hu = 0.5 * u
        b_ref[rows, c0:c0 + _PAIR] = root * (hu * t_i + hu)

    def scan_half(hf, hprev):
        pair = 2 * ODD_BATCH
        for i in range(rows_h // pair):
            r0 = hf * rows_h + i * pair
            lo8, hi8, both = slice(r0, r0 + ODD_BATCH), slice(r0 + ODD_BATCH, r0 + pair), slice(r0, r0 + pair)
            h1 = a_ref[lo8, :] * hprev + b_ref[lo8, :]
            hprev = a_ref[hi8, :] * h1 + b_ref[hi8, :]
            y_ref[both, :] = (jnp.concatenate([h1, hprev], axis=0) * _silu(gate_ref[both, :])).astype(BF16)
        return hprev

    def project_out(hf):
        return _dot(y_ref[hf * rows_h:(hf + 1) * rows_h, :], wout_ref[...])

    def finish_half(hf, proj):
        out = _rms_scale(xh[hf] + proj) * gfin_ref[...]
        out_ref[:, hf * steps_h:(hf + 1) * steps_h, :] = jnp.transpose(
            out.reshape(steps_h, ODD_BATCH, D_MODEL), (1, 0, 2))

    load_half(0)
    project_x(0, 0)
    load_half(1)
    project_x(0, 1)
    gate_pair(0, 0)
    project_x(1, 0)
    gate_pair(0, 1)
    project_x(1, 1)
    gate_pair(0, 2)
    project_gate(0)
    gate_pair(0, 3)
    gate_pair(1, 0)
    project_gate(1)
    gate_pair(1, 1)
    h_mid = scan_half(0, hcar_ref[...])
    proj0 = project_out(0)
    gate_pair(1, 2)
    gate_pair(1, 3)
    hcar_ref[...] = scan_half(1, h_mid)
    proj1 = project_out(1)
    finish_half(0, proj0)
    finish_half(1, proj1)
    for k in range(2):
        xs_ref[k, 0:_HIST_ROWS, :] = xs_ref[k, ROW_TILE:ROW_TILE + _HIST_ROWS, :]

    @pl.when(t == nt - 1)
    def _emit_state():
        for k in range(2):
            conv_ref[:, :, k * half:(k + 1) * half] = (
                xs_ref[k, 0:_HIST_ROWS, :].reshape(CONV_W - 1, ODD_BATCH, half))
        lru_ref[...] = hcar_ref[...]


def _odd_layer(x3, hist_t, h0, norm_g, w_in, conv_w, conv_b, w_ra, b_ra, w_ri, b_ri, lam, w_out, g_fin):
    batch, seq, _ = x3.shape
    nt = seq // ODD_STEPS
    nb = batch // ODD_BATCH
    n_hist = CONV_W - 1
    x_spec = pl.BlockSpec((ODD_BATCH, ODD_STEPS, D_MODEL), lambda b, t: (b, t, 0))
    hist_spec = pl.BlockSpec((n_hist, ODD_BATCH, C_WIDTH), lambda b, t: (0, b, 0))
    h_spec = pl.BlockSpec((ODD_BATCH, C_WIDTH), lambda b, t: (b, 0))
    consts = (norm_g, w_in, conv_w, conv_b, w_ra, b_ra, w_ri, b_ri, lam, w_out, g_fin)
    return pl.pallas_call(
        functools.partial(_odd_kernel, nt=nt),
        grid=(nb, nt),
        in_specs=[x_spec, hist_spec, h_spec] + [_const_spec(c.shape) for c in consts],
        out_specs=[x_spec, hist_spec, h_spec],
        out_shape=[jax.ShapeDtypeStruct((batch, seq, D_MODEL), F32),
                   jax.ShapeDtypeStruct((n_hist, batch, C_WIDTH), F32),
                   jax.ShapeDtypeStruct((batch, C_WIDTH), F32)],
        scratch_shapes=[pltpu.VMEM((2, _HIST_ROWS + ROW_TILE, C_WIDTH // 2), F32),
                        pltpu.VMEM((ROW_TILE, C_WIDTH), F32),
                        pltpu.VMEM((ROW_TILE, C_WIDTH), F32),
                        pltpu.VMEM((ROW_TILE, C_WIDTH), F32),
                        pltpu.VMEM((ROW_TILE, C_WIDTH), BF16),
                        pltpu.VMEM((ODD_BATCH, C_WIDTH), F32)],
        compiler_params=pltpu.CompilerParams(dimension_semantics=("arbitrary", "arbitrary"),
                                             vmem_limit_bytes=VMEM_LIMIT),
        name="odd_layer",
    )(x3, hist_t, h0, *consts)


def _bias_table(masked):
    slopes = 2.0 ** (-8.0 * jnp.arange(1, A_HEADS + 1, dtype=F32) / A_HEADS)
    dist = jnp.abs(WIN_ROWS + jnp.arange(CHUNK)[None, :] - jnp.arange(WIN_KEYS)[:, None]).astype(F32)
    alibi = slopes.reshape(A_KV_HEADS, 1, A_GROUP, 1) * dist[None, :, None, :]
    alibi = alibi.reshape(1, A_KV_HEADS, WIN_KEYS, A_GROUP * CHUNK)
    if not masked:
        return alibi
    first_chunk = jnp.arange(WIN_ROWS // CHUNK + 1)[:, None]
    key_pos = (first_chunk - WIN_ROWS // CHUNK) * CHUNK + jnp.arange(WIN_KEYS)[None, :]
    mask = jnp.where(key_pos < 0, F32(1e30), F32(0.0))
    return alibi + mask[:, None, :, None]


def _pad_gate_blocks(w):
    pad = MXU_DIM - C_BLOCK
    even = jnp.pad(w[0::2], ((0, 0), (0, pad), (0, pad)))
    odd = jnp.pad(w[1::2], ((0, 0), (pad, 0), (pad, 0)))
    return jnp.stack([even, odd], axis=1).reshape(C_BLOCKS, MXU_DIM, MXU_DIM).astype(BF16)


def kernel(x_prompt, x_sample, cache_swa_k, cache_swa_v, state_gla, cache_conv, state_lru, norm_even, w_in_even, w_gate_lr, b_gate_lr, sinks, gla_norm, w_out_even, norm_odd, w_in_odd, conv_w, conv_b, w_rg_a, b_rg_a, w_rg_i, b_rg_i, lru_lambda, w_out_odd, norm_final):
    batch, seq, _ = x_prompt.shape
    dbatch, dseq, _ = x_sample.shape
    row = lambda v: v.reshape(1, -1)

    lr0 = _OFF_GATE
    w_e = w_in_even[0]
    w_main = jnp.concatenate([w_e[:, :lr0].astype(BF16), w_e[:, lr0 + B_LOWRANK:].astype(BF16)], axis=1)
    w_lr1 = jnp.pad(w_e[:, lr0:lr0 + B_LOWRANK], ((0, 0), (0, LANES - B_LOWRANK))).astype(BF16)
    w_lr2 = jnp.pad(w_gate_lr[0], ((0, LANES - B_LOWRANK), (0, 0))).astype(BF16)
    w_out_e = w_out_even[0].astype(BF16)
    even_consts = (row(norm_even[0]), w_main, w_lr1, w_lr2, row(b_gate_lr[0]))
    sink_rows = jnp.repeat(sinks[0].reshape(A_KV_HEADS, 1, A_GROUP), CHUNK, axis=2)

    def even_layer(x, past):
        b, tl, _ = x.shape
        x2 = x.reshape(b * tl, D_MODEL)
        q, k, v, qg, kg, kd, bv, glast, sgate = _even_in(x2, *even_consts)
        if past is None:
            kprev, vprev, s0 = k, v, None
        else:
            kprev = jnp.transpose(past[0], (0, 2, 3, 1))
            vprev = jnp.transpose(past[1], (0, 2, 3, 1))
            s0 = past[2]
        out, s_new = _even_mix(q, k, v, kprev, vprev, qg, kg, kd, bv,
                               glast.reshape(-1, 1, B_QK_WIDTH), sgate, x2,
                               _bias_table(masked=past is None), sink_rows,
                               row(gla_norm[0]), w_out_e, s0, batch=b, seq=tl)
        keep = min(tl, WIN_ROWS)
        k4 = k.reshape(b, tl, A_KV_WIDTH)[:, tl - keep:].reshape(b, keep, A_KV_HEADS, A_HEAD_DIM)
        v4 = v.reshape(b, tl, A_KV_WIDTH)[:, tl - keep:].reshape(b, keep, A_KV_HEADS, A_HEAD_DIM)
        return out, k4, v4, s_new

    xp, pk, pv, pg = even_layer(x_prompt, None)
    xs, sk, sv, sg = even_layer(x_sample, (cache_swa_k[0], cache_swa_v[0], state_gla[0]))

    odd_consts = (row(norm_odd[0]), w_in_odd[0].astype(BF16), conv_w[0], row(conv_b[0]),
                  _pad_gate_blocks(0.5 * w_rg_a[0]), row(0.5 * b_rg_a[0]),
                  _pad_gate_blocks(0.5 * w_rg_i[0]), row(0.5 * b_rg_i[0]),
                  row(lru_lambda[0]), w_out_odd[0].astype(BF16), row(norm_final))
    zero_hist = jnp.zeros((CONV_W - 1, batch, C_WIDTH), F32)
    zero_h = jnp.zeros((batch, C_WIDTH), F32)
    yp, pc, plru = _odd_layer(xp.reshape(batch, seq, D_MODEL), zero_hist, zero_h, *odd_consts)
    ys, sc, slru = _odd_layer(xs.reshape(dbatch, dseq, D_MODEL), jnp.swapaxes(cache_conv[0], 0, 1),
                              state_lru[0], *odd_consts)

    return (yp, ys, pk[None], pv[None], pg[None],
            jnp.swapaxes(pc, 0, 1)[None], plru[None],
            sk[None], sv[None], sg[None], jnp.swapaxes(sc, 0, 1)[None], slru[None])
```

```python
import functools

import jax
import jax.numpy as jnp
import numpy as np
from jax import lax
from jax.experimental import pallas as pl
from jax.experimental.pallas import tpu as pltpu

F32 = jnp.float32
BF16 = jnp.bfloat16

D_MODEL = 1024
CHUNK = 64
EPS = 1e-6
A_HEADS = 16
A_KV_HEADS = 4
A_HEAD_DIM = 64
A_GROUP = A_HEADS // A_KV_HEADS
A_WIDTH = A_HEADS * A_HEAD_DIM
A_KV_WIDTH = A_KV_HEADS * A_HEAD_DIM
WIN_ROWS = 128
WIN_KEYS = WIN_ROWS + CHUNK
B_HEADS = 4
B_DK = 128
B_DV = 256
B_QK_WIDTH = B_HEADS * B_DK
B_WIDTH = B_HEADS * B_DV
B_LOWRANK = 16
B_GATE_NORM = 16.0
C_WIDTH = 1536
C_BLOCKS = 8
C_BLOCK = C_WIDTH // C_BLOCKS
CONV_W = 4
LRU_C = 8.0
MIX_EVEN = A_WIDTH + B_WIDTH

LANES = 128
SUBLANES = 8
MXU_DIM = 256
ROW_TILE = 512
VMEM_LIMIT = 48 * 1024 * 1024

_OFF_Q = 0
_OFF_K = _OFF_Q + A_WIDTH
_OFF_V = _OFF_K + A_KV_WIDTH
_OFF_BQ = _OFF_V + A_KV_WIDTH
_OFF_BK = _OFF_BQ + B_QK_WIDTH
_OFF_BV = _OFF_BK + B_QK_WIDTH
_OFF_GATE = _OFF_BV + B_WIDTH
_MAIN_WIDTH = _OFF_GATE + MIX_EVEN


def _const_spec(shape):
    nd = len(shape)
    return pl.BlockSpec(shape, lambda *_: (0,) * nd, pipeline_mode=pl.Buffered(1))


def _rms_scale(x):
    return x * lax.rsqrt(jnp.mean(x * x, axis=-1, keepdims=True) + EPS)


def _silu(x):
    half = 0.5 * x
    return half * jnp.tanh(half) + half


def _dot(a, b):
    return jnp.dot(a, b, preferred_element_type=F32)


def _dot_nt(a, b):
    return lax.dot_general(a, b, (((1,), (1,)), ((), ())), preferred_element_type=F32)


def _dot_tn(a, b):
    return lax.dot_general(a, b, (((0,), (0,)), ((), ())), preferred_element_type=F32)


def _even_in_kernel(x_ref, g_ref, w_ref, wlr1_ref, wlr2_ref, blr_ref,
                    q_ref, k_ref, v_ref, qg_ref, kg_ref, kd_ref, bv_ref, glast_ref, sgate_ref):
    h = (_rms_scale(x_ref[...]) * g_ref[...]).astype(BF16)

    def proj(off, width):
        return _dot(h, w_ref[:, off:off + width])

    q_ref[...] = (proj(_OFF_Q, A_WIDTH) * (A_HEAD_DIM ** -0.5)).astype(BF16)
    k_ref[...] = proj(_OFF_K, A_KV_WIDTH)
    v_ref[...] = proj(_OFF_V, A_KV_WIDTH)
    bv_ref[...] = proj(_OFF_BV, B_WIDTH).astype(BF16)
    sgate_ref[...] = _silu(proj(_OFF_GATE, MIX_EVEN)).astype(BF16)

    low = _dot(h, wlr1_ref[...])
    pre = _dot(low.astype(BF16), wlr2_ref[...]) + blr_ref[...]
    glog = (jnp.minimum(pre, 0.0) - jnp.log1p(jnp.exp(-jnp.abs(pre)))) * (1.0 / B_GATE_NORM)
    row_id = lax.broadcasted_iota(jnp.int32, (SUBLANES, B_QK_WIDTH), 0)
    groups = []
    for r in range(ROW_TILE // SUBLANES):
        g8 = glog[r * SUBLANES:(r + 1) * SUBLANES, :]
        for sh in (1, 2, 4):
            g8 = g8 + jnp.where(row_id >= sh, pltpu.roll(g8, sh, axis=0), 0.0)
        if r % (CHUNK // SUBLANES) != 0:
            g8 = g8 + groups[-1][SUBLANES - 1:SUBLANES, :]
        groups.append(g8)
    gcum = jnp.concatenate(groups, axis=0)

    bk = proj(_OFF_BK, B_QK_WIDTH)
    qg_ref[...] = (proj(_OFF_BQ, B_QK_WIDTH) * (B_DK ** -0.5) * jnp.exp(gcum)).astype(BF16)
    kg_ref[...] = (bk * jnp.exp(-gcum)).astype(BF16)
    for c in range(ROW_TILE // CHUNK):
        rows = slice(c * CHUNK, (c + 1) * CHUNK)
        glast = gcum[(c + 1) * CHUNK - 1:(c + 1) * CHUNK, :]
        glast_ref[c:c + 1, :] = glast
        kd_ref[rows, :] = (bk[rows] * jnp.exp(glast - gcum[rows])).astype(BF16)


def _even_in(x2, norm_g, w_main, w_lr1, w_lr2, b_lr):
    rows = x2.shape[0]
    nsteps = rows // ROW_TILE
    row_spec = lambda w: pl.BlockSpec((ROW_TILE, w), lambda i: (i, 0))
    chunks = ROW_TILE // CHUNK
    out_widths = (A_WIDTH, A_KV_WIDTH, A_KV_WIDTH, B_QK_WIDTH, B_QK_WIDTH, B_QK_WIDTH, B_WIDTH,
                  B_QK_WIDTH, MIX_EVEN)
    out_dtypes = (BF16, F32, F32, BF16, BF16, BF16, BF16, F32, BF16)
    out_rows = [ROW_TILE] * 7 + [chunks, ROW_TILE]
    return pl.pallas_call(
        _even_in_kernel,
        grid=(nsteps,),
        in_specs=[row_spec(D_MODEL), _const_spec(norm_g.shape), _const_spec(w_main.shape),
                  _const_spec(w_lr1.shape), _const_spec(w_lr2.shape), _const_spec(b_lr.shape)],
        out_specs=[pl.BlockSpec((r, w), lambda i: (i, 0)) for r, w in zip(out_rows, out_widths)],
        out_shape=[jax.ShapeDtypeStruct((nsteps * r, w), d)
                   for r, w, d in zip(out_rows, out_widths, out_dtypes)],
        compiler_params=pltpu.CompilerParams(dimension_semantics=("arbitrary",),
                                             vmem_limit_bytes=VMEM_LIMIT),
        name="even_in",
    )(x2, norm_g, w_main, w_lr1, w_lr2, b_lr)


def _even_mix_kernel(*refs, bb, tb, nt, has_past):
    if has_past:
        (q_ref, kc_ref, vc_ref, kp_ref, vp_ref, qg_ref, kg_ref, kd_ref, bv_ref, glast_ref,
         sgate_ref, x_ref, bias_ref, sink_ref, glag_ref, wout_ref, s0_ref,
         out_ref, sfin_ref, kbuf, vbuf, st_ref, y_ref) = refs
    else:
        (q_ref, kc_ref, vc_ref, kp_ref, vp_ref, qg_ref, kg_ref, kd_ref, bv_ref, glast_ref,
         sgate_ref, x_ref, bias_ref, sink_ref, glag_ref, wout_ref,
         out_ref, sfin_ref, kbuf, vbuf, st_ref, y_ref) = refs
        s0_ref = None
    t = pl.program_id(1)
    tc = tb // CHUNK

    one_lane = jnp.where(lax.broadcasted_iota(jnp.int32, (WIN_ROWS + tb, A_HEAD_DIM), 1) == 0,
                         1.0, 0.0).astype(BF16)
    for bi in range(bb):
        for kh in range(A_KV_HEADS):
            cols = slice(kh * A_HEAD_DIM, (kh + 1) * A_HEAD_DIM)
            if has_past:
                kbuf[bi, kh, 0:WIN_ROWS, :] = kp_ref[bi, kh].T.astype(BF16)
                vbuf[bi, kh, 0:WIN_ROWS, 0:A_HEAD_DIM] = vp_ref[bi, kh].T.astype(BF16)
            else:
                kbuf[bi, kh, 0:WIN_ROWS, :] = kp_ref[:, cols].astype(BF16)
                vbuf[bi, kh, 0:WIN_ROWS, 0:A_HEAD_DIM] = vp_ref[:, cols].astype(BF16)
            kbuf[bi, kh, WIN_ROWS:WIN_ROWS + tb, :] = kc_ref[bi * tb:(bi + 1) * tb, cols].astype(BF16)
            vbuf[bi, kh, WIN_ROWS:WIN_ROWS + tb, 0:A_HEAD_DIM] = (
                vc_ref[bi * tb:(bi + 1) * tb, cols].astype(BF16))
            vbuf[bi, kh, :, A_HEAD_DIM:2 * A_HEAD_DIM] = one_lane

    @pl.when(t == 0)
    def _init_state():
        for bi in range(bb):
            for hh in range(B_HEADS):
                if has_past:
                    st_ref[bi, hh] = s0_ref[bi, hh].T
                else:
                    st_ref[bi, hh] = jnp.zeros((B_DV, B_DK), F32)

    tril = (lax.broadcasted_iota(jnp.int32, (CHUNK, CHUNK), 0)
            >= lax.broadcasted_iota(jnp.int32, (CHUNK, CHUNK), 1))
    glag = glag_ref[...]
    tail_rows = 16
    tail_row0 = lax.broadcasted_iota(jnp.int32, (tail_rows, A_GROUP * CHUNK), 0) == 0
    tail_iota = lax.broadcasted_iota(jnp.int32, (tail_rows, 2 * A_HEAD_DIM), 0)
    tail_lane = lax.broadcasted_iota(jnp.int32, (tail_rows, 2 * A_HEAD_DIM), 1)
    v_tail = jnp.where((tail_iota == 0) & (tail_lane == A_HEAD_DIM), 1.0, 0.0).astype(BF16)

    def chunk_body(n, carry):
        bi = n // tc
        c = n % tc
        w0 = pl.multiple_of(c * CHUNK, CHUNK)
        rows = pl.ds(pl.multiple_of(n * CHUNK, CHUNK), CHUNK)

        variant = 0 if has_past else jnp.minimum(t * tc + c, WIN_ROWS // CHUNK)
        def attn_scores(kh):
            q4 = q_ref[rows, kh * A_GROUP * A_HEAD_DIM:(kh + 1) * A_GROUP * A_HEAD_DIM]
            qs = jnp.concatenate([q4[:, g * A_HEAD_DIM:(g + 1) * A_HEAD_DIM] for g in range(A_GROUP)],
                                 axis=0)
            kw = kbuf[bi, kh, pl.ds(w0, WIN_KEYS), :]
            return _dot_nt(kw, qs) - bias_ref[variant, kh]

        def attn_finish(kh, sg):
            cols = slice(kh * A_GROUP * A_HEAD_DIM, (kh + 1) * A_GROUP * A_HEAD_DIM)
            vw = vbuf[bi, kh, pl.ds(w0, WIN_KEYS), :]
            sink = sink_ref[kh]
            m = jnp.maximum(jnp.max(sg, axis=0, keepdims=True), sink)
            p = jnp.exp(sg - m).astype(BF16)
            p_sink = jnp.where(tail_row0, jnp.exp(sink - m), 0.0).astype(BF16)
            oe = _dot_tn(jnp.concatenate([vw, v_tail], axis=0),
                         jnp.concatenate([p, p_sink], axis=0))
            o = jnp.transpose(oe[0:A_HEAD_DIM, :] * (1.0 / oe[A_HEAD_DIM:A_HEAD_DIM + 1, :]))
            ao = jnp.concatenate([o[g * CHUNK:(g + 1) * CHUNK] for g in range(A_GROUP)], axis=1)
            y_ref[rows, cols] = (ao * sgate_ref[rows, cols].astype(F32)).astype(BF16)

        dec = jnp.exp(glast_ref[n])

        def gla_head(hh):
            ks = slice(hh * B_DK, (hh + 1) * B_DK)
            vs = slice(hh * B_DV, (hh + 1) * B_DV)
            qg = qg_ref[rows, ks]
            a = jnp.where(tril, _dot_nt(qg, kg_ref[rows, ks]), 0.0).astype(BF16)
            vh = bv_ref[rows, vs]
            st = st_ref[bi, hh]
            o = _dot(a, vh) + _dot_nt(qg, st.astype(BF16))
            st_ref[bi, hh] = st * dec[:, ks] + _dot_tn(vh, kd_ref[rows, ks])
            bo = _rms_scale(o) * glag
            cols = slice(A_WIDTH + hh * B_DV, A_WIDTH + (hh + 1) * B_DV)
            y_ref[rows, cols] = (bo * sgate_ref[rows, cols].astype(F32)).astype(BF16)

        scores = attn_scores(0)
        for kh in range(A_KV_HEADS):
            nxt = attn_scores(kh + 1) if kh + 1 < A_KV_HEADS else None
            gla_head(kh)
            attn_finish(kh, scores)
            scores = nxt
        return carry

    lax.fori_loop(0, bb * tc, chunk_body, 0, unroll=4)

    out_ref[...] = x_ref[...] + _dot(y_ref[...], wout_ref[...])

    @pl.when(t == nt - 1)
    def _emit_state():
        for bi in range(bb):
            for hh in range(B_HEADS):
                sfin_ref[bi, hh] = st_ref[bi, hh].T


def _even_mix(q, k, v, kprev, vprev, qg, kg, kd, bv, glast, sgate, x2, bias, sink_rows, gla_g, w_out, s0,
              *, batch, seq):
    has_past = s0 is not None
    tile = ROW_TILE // 2 if has_past else ROW_TILE
    tb = min(seq, tile)
    bb = tile // tb
    nt = seq // tb
    nb = batch // bb
    assert bb == 1 or nt == 1
    row_spec = lambda w: pl.BlockSpec((tile, w), lambda b, t: (b * nt + t, 0))
    if has_past:
        prev_spec = pl.BlockSpec((bb, A_KV_HEADS, A_HEAD_DIM, WIN_ROWS), lambda b, t: (b, 0, 0, 0))
    else:
        assert bb == 1
        per_b = seq // WIN_ROWS
        step = tb // WIN_ROWS
        prev_spec = pl.BlockSpec((WIN_ROWS, A_KV_WIDTH),
                                 lambda b, t: (b * per_b + jnp.maximum(t * step - 1, 0), 0))
    state_spec = pl.BlockSpec((bb, B_HEADS, B_DK, B_DV), lambda b, t: (b, 0, 0, 0))
    glast_spec = pl.BlockSpec((tile // CHUNK, 1, B_QK_WIDTH), lambda b, t: (b * nt + t, 0, 0))
    in_specs = [row_spec(A_WIDTH), row_spec(A_KV_WIDTH), row_spec(A_KV_WIDTH), prev_spec, prev_spec,
                row_spec(B_QK_WIDTH), row_spec(B_QK_WIDTH), row_spec(B_QK_WIDTH), row_spec(B_WIDTH),
                glast_spec, row_spec(MIX_EVEN), row_spec(D_MODEL),
                _const_spec(bias.shape), _const_spec(sink_rows.shape), _const_spec(gla_g.shape),
                _const_spec(w_out.shape)]
    args = [q, k, v, kprev, vprev, qg, kg, kd, bv, glast, sgate, x2, bias, sink_rows, gla_g, w_out]
    if has_past:
        in_specs.append(state_spec)
        args.append(s0)
    return pl.pallas_call(
        functools.partial(_even_mix_kernel, bb=bb, tb=tb, nt=nt, has_past=has_past),
        grid=(nb, nt),
        in_specs=in_specs,
        out_specs=[row_spec(D_MODEL), state_spec],
        out_shape=[jax.ShapeDtypeStruct((batch * seq, D_MODEL), F32),
                   jax.ShapeDtypeStruct((batch, B_HEADS, B_DK, B_DV), F32)],
        scratch_shapes=[pltpu.VMEM((bb, A_KV_HEADS, WIN_ROWS + tb, A_HEAD_DIM), BF16),
                        pltpu.VMEM((bb, A_KV_HEADS, WIN_ROWS + tb, 2 * A_HEAD_DIM), BF16),
                        pltpu.VMEM((bb, B_HEADS, B_DV, B_DK), F32),
                        pltpu.VMEM((tile, MIX_EVEN), BF16)],
        compiler_params=pltpu.CompilerParams(dimension_semantics=("arbitrary", "arbitrary"),
                                             vmem_limit_bytes=VMEM_LIMIT),
        name="even_mix",
    )(*args)


_PAIR = 2 * C_BLOCK
ODD_BATCH = SUBLANES
ODD_STEPS = ROW_TILE // ODD_BATCH
_HIST_ROWS = (CONV_W - 1) * ODD_BATCH


def _odd_kernel(x_ref, hist_ref, h0_ref, g_ref, win_ref, cw_ref, cb_ref, wra_ref, bra_ref,
                wri_ref, bri_ref, lam_ref, wout_ref, gfin_ref,
                out_ref, conv_ref, lru_ref,
                xs_ref, gate_ref, a_ref, b_ref, y_ref, hcar_ref, *, nt):
    t = pl.program_id(1)
    half = C_WIDTH // 2

    @pl.when(t == 0)
    def _init_state():
        hist = hist_ref[...].reshape(_HIST_ROWS, C_WIDTH)
        for k in range(2):
            xs_ref[k, 0:_HIST_ROWS, :] = hist[:, k * half:(k + 1) * half]
        hcar_ref[...] = h0_ref[...]

    steps_h = ODD_STEPS // 2
    rows_h = ROW_TILE // 2
    xh, hh = [None, None], [None, None]

    def load_half(hf):
        x3 = x_ref[:, hf * steps_h:(hf + 1) * steps_h, :]
        xh[hf] = jnp.transpose(x3, (1, 0, 2)).reshape(rows_h, D_MODEL)
        hh[hf] = (_rms_scale(xh[hf]) * g_ref[...]).astype(BF16)

    def project_x(hf, k):
        r0 = _HIST_ROWS + hf * rows_h
        xs_ref[k, r0:r0 + rows_h, :] = _dot(hh[hf], win_ref[:, k * half:(k + 1) * half])

    def project_gate(hf):
        gate_ref[hf * rows_h:(hf + 1) * rows_h, :] = _dot(hh[hf], win_ref[:, C_WIDTH:2 * C_WIDTH])

    soft_lam = lam_ref[...]
    soft_lam = jnp.maximum(-soft_lam, 0.0) + jnp.log1p(jnp.exp(-jnp.abs(soft_lam)))

    def gate_pair(hf, m):
        c0 = m * _PAIR
        k, ck = divmod(c0, half)
        u = cb_ref[:, c0:c0 + _PAIR]
        for j in range(CONV_W):
            r = j * ODD_BATCH + hf * rows_h
            u = u + xs_ref[k, r:r + rows_h, ck:ck + _PAIR] * cw_ref[j:j + 1, c0:c0 + _PAIR]
        ub = u.astype(BF16)
        lo, hi = ub[:, 0:MXU_DIM], ub[:, LANES:LANES + MXU_DIM]

        def gate_tanh(w_ref, bias_ref):
            e = _dot(lo, w_ref[2 * m])
            o = _dot(hi, w_ref[2 * m + 1])
            full = jnp.concatenate([e[:, 0:LANES], e[:, LANES:] + o[:, 0:LANES], o[:, LANES:]], axis=1)
            return jnp.tanh(full + bias_ref[:, c0:c0 + _PAIR])

        t_r = gate_tanh(wra_ref, bra_ref)
        t_i = gate_tanh(wri_ref, bri_ref)
        c4 = (0.5 * LRU_C) * soft_lam[:, c0:c0 + _PAIR]
        neg_log_a = c4 * t_r + c4
        a = jnp.exp(-neg_log_a)
        rows = slice(hf * rows_h, (hf + 1) * rows_h)
        a_ref[rows, c0:c0 + _PAIR] = a
        x1 = jnp.tanh(neg_log_a) * (a * a + 1.0)
        root = jnp.where(x1 > 0.0, x1 * lax.rsqrt(x1), 0.0)
        hu = 0.5 * u
        b_ref[rows, c0:c0 + _PAIR] = root * (hu * t_i + hu)

    def scan_half(hf, hprev):
        pair = 2 * ODD_BATCH
        for i in range(rows_h // pair):
            r0 = hf * rows_h + i * pair
            lo8, hi8, both = slice(r0, r0 + ODD_BATCH), slice(r0 + ODD_BATCH, r0 + pair), slice(r0, r0 + pair)
            h1 = a_ref[lo8, :] * hprev + b_ref[lo8, :]
            hprev = a_ref[hi8, :] * h1 + b_ref[hi8, :]
            y_ref[both, :] = (jnp.concatenate([h1, hprev], axis=0) * _silu(gate_ref[both, :])).astype(BF16)
        return hprev

    def project_out(hf):
        return _dot(y_ref[hf * rows_h:(hf + 1) * rows_h, :], wout_ref[...])

    def finish_half(hf, proj):
        out = _rms_scale(xh[hf] + proj) * gfin_ref[...]
        out_ref[:, hf * steps_h:(hf + 1) * steps_h, :] = jnp.transpose(
            out.reshape(steps_h, ODD_BATCH, D_MODEL), (1, 0, 2))

    load_half(0)
    project_x(0, 0)
    load_half(1)
    project_x(0, 1)
    gate_pair(0, 0)
    project_x(1, 0)
    gate_pair(0, 1)
    project_x(1, 1)
    gate_pair(0, 2)
    project_gate(0)
    gate_pair(0, 3)
    gate_pair(1, 0)
    project_gate(1)
    gate_pair(1, 1)
    h_mid = scan_half(0, hcar_ref[...])
    proj0 = project_out(0)
    gate_pair(1, 2)
    gate_pair(1, 3)
    hcar_ref[...] = scan_half(1, h_mid)
    proj1 = project_out(1)
    finish_half(0, proj0)
    finish_half(1, proj1)
    for k in range(2):
        xs_ref[k, 0:_HIST_ROWS, :] = xs_ref[k, ROW_TILE:ROW_TILE + _HIST_ROWS, :]

    @pl.when(t == nt - 1)
    def _emit_state():
        for k in range(2):
            conv_ref[:, :, k * half:(k + 1) * half] = (
                xs_ref[k, 0:_HIST_ROWS, :].reshape(CONV_W - 1, ODD_BATCH, half))
        lru_ref[...] = hcar_ref[...]


def _odd_layer(x3, hist_t, h0, norm_g, w_in, conv_w, conv_b, w_ra, b_ra, w_ri, b_ri, lam, w_out, g_fin):
    batch, seq, _ = x3.shape
    nt = seq // ODD_STEPS
    nb = batch // ODD_BATCH
    n_hist = CONV_W - 1
    x_spec = pl.BlockSpec((ODD_BATCH, ODD_STEPS, D_MODEL), lambda b, t: (b, t, 0))
    hist_spec = pl.BlockSpec((n_hist, ODD_BATCH, C_WIDTH), lambda b, t: (0, b, 0))
    h_spec = pl.BlockSpec((ODD_BATCH, C_WIDTH), lambda b, t: (b, 0))
    consts = (norm_g, w_in, conv_w, conv_b, w_ra, b_ra, w_ri, b_ri, lam, w_out, g_fin)
    return pl.pallas_call(
        functools.partial(_odd_kernel, nt=nt),
        grid=(nb, nt),
        in_specs=[x_spec, hist_spec, h_spec] + [_const_spec(c.shape) for c in consts],
        out_specs=[x_spec, hist_spec, h_spec],
        out_shape=[jax.ShapeDtypeStruct((batch, seq, D_MODEL), F32),
                   jax.ShapeDtypeStruct((n_hist, batch, C_WIDTH), F32),
                   jax.ShapeDtypeStruct((batch, C_WIDTH), F32)],
        scratch_shapes=[pltpu.VMEM((2, _HIST_ROWS + ROW_TILE, C_WIDTH // 2), F32),
                        pltpu.VMEM((ROW_TILE, C_WIDTH), F32),
                        pltpu.VMEM((ROW_TILE, C_WIDTH), F32),
                        pltpu.VMEM((ROW_TILE, C_WIDTH), F32),
                        pltpu.VMEM((ROW_TILE, C_WIDTH), BF16),
                        pltpu.VMEM((ODD_BATCH, C_WIDTH), F32)],
        compiler_params=pltpu.CompilerParams(dimension_semantics=("arbitrary", "arbitrary"),
                                             vmem_limit_bytes=VMEM_LIMIT),
        name="odd_layer",
    )(x3, hist_t, h0, *consts)


def _bias_table(masked):
    slopes = 2.0 ** (-8.0 * jnp.arange(1, A_HEADS + 1, dtype=F32) / A_HEADS)
    dist = jnp.abs(WIN_ROWS + jnp.arange(CHUNK)[None, :] - jnp.arange(WIN_KEYS)[:, None]).astype(F32)
    alibi = slopes.reshape(A_KV_HEADS, 1, A_GROUP, 1) * dist[None, :, None, :]
    alibi = alibi.reshape(1, A_KV_HEADS, WIN_KEYS, A_GROUP * CHUNK)
    if not masked:
        return alibi
    first_chunk = jnp.arange(WIN_ROWS // CHUNK + 1)[:, None]
    key_pos = (first_chunk - WIN_ROWS // CHUNK) * CHUNK + jnp.arange(WIN_KEYS)[None, :]
    mask = jnp.where(key_pos < 0, F32(1e30), F32(0.0))
    return alibi + mask[:, None, :, None]


def _pad_gate_blocks(w):
    pad = MXU_DIM - C_BLOCK
    even = jnp.pad(w[0::2], ((0, 0), (0, pad), (0, pad)))
    odd = jnp.pad(w[1::2], ((0, 0), (pad, 0), (pad, 0)))
    return jnp.stack([even, odd], axis=1).reshape(C_BLOCKS, MXU_DIM, MXU_DIM).astype(BF16)


def kernel(x_prompt, x_sample, cache_swa_k, cache_swa_v, state_gla, cache_conv, state_lru, norm_even, w_in_even, w_gate_lr, b_gate_lr, sinks, gla_norm, w_out_even, norm_odd, w_in_odd, conv_w, conv_b, w_rg_a, b_rg_a, w_rg_i, b_rg_i, lru_lambda, w_out_odd, norm_final):
    batch, seq, _ = x_prompt.shape
    dbatch, dseq, _ = x_sample.shape
    row = lambda v: v.reshape(1, -1)

    lr0 = _OFF_GATE
    w_e = w_in_even[0]
    w_main = jnp.concatenate([w_e[:, :lr0].astype(BF16), w_e[:, lr0 + B_LOWRANK:].astype(BF16)], axis=1)
    w_lr1 = jnp.pad(w_e[:, lr0:lr0 + B_LOWRANK], ((0, 0), (0, LANES - B_LOWRANK))).astype(BF16)
    w_lr2 = jnp.pad(w_gate_lr[0], ((0, LANES - B_LOWRANK), (0, 0))).astype(BF16)
    w_out_e = w_out_even[0].astype(BF16)
    even_consts = (row(norm_even[0]), w_main, w_lr1, w_lr2, row(b_gate_lr[0]))
    sink_rows = jnp.repeat(sinks[0].reshape(A_KV_HEADS, 1, A_GROUP), CHUNK, axis=2)

    def even_layer(x, past):
        b, tl, _ = x.shape
        x2 = x.reshape(b * tl, D_MODEL)
        q, k, v, qg, kg, kd, bv, glast, sgate = _even_in(x2, *even_consts)
        if past is None:
            kprev, vprev, s0 = k, v, None
        else:
            kprev = jnp.transpose(past[0], (0, 2, 3, 1))
            vprev = jnp.transpose(past[1], (0, 2, 3, 1))
            s0 = past[2]
        out, s_new = _even_mix(q, k, v, kprev, vprev, qg, kg, kd, bv,
                               glast.reshape(-1, 1, B_QK_WIDTH), sgate, x2,
                               _bias_table(masked=past is None), sink_rows,
                               row(gla_norm[0]), w_out_e, s0, batch=b, seq=tl)
        keep = min(tl, WIN_ROWS)
        k4 = k.reshape(b, tl, A_KV_WIDTH)[:, tl - keep:].reshape(b, keep, A_KV_HEADS, A_HEAD_DIM)
        v4 = v.reshape(b, tl, A_KV_WIDTH)[:, tl - keep:].reshape(b, keep, A_KV_HEADS, A_HEAD_DIM)
        return out, k4, v4, s_new

    xp, pk, pv, pg = even_layer(x_prompt, None)
    xs, sk, sv, sg = even_layer(x_sample, (cache_swa_k[0], cache_swa_v[0], state_gla[0]))

    odd_consts = (row(norm_odd[0]), w_in_odd[0].astype(BF16), conv_w[0], row(conv_b[0]),
                  _pad_gate_blocks(0.5 * w_rg_a[0]), row(0.5 * b_rg_a[0]),
                  _pad_gate_blocks(0.5 * w_rg_i[0]), row(0.5 * b_rg_i[0]),
                  row(lru_lambda[0]), w_out_odd[0].astype(BF16), row(norm_final))
    zero_hist = jnp.zeros((CONV_W - 1, batch, C_WIDTH), F32)
    zero_h = jnp.zeros((batch, C_WIDTH), F32)
    yp, pc, plru = _odd_layer(xp.reshape(batch, seq, D_MODEL), zero_hist, zero_h, *odd_consts)
    ys, sc, slru = _odd_layer(xs.reshape(dbatch, dseq, D_MODEL), jnp.swapaxes(cache_conv[0], 0, 1),
                              state_lru[0], *odd_consts)

    return (yp, ys, pk[None], pv[None], pg[None],
            jnp.swapaxes(pc, 0, 1)[None], plru[None],
            sk[None], sv[None], sg[None], jnp.swapaxes(sc, 0, 1)[None], slru[None])
```

```python
import functools

import jax
import jax.numpy as jnp
import numpy as np
from jax import lax
from jax.experimental import pallas as pl
from jax.experimental.pallas import tpu as pltpu

F32 = jnp.float32
BF16 = jnp.bfloat16

D_MODEL = 1024
CHUNK = 64
EPS = 1e-6
A_HEADS = 16
A_KV_HEADS = 4
A_HEAD_DIM = 64
A_GROUP = A_HEADS // A_KV_HEADS
A_WIDTH = A_HEADS * A_HEAD_DIM
A_KV_WIDTH = A_KV_HEADS * A_HEAD_DIM
WIN_ROWS = 128
WIN_KEYS = WIN_ROWS + CHUNK
B_HEADS = 4
B_DK = 128
B_DV = 256
B_QK_WIDTH = B_HEADS * B_DK
B_WIDTH = B_HEADS * B_DV
B_LOWRANK = 16
B_GATE_NORM = 16.0
C_WIDTH = 1536
C_BLOCKS = 8
C_BLOCK = C_WIDTH // C_BLOCKS
CONV_W = 4
LRU_C = 8.0
MIX_EVEN = A_WIDTH + B_WIDTH

LANES = 128
SUBLANES = 8
MXU_DIM = 256
ROW_TILE = 512
VMEM_LIMIT = 48 * 1024 * 1024

_OFF_Q = 0
_OFF_K = _OFF_Q + A_WIDTH
_OFF_V = _OFF_K + A_KV_WIDTH
_OFF_BQ = _OFF_V + A_KV_WIDTH
_OFF_BK = _OFF_BQ + B_QK_WIDTH
_OFF_BV = _OFF_BK + B_QK_WIDTH
_OFF_GATE = _OFF_BV + B_WIDTH
_MAIN_WIDTH = _OFF_GATE + MIX_EVEN


def _const_spec(shape):
    nd = len(shape)
    return pl.BlockSpec(shape, lambda *_: (0,) * nd, pipeline_mode=pl.Buffered(1))


def _rms_scale(x):
    return x * lax.rsqrt(jnp.mean(x * x, axis=-1, keepdims=True) + EPS)


def _silu(x):
    half = 0.5 * x
    return half * jnp.tanh(half) + half


def _dot(a, b):
    return jnp.dot(a, b, preferred_element_type=F32)


def _dot_nt(a, b):
    return lax.dot_general(a, b, (((1,), (1,)), ((), ())), preferred_element_type=F32)


def _dot_tn(a, b):
    return lax.dot_general(a, b, (((0,), (0,)), ((), ())), preferred_element_type=F32)


def _even_in_kernel(x_ref, g_ref, w_ref, wlr1_ref, wlr2_ref, blr_ref,
                    q_ref, k_ref, v_ref, qg_ref, kg_ref, kd_ref, bv_ref, glast_ref, sgate_ref):
    h = (_rms_scale(x_ref[...]) * g_ref[...]).astype(BF16)

    def proj(off, width):
        return _dot(h, w_ref[:, off:off + width])

    q_ref[...] = (proj(_OFF_Q, A_WIDTH) * (A_HEAD_DIM ** -0.5)).astype(BF16)
    k_ref[...] = proj(_OFF_K, A_KV_WIDTH)
    v_ref[...] = proj(_OFF_V, A_KV_WIDTH)
    bv_ref[...] = proj(_OFF_BV, B_WIDTH).astype(BF16)
    sgate_ref[...] = _silu(proj(_OFF_GATE, MIX_EVEN)).astype(BF16)

    low = _dot(h, wlr1_ref[...])
    pre = _dot(low.astype(BF16), wlr2_ref[...]) + blr_ref[...]
    glog = (jnp.minimum(pre, 0.0) - jnp.log1p(jnp.exp(-jnp.abs(pre)))) * (1.0 / B_GATE_NORM)
    row_id = lax.broadcasted_iota(jnp.int32, (SUBLANES, B_QK_WIDTH), 0)
    groups = []
    for r in range(ROW_TILE // SUBLANES):
        g8 = glog[r * SUBLANES:(r + 1) * SUBLANES, :]
        for sh in (1, 2, 4):
            g8 = g8 + jnp.where(row_id >= sh, pltpu.roll(g8, sh, axis=0), 0.0)
        if r % (CHUNK // SUBLANES) != 0:
            g8 = g8 + groups[-1][SUBLANES - 1:SUBLANES, :]
        groups.append(g8)
    gcum = jnp.concatenate(groups, axis=0)

    bk = proj(_OFF_BK, B_QK_WIDTH)
    qg_ref[...] = (proj(_OFF_BQ, B_QK_WIDTH) * (B_DK ** -0.5) * jnp.exp(gcum)).astype(BF16)
    kg_ref[...] = (bk * jnp.exp(-gcum)).astype(BF16)
    for c in range(ROW_TILE // CHUNK):
        rows = slice(c * CHUNK, (c + 1) * CHUNK)
        glast = gcum[(c + 1) * CHUNK - 1:(c + 1) * CHUNK, :]
        glast_ref[c:c + 1, :] = glast
        kd_ref[rows, :] = (bk[rows] * jnp.exp(glast - gcum[rows])).astype(BF16)


def _even_in(x2, norm_g, w_main, w_lr1, w_lr2, b_lr):
    rows = x2.shape[0]
    nsteps = rows // ROW_TILE
    row_spec = lambda w: pl.BlockSpec((ROW_TILE, w), lambda i: (i, 0))
    chunks = ROW_TILE // CHUNK
    out_widths = (A_WIDTH, A_KV_WIDTH, A_KV_WIDTH, B_QK_WIDTH, B_QK_WIDTH, B_QK_WIDTH, B_WIDTH,
                  B_QK_WIDTH, MIX_EVEN)
    out_dtypes = (BF16, F32, F32, BF16, BF16, BF16, BF16, F32, BF16)
    out_rows = [ROW_TILE] * 7 + [chunks, ROW_TILE]
    return pl.pallas_call(
        _even_in_kernel,
        grid=(nsteps,),
        in_specs=[row_spec(D_MODEL), _const_spec(norm_g.shape), _const_spec(w_main.shape),
                  _const_spec(w_lr1.shape), _const_spec(w_lr2.shape), _const_spec(b_lr.shape)],
        out_specs=[pl.BlockSpec((r, w), lambda i: (i, 0)) for r, w in zip(out_rows, out_widths)],
        out_shape=[jax.ShapeDtypeStruct((nsteps * r, w), d)
                   for r, w, d in zip(out_rows, out_widths, out_dtypes)],
        compiler_params=pltpu.CompilerParams(dimension_semantics=("arbitrary",),
                                             vmem_limit_bytes=VMEM_LIMIT),
        name="even_in",
    )(x2, norm_g, w_main, w_lr1, w_lr2, b_lr)


def _even_mix_kernel(*refs, bb, tb, nt, has_past):
    if has_past:
        (q_ref, kc_ref, vc_ref, kp_ref, vp_ref, qg_ref, kg_ref, kd_ref, bv_ref, glast_ref,
         sgate_ref, x_ref, bias_ref, sink_ref, glag_ref, wout_ref, s0_ref,
         out_ref, sfin_ref, kbuf, vbuf, st_ref, y_ref) = refs
    else:
        (q_ref, kc_ref, vc_ref, kp_ref, vp_ref, qg_ref, kg_ref, kd_ref, bv_ref, glast_ref,
         sgate_ref, x_ref, bias_ref, sink_ref, glag_ref, wout_ref,
         out_ref, sfin_ref, kbuf, vbuf, st_ref, y_ref) = refs
        s0_ref = None
    t = pl.program_id(1)
    tc = tb // CHUNK

    one_lane = jnp.where(lax.broadcasted_iota(jnp.int32, (WIN_ROWS + tb, A_HEAD_DIM), 1) == 0,
                         1.0, 0.0).astype(BF16)
    for bi in range(bb):
        for kh in range(A_KV_HEADS):
            cols = slice(kh * A_HEAD_DIM, (kh + 1) * A_HEAD_DIM)
            if has_past:
                kbuf[bi, kh, 0:WIN_ROWS, :] = kp_ref[bi, kh].T.astype(BF16)
                vbuf[bi, kh, 0:WIN_ROWS, 0:A_HEAD_DIM] = vp_ref[bi, kh].T.astype(BF16)
            else:
                kbuf[bi, kh, 0:WIN_ROWS, :] = kp_ref[:, cols].astype(BF16)
                vbuf[bi, kh, 0:WIN_ROWS, 0:A_HEAD_DIM] = vp_ref[:, cols].astype(BF16)
            kbuf[bi, kh, WIN_ROWS:WIN_ROWS + tb, :] = kc_ref[bi * tb:(bi + 1) * tb, cols].astype(BF16)
            vbuf[bi, kh, WIN_ROWS:WIN_ROWS + tb, 0:A_HEAD_DIM] = (
                vc_ref[bi * tb:(bi + 1) * tb, cols].astype(BF16))
            vbuf[bi, kh, :, A_HEAD_DIM:2 * A_HEAD_DIM] = one_lane

    @pl.when(t == 0)
    def _init_state():
        for bi in range(bb):
            for hh in range(B_HEADS):
                if has_past:
                    st_ref[bi, hh] = s0_ref[bi, hh].T
                else:
                    st_ref[bi, hh] = jnp.zeros((B_DV, B_DK), F32)

    tril = (lax.broadcasted_iota(jnp.int32, (CHUNK, CHUNK), 0)
            >= lax.broadcasted_iota(jnp.int32, (CHUNK, CHUNK), 1))
    glag = glag_ref[...]
    tail_rows = 16
    tail_row0 = lax.broadcasted_iota(jnp.int32, (tail_rows, A_GROUP * CHUNK), 0) == 0
    tail_iota = lax.broadcasted_iota(jnp.int32, (tail_rows, 2 * A_HEAD_DIM), 0)
    tail_lane = lax.broadcasted_iota(jnp.int32, (tail_rows, 2 * A_HEAD_DIM), 1)
    v_tail = jnp.where((tail_iota == 0) & (tail_lane == A_HEAD_DIM), 1.0, 0.0).astype(BF16)

    def chunk_body(n, carry):
        bi = n // tc
        c = n % tc
        w0 = pl.multiple_of(c * CHUNK, CHUNK)
        rows = pl.ds(pl.multiple_of(n * CHUNK, CHUNK), CHUNK)

        variant = 0 if has_past else jnp.minimum(t * tc + c, WIN_ROWS // CHUNK)
        def attn_scores(kh):
            q4 = q_ref[rows, kh * A_GROUP * A_HEAD_DIM:(kh + 1) * A_GROUP * A_HEAD_DIM]
            qs = jnp.concatenate([q4[:, g * A_HEAD_DIM:(g + 1) * A_HEAD_DIM] for g in range(A_GROUP)],
                                 axis=0)
            kw = kbuf[bi, kh, pl.ds(w0, WIN_KEYS), :]
            return _dot_nt(kw, qs) - bias_ref[variant, kh]

        def attn_finish(kh, sg):
            cols = slice(kh * A_GROUP * A_HEAD_DIM, (kh + 1) * A_GROUP * A_HEAD_DIM)
            vw = vbuf[bi, kh, pl.ds(w0, WIN_KEYS), :]
            sink = sink_ref[kh]
            m = jnp.maximum(jnp.max(sg, axis=0, keepdims=True), sink)
            p = jnp.exp(sg - m).astype(BF16)
            p_sink = jnp.where(tail_row0, jnp.exp(sink - m), 0.0).astype(BF16)
            oe = _dot_tn(jnp.concatenate([vw, v_tail], axis=0),
                         jnp.concatenate([p, p_sink], axis=0))
            o = jnp.transpose(oe[0:A_HEAD_DIM, :] * (1.0 / oe[A_HEAD_DIM:A_HEAD_DIM + 1, :]))
            ao = jnp.concatenate([o[g * CHUNK:(g + 1) * CHUNK] for g in range(A_GROUP)], axis=1)
            y_ref[rows, cols] = (ao * sgate_ref[rows, cols].astype(F32)).astype(BF16)

        dec = jnp.exp(glast_ref[n])

        def gla_head(hh):
            ks = slice(hh * B_DK, (hh + 1) * B_DK)
            vs = slice(hh * B_DV, (hh + 1) * B_DV)
            qg = qg_ref[rows, ks]
            a = jnp.where(tril, _dot_nt(qg, kg_ref[rows, ks]), 0.0).astype(BF16)
            vh = bv_ref[rows, vs]
            st = st_ref[bi, hh]
            o = _dot(a, vh) + _dot_nt(qg, st.astype(BF16))
            st_ref[bi, hh] = st * dec[:, ks] + _dot_tn(vh, kd_ref[rows, ks])
            bo = _rms_scale(o) * glag
            cols = slice(A_WIDTH + hh * B_DV, A_WIDTH + (hh + 1) * B_DV)
            y_ref[rows, cols] = (bo * sgate_ref[rows, cols].astype(F32)).astype(BF16)

        scores = attn_scores(0)
        for kh in range(A_KV_HEADS):
            nxt = attn_scores(kh + 1) if kh + 1 < A_KV_HEADS else None
            gla_head(kh)
            attn_finish(kh, scores)
            scores = nxt
        return carry

    lax.fori_loop(0, bb * tc, chunk_body, 0, unroll=True)

    out_ref[...] = x_ref[...] + _dot(y_ref[...], wout_ref[...])

    @pl.when(t == nt - 1)
    def _emit_state():
        for bi in range(bb):
            for hh in range(B_HEADS):
                sfin_ref[bi, hh] = st_ref[bi, hh].T


def _even_mix(q, k, v, kprev, vprev, qg, kg, kd, bv, glast, sgate, x2, bias, sink_rows, gla_g, w_out, s0,
              *, batch, seq):
    has_past = s0 is not None
    tile = ROW_TILE // 2 if has_past else ROW_TILE
    tb = min(seq, tile)
    bb = tile // tb
    nt = seq // tb
    nb = batch // bb
    assert bb == 1 or nt == 1
    row_spec = lambda w: pl.BlockSpec((tile, w), lambda b, t: (b * nt + t, 0))
    if has_past:
        prev_spec = pl.BlockSpec((bb, A_KV_HEADS, A_HEAD_DIM, WIN_ROWS), lambda b, t: (b, 0, 0, 0))
    else:
        assert bb == 1
        per_b = seq // WIN_ROWS
        step = tb // WIN_ROWS
        prev_spec = pl.BlockSpec((WIN_ROWS, A_KV_WIDTH),
                                 lambda b, t: (b * per_b + jnp.maximum(t * step - 1, 0), 0))
    state_spec = pl.BlockSpec((bb, B_HEADS, B_DK, B_DV), lambda b, t: (b, 0, 0, 0))
    glast_spec = pl.BlockSpec((tile // CHUNK, 1, B_QK_WIDTH), lambda b, t: (b * nt + t, 0, 0))
    in_specs = [row_spec(A_WIDTH), row_spec(A_KV_WIDTH), row_spec(A_KV_WIDTH), prev_spec, prev_spec,
                row_spec(B_QK_WIDTH), row_spec(B_QK_WIDTH), row_spec(B_QK_WIDTH), row_spec(B_WIDTH),
                glast_spec, row_spec(MIX_EVEN), row_spec(D_MODEL),
                _const_spec(bias.shape), _const_spec(sink_rows.shape), _const_spec(gla_g.shape),
                _const_spec(w_out.shape)]
    args = [q, k, v, kprev, vprev, qg, kg, kd, bv, glast, sgate, x2, bias, sink_rows, gla_g, w_out]
    if has_past:
        in_specs.append(state_spec)
        args.append(s0)
    return pl.pallas_call(
        functools.partial(_even_mix_kernel, bb=bb, tb=tb, nt=nt, has_past=has_past),
        grid=(nb, nt),
        in_specs=in_specs,
        out_specs=[row_spec(D_MODEL), state_spec],
        out_shape=[jax.ShapeDtypeStruct((batch * seq, D_MODEL), F32),
                   jax.ShapeDtypeStruct((batch, B_HEADS, B_DK, B_DV), F32)],
        scratch_shapes=[pltpu.VMEM((bb, A_KV_HEADS, WIN_ROWS + tb, A_HEAD_DIM), BF16),
                        pltpu.VMEM((bb, A_KV_HEADS, WIN_ROWS + tb, 2 * A_HEAD_DIM), BF16),
                        pltpu.VMEM((bb, B_HEADS, B_DV, B_DK), F32),
                        pltpu.VMEM((tile, MIX_EVEN), BF16)],
        compiler_params=pltpu.CompilerParams(dimension_semantics=("arbitrary", "arbitrary"),
                                             vmem_limit_bytes=VMEM_LIMIT),
        name="even_mix",
    )(*args)


_PAIR = 2 * C_BLOCK
ODD_BATCH = SUBLANES
ODD_STEPS = ROW_TILE // ODD_BATCH
_HIST_ROWS = (CONV_W - 1) * ODD_BATCH


def _odd_kernel(x_ref, hist_ref, h0_ref, g_ref, win_ref, cw_ref, cb_ref, wra_ref, bra_ref,
                wri_ref, bri_ref, lam_ref, wout_ref, gfin_ref,
                out_ref, conv_ref, lru_ref,
                xs_ref, gate_ref, a_ref, b_ref, y_ref, hcar_ref, *, nt):
    t = pl.program_id(1)
    half = C_WIDTH // 2

    @pl.when(t == 0)
    def _init_state():
        hist = hist_ref[...].reshape(_HIST_ROWS, C_WIDTH)
        for k in range(2):
            xs_ref[k, 0:_HIST_ROWS, :] = hist[:, k * half:(k + 1) * half]
        hcar_ref[...] = h0_ref[...]

    steps_h = ODD_STEPS // 2
    rows_h = ROW_TILE // 2
    xh, hh = [None, None], [None, None]

    def load_half(hf):
        x3 = x_ref[:, hf * steps_h:(hf + 1) * steps_h, :]
        xh[hf] = jnp.transpose(x3, (1, 0, 2)).reshape(rows_h, D_MODEL)
        hh[hf] = (_rms_scale(xh[hf]) * g_ref[...]).astype(BF16)

    def project_x(hf, k):
        r0 = _HIST_ROWS + hf * rows_h
        xs_ref[k, r0:r0 + rows_h, :] = _dot(hh[hf], win_ref[:, k * half:(k + 1) * half])

    def project_gate(hf):
        gate_ref[hf * rows_h:(hf + 1) * rows_h, :] = _dot(hh[hf], win_ref[:, C_WIDTH:2 * C_WIDTH])

    soft_lam = lam_ref[...]
    soft_lam = jnp.maximum(-soft_lam, 0.0) + jnp.log1p(jnp.exp(-jnp.abs(soft_lam)))

    def gate_pair(hf, m, then_issue=None):
        c0 = m * _PAIR
        k, ck = divmod(c0, half)
        u = cb_ref[:, c0:c0 + _PAIR]
        for j in range(CONV_W):
            r = j * ODD_BATCH + hf * rows_h
            u = u + xs_ref[k, r:r + rows_h, ck:ck + _PAIR] * cw_ref[j:j + 1, c0:c0 + _PAIR]
        ub = u.astype(BF16)
        lo, hi = ub[:, 0:MXU_DIM], ub[:, LANES:LANES + MXU_DIM]

        def gate_pre(w_ref):
            e = _dot(lo, w_ref[2 * m])
            o = _dot(hi, w_ref[2 * m + 1])
            return jnp.concatenate([e[:, 0:LANES], e[:, LANES:] + o[:, 0:LANES], o[:, LANES:]], axis=1)

        z_r = gate_pre(wra_ref)
        z_i = gate_pre(wri_ref)
        if then_issue is not None:
            then_issue()
        t_r = jnp.tanh(z_r + bra_ref[:, c0:c0 + _PAIR])
        t_i = jnp.tanh(z_i + bri_ref[:, c0:c0 + _PAIR])
        c4 = (0.5 * LRU_C) * soft_lam[:, c0:c0 + _PAIR]
        neg_log_a = c4 * t_r + c4
        a = jnp.exp(-neg_log_a)
        rows = slice(hf * rows_h, (hf + 1) * rows_h)
        a_ref[rows, c0:c0 + _PAIR] = a
        x1 = jnp.tanh(neg_log_a) * (a * a + 1.0)
        root = jnp.where(x1 > 0.0, x1 * lax.rsqrt(x1), 0.0)
        hu = 0.5 * u
        b_ref[rows, c0:c0 + _PAIR] = root * (hu * t_i + hu)

    def scan_half(hf, hprev):
        pair = 2 * ODD_BATCH
        for i in range(rows_h // pair):
            r0 = hf * rows_h + i * pair
            lo8, hi8, both = slice(r0, r0 + ODD_BATCH), slice(r0 + ODD_BATCH, r0 + pair), slice(r0, r0 + pair)
            h1 = a_ref[lo8, :] * hprev + b_ref[lo8, :]
            hprev = a_ref[hi8, :] * h1 + b_ref[hi8, :]
            y_ref[both, :] = (jnp.concatenate([h1, hprev], axis=0) * _silu(gate_ref[both, :])).astype(BF16)
        return hprev

    def project_out(hf):
        return _dot(y_ref[hf * rows_h:(hf + 1) * rows_h, :], wout_ref[...])

    def finish_half(hf, proj):
        out = _rms_scale(xh[hf] + proj) * gfin_ref[...]
        out_ref[:, hf * steps_h:(hf + 1) * steps_h, :] = jnp.transpose(
            out.reshape(steps_h, ODD_BATCH, D_MODEL), (1, 0, 2))

    load_half(0)
    project_x(0, 0)
    load_half(1)
    gate_pair(0, 0, then_issue=lambda: project_x(0, 1))
    gate_pair(0, 1, then_issue=lambda: project_x(1, 0))
    gate_pair(0, 2, then_issue=lambda: project_x(1, 1))
    gate_pair(0, 3, then_issue=lambda: project_gate(0))
    gate_pair(1, 0, then_issue=lambda: project_gate(1))
    gate_pair(1, 1)
    h_mid = scan_half(0, hcar_ref[...])
    proj = [None, None]

    def issue_out0():
        proj[0] = project_out(0)

    gate_pair(1, 2, then_issue=issue_out0)
    gate_pair(1, 3)
    proj0 = proj[0]
    hcar_ref[...] = scan_half(1, h_mid)
    proj1 = project_out(1)
    finish_half(0, proj0)
    finish_half(1, proj1)
    for k in range(2):
        xs_ref[k, 0:_HIST_ROWS, :] = xs_ref[k, ROW_TILE:ROW_TILE + _HIST_ROWS, :]

    @pl.when(t == nt - 1)
    def _emit_state():
        for k in range(2):
            conv_ref[:, :, k * half:(k + 1) * half] = (
                xs_ref[k, 0:_HIST_ROWS, :].reshape(CONV_W - 1, ODD_BATCH, half))
        lru_ref[...] = hcar_ref[...]


def _odd_layer(x3, hist_t, h0, norm_g, w_in, conv_w, conv_b, w_ra, b_ra, w_ri, b_ri, lam, w_out, g_fin):
    batch, seq, _ = x3.shape
    nt = seq // ODD_STEPS
    nb = batch // ODD_BATCH
    n_hist = CONV_W - 1
    x_spec = pl.BlockSpec((ODD_BATCH, ODD_STEPS, D_MODEL), lambda b, t: (b, t, 0))
    hist_spec = pl.BlockSpec((n_hist, ODD_BATCH, C_WIDTH), lambda b, t: (0, b, 0))
    h_spec = pl.BlockSpec((ODD_BATCH, C_WIDTH), lambda b, t: (b, 0))
    consts = (norm_g, w_in, conv_w, conv_b, w_ra, b_ra, w_ri, b_ri, lam, w_out, g_fin)
    return pl.pallas_call(
        functools.partial(_odd_kernel, nt=nt),
        grid=(nb, nt),
        in_specs=[x_spec, hist_spec, h_spec] + [_const_spec(c.shape) for c in consts],
        out_specs=[x_spec, hist_spec, h_spec],
        out_shape=[jax.ShapeDtypeStruct((batch, seq, D_MODEL), F32),
                   jax.ShapeDtypeStruct((n_hist, batch, C_WIDTH), F32),
                   jax.ShapeDtypeStruct((batch, C_WIDTH), F32)],
        scratch_shapes=[pltpu.VMEM((2, _HIST_ROWS + ROW_TILE, C_WIDTH // 2), F32),
                        pltpu.VMEM((ROW_TILE, C_WIDTH), F32),
                        pltpu.VMEM((ROW_TILE, C_WIDTH), F32),
                        pltpu.VMEM((ROW_TILE, C_WIDTH), F32),
                        pltpu.VMEM((ROW_TILE, C_WIDTH), BF16),
                        pltpu.VMEM((ODD_BATCH, C_WIDTH), F32)],
        compiler_params=pltpu.CompilerParams(dimension_semantics=("arbitrary", "arbitrary"),
                                             vmem_limit_bytes=VMEM_LIMIT),
        name="odd_layer",
    )(x3, hist_t, h0, *consts)


def _bias_table(masked):
    slopes = 2.0 ** (-8.0 * jnp.arange(1, A_HEADS + 1, dtype=F32) / A_HEADS)
    dist = jnp.abs(WIN_ROWS + jnp.arange(CHUNK)[None, :] - jnp.arange(WIN_KEYS)[:, None]).astype(F32)
    alibi = slopes.reshape(A_KV_HEADS, 1, A_GROUP, 1) * dist[None, :, None, :]
    alibi = alibi.reshape(1, A_KV_HEADS, WIN_KEYS, A_GROUP * CHUNK)
    if not masked:
        return alibi
    first_chunk = jnp.arange(WIN_ROWS // CHUNK + 1)[:, None]
    key_pos = (first_chunk - WIN_ROWS // CHUNK) * CHUNK + jnp.arange(WIN_KEYS)[None, :]
    mask = jnp.where(key_pos < 0, F32(1e30), F32(0.0))
    return alibi + mask[:, None, :, None]


def _pad_gate_blocks(w):
    pad = MXU_DIM - C_BLOCK
    even = jnp.pad(w[0::2], ((0, 0), (0, pad), (0, pad)))
    odd = jnp.pad(w[1::2], ((0, 0), (pad, 0), (pad, 0)))
    return jnp.stack([even, odd], axis=1).reshape(C_BLOCKS, MXU_DIM, MXU_DIM).astype(BF16)


def kernel(x_prompt, x_sample, cache_swa_k, cache_swa_v, state_gla, cache_conv, state_lru, norm_even, w_in_even, w_gate_lr, b_gate_lr, sinks, gla_norm, w_out_even, norm_odd, w_in_odd, conv_w, conv_b, w_rg_a, b_rg_a, w_rg_i, b_rg_i, lru_lambda, w_out_odd, norm_final):
    batch, seq, _ = x_prompt.shape
    dbatch, dseq, _ = x_sample.shape
    row = lambda v: v.reshape(1, -1)

    lr0 = _OFF_GATE
    w_e = w_in_even[0]
    w_main = jnp.concatenate([w_e[:, :lr0].astype(BF16), w_e[:, lr0 + B_LOWRANK:].astype(BF16)], axis=1)
    w_lr1 = jnp.pad(w_e[:, lr0:lr0 + B_LOWRANK], ((0, 0), (0, LANES - B_LOWRANK))).astype(BF16)
    w_lr2 = jnp.pad(w_gate_lr[0], ((0, LANES - B_LOWRANK), (0, 0))).astype(BF16)
    w_out_e = w_out_even[0].astype(BF16)
    even_consts = (row(norm_even[0]), w_main, w_lr1, w_lr2, row(b_gate_lr[0]))
    sink_rows = jnp.repeat(sinks[0].reshape(A_KV_HEADS, 1, A_GROUP), CHUNK, axis=2)

    def even_layer(x, past):
        b, tl, _ = x.shape
        x2 = x.reshape(b * tl, D_MODEL)
        q, k, v, qg, kg, kd, bv, glast, sgate = _even_in(x2, *even_consts)
        if past is None:
            kprev, vprev, s0 = k, v, None
        else:
            kprev = jnp.transpose(past[0], (0, 2, 3, 1))
            vprev = jnp.transpose(past[1], (0, 2, 3, 1))
            s0 = past[2]
        out, s_new = _even_mix(q, k, v, kprev, vprev, qg, kg, kd, bv,
                               glast.reshape(-1, 1, B_QK_WIDTH), sgate, x2,
                               _bias_table(masked=past is None), sink_rows,
                               row(gla_norm[0]), w_out_e, s0, batch=b, seq=tl)
        keep = min(tl, WIN_ROWS)
        k4 = k.reshape(b, tl, A_KV_WIDTH)[:, tl - keep:].reshape(b, keep, A_KV_HEADS, A_HEAD_DIM)
        v4 = v.reshape(b, tl, A_KV_WIDTH)[:, tl - keep:].reshape(b, keep, A_KV_HEADS, A_HEAD_DIM)
        return out, k4, v4, s_new

    xp, pk, pv, pg = even_layer(x_prompt, None)
    xs, sk, sv, sg = even_layer(x_sample, (cache_swa_k[0], cache_swa_v[0], state_gla[0]))

    odd_consts = (row(norm_odd[0]), w_in_odd[0].astype(BF16), conv_w[0], row(conv_b[0]),
                  _pad_gate_blocks(0.5 * w_rg_a[0]), row(0.5 * b_rg_a[0]),
                  _pad_gate_blocks(0.5 * w_rg_i[0]), row(0.5 * b_rg_i[0]),
                  row(lru_lambda[0]), w_out_odd[0].astype(BF16), row(norm_final))
    zero_hist = jnp.zeros((CONV_W - 1, batch, C_WIDTH), F32)
    zero_h = jnp.zeros((batch, C_WIDTH), F32)
    yp, pc, plru = _odd_layer(xp.reshape(batch, seq, D_MODEL), zero_hist, zero_h, *odd_consts)
    ys, sc, slru = _odd_layer(xs.reshape(dbatch, dseq, D_MODEL), jnp.swapaxes(cache_conv[0], 0, 1),
                              state_lru[0], *odd_consts)

    return (yp, ys, pk[None], pv[None], pg[None],
            jnp.swapaxes(pc, 0, 1)[None], plru[None],
            sk[None], sv[None], sg[None], jnp.swapaxes(sc, 0, 1)[None], slru[None])
```

```python
import functools

import jax
import jax.numpy as jnp
import numpy as np
from jax import lax
from jax.experimental import pallas as pl
from jax.experimental.pallas import tpu as pltpu

F32 = jnp.float32
BF16 = jnp.bfloat16

D_MODEL = 1024
CHUNK = 64
EPS = 1e-6
A_HEADS = 16
A_KV_HEADS = 4
A_HEAD_DIM = 64
A_GROUP = A_HEADS // A_KV_HEADS
A_WIDTH = A_HEADS * A_HEAD_DIM
A_KV_WIDTH = A_KV_HEADS * A_HEAD_DIM
WIN_ROWS = 128
WIN_KEYS = WIN_ROWS + CHUNK
KV_PAD_WIDTH = A_KV_HEADS * 2 * A_HEAD_DIM
B_HEADS = 4
B_DK = 128
B_DV = 256
B_QK_WIDTH = B_HEADS * B_DK
B_WIDTH = B_HEADS * B_DV
B_LOWRANK = 16
B_GATE_NORM = 16.0
C_WIDTH = 1536
C_BLOCKS = 8
C_BLOCK = C_WIDTH // C_BLOCKS
CONV_W = 4
LRU_C = 8.0
MIX_EVEN = A_WIDTH + B_WIDTH

LANES = 128
SUBLANES = 8
MXU_DIM = 256
ROW_TILE = 512
VMEM_LIMIT = 48 * 1024 * 1024

_OFF_Q = 0
_OFF_K = _OFF_Q + A_WIDTH
_OFF_V = _OFF_K + A_KV_WIDTH
_OFF_BQ = _OFF_V + A_KV_WIDTH
_OFF_BK = _OFF_BQ + B_QK_WIDTH
_OFF_BV = _OFF_BK + B_QK_WIDTH
_OFF_GATE = _OFF_BV + B_WIDTH
_MAIN_WIDTH = _OFF_GATE + MIX_EVEN


def _const_spec(shape):
    nd = len(shape)
    return pl.BlockSpec(shape, lambda *_: (0,) * nd, pipeline_mode=pl.Buffered(1))


def _rms_scale(x):
    return x * lax.rsqrt(jnp.mean(x * x, axis=-1, keepdims=True) + EPS)


def _silu(x):
    half = 0.5 * x
    return half * jnp.tanh(half) + half


def _dot(a, b):
    return jnp.dot(a, b, preferred_element_type=F32)


def _dot_nt(a, b):
    return lax.dot_general(a, b, (((1,), (1,)), ((), ())), preferred_element_type=F32)


def _dot_tn(a, b):
    return lax.dot_general(a, b, (((0,), (0,)), ((), ())), preferred_element_type=F32)


def _even_in_kernel(x_ref, g_ref, w_ref, wlr1_ref, wlr2_ref, blr_ref,
                    q_ref, k_ref, v_ref, kb_ref, vb_ref, qg_ref, kg_ref, kd_ref, bv_ref, glast_ref,
                    sgate_ref):
    h = (_rms_scale(x_ref[...]) * g_ref[...]).astype(BF16)

    def proj(off, width):
        return _dot(h, w_ref[:, off:off + width])

    q_ref[...] = (proj(_OFF_Q, A_WIDTH) * (A_HEAD_DIM ** -0.5)).astype(BF16)
    kf = proj(_OFF_K, A_KV_WIDTH)
    vf = proj(_OFF_V, A_KV_WIDTH)
    k_ref[...] = kf
    v_ref[...] = vf
    n_rows = kf.shape[0]
    zeros = jnp.zeros((n_rows, A_HEAD_DIM), F32)
    one_lane = jnp.where(lax.broadcasted_iota(jnp.int32, (n_rows, A_HEAD_DIM), 1) == 0, 1.0, 0.0)
    heads = [slice(kh * A_HEAD_DIM, (kh + 1) * A_HEAD_DIM) for kh in range(A_KV_HEADS)]
    kb_ref[...] = jnp.concatenate([piece for s in heads for piece in (kf[:, s], zeros)], axis=1).astype(BF16)
    vb_ref[...] = jnp.concatenate([piece for s in heads for piece in (vf[:, s], one_lane)], axis=1).astype(BF16)
    bv_ref[...] = proj(_OFF_BV, B_WIDTH).astype(BF16)
    sgate_ref[...] = _silu(proj(_OFF_GATE, MIX_EVEN)).astype(BF16)

    low = _dot(h, wlr1_ref[...])
    pre = _dot(low.astype(BF16), wlr2_ref[...]) + blr_ref[...]
    glog = (jnp.minimum(pre, 0.0) - jnp.log1p(jnp.exp(-jnp.abs(pre)))) * (1.0 / B_GATE_NORM)
    row_id = lax.broadcasted_iota(jnp.int32, (SUBLANES, B_QK_WIDTH), 0)
    groups = []
    for r in range(ROW_TILE // SUBLANES):
        g8 = glog[r * SUBLANES:(r + 1) * SUBLANES, :]
        for sh in (1, 2, 4):
            g8 = g8 + jnp.where(row_id >= sh, pltpu.roll(g8, sh, axis=0), 0.0)
        if r % (CHUNK // SUBLANES) != 0:
            g8 = g8 + groups[-1][SUBLANES - 1:SUBLANES, :]
        groups.append(g8)
    gcum = jnp.concatenate(groups, axis=0)

    bk = proj(_OFF_BK, B_QK_WIDTH)
    qg_ref[...] = (proj(_OFF_BQ, B_QK_WIDTH) * (B_DK ** -0.5) * jnp.exp(gcum)).astype(BF16)
    kg_ref[...] = (bk * jnp.exp(-gcum)).astype(BF16)
    for c in range(ROW_TILE // CHUNK):
        rows = slice(c * CHUNK, (c + 1) * CHUNK)
        glast = gcum[(c + 1) * CHUNK - 1:(c + 1) * CHUNK, :]
        glast_ref[c:c + 1, :] = glast
        kd_ref[rows, :] = (bk[rows] * jnp.exp(glast - gcum[rows])).astype(BF16)


def _even_in(x2, norm_g, w_main, w_lr1, w_lr2, b_lr):
    rows = x2.shape[0]
    nsteps = rows // ROW_TILE
    row_spec = lambda w: pl.BlockSpec((ROW_TILE, w), lambda i: (i, 0))
    chunks = ROW_TILE // CHUNK
    out_widths = (A_WIDTH, A_KV_WIDTH, A_KV_WIDTH, KV_PAD_WIDTH, KV_PAD_WIDTH, B_QK_WIDTH, B_QK_WIDTH,
                  B_QK_WIDTH, B_WIDTH, B_QK_WIDTH, MIX_EVEN)
    out_dtypes = (BF16, F32, F32, BF16, BF16, BF16, BF16, BF16, BF16, F32, BF16)
    out_rows = [ROW_TILE] * 9 + [chunks, ROW_TILE]
    return pl.pallas_call(
        _even_in_kernel,
        grid=(nsteps,),
        in_specs=[row_spec(D_MODEL), _const_spec(norm_g.shape), _const_spec(w_main.shape),
                  _const_spec(w_lr1.shape), _const_spec(w_lr2.shape), _const_spec(b_lr.shape)],
        out_specs=[pl.BlockSpec((r, w), lambda i: (i, 0)) for r, w in zip(out_rows, out_widths)],
        out_shape=[jax.ShapeDtypeStruct((nsteps * r, w), d)
                   for r, w, d in zip(out_rows, out_widths, out_dtypes)],
        compiler_params=pltpu.CompilerParams(dimension_semantics=("arbitrary",),
                                             vmem_limit_bytes=VMEM_LIMIT),
        name="even_in",
    )(x2, norm_g, w_main, w_lr1, w_lr2, b_lr)


def _even_mix_kernel(*refs, bb, tb, nt, has_past):
    if has_past:
        (q_ref, kc_ref, vc_ref, kp_ref, vp_ref, qg_ref, kg_ref, kd_ref, bv_ref, glast_ref,
         sgate_ref, x_ref, bias_ref, sink_ref, glag_ref, wout_ref, s0_ref,
         out_ref, sfin_ref, kbuf, vbuf, st_ref, y_ref) = refs
    else:
        (q_ref, kc_ref, vc_ref, kp_ref, vp_ref, qg_ref, kg_ref, kd_ref, bv_ref, glast_ref,
         sgate_ref, x_ref, bias_ref, sink_ref, glag_ref, wout_ref,
         out_ref, sfin_ref, kbuf, vbuf, st_ref, y_ref) = refs
        s0_ref = None
    t = pl.program_id(1)
    tc = tb // CHUNK

    one_lane = jnp.where(lax.broadcasted_iota(jnp.int32, (WIN_ROWS, A_HEAD_DIM), 1) == 0,
                         1.0, 0.0).astype(BF16)
    for bi in range(bb):
        for kh in range(A_KV_HEADS):
            slot = slice(kh * 2 * A_HEAD_DIM, (kh + 1) * 2 * A_HEAD_DIM)
            kslot = slice(kh * 2 * A_HEAD_DIM, kh * 2 * A_HEAD_DIM + A_HEAD_DIM)
            if has_past:
                kbuf[bi, kh, 0:WIN_ROWS, :] = kp_ref[bi, kh].T.astype(BF16)
                vbuf[bi, kh, 0:WIN_ROWS, 0:A_HEAD_DIM] = vp_ref[bi, kh].T.astype(BF16)
                vbuf[bi, kh, 0:WIN_ROWS, A_HEAD_DIM:2 * A_HEAD_DIM] = one_lane
            else:
                kbuf[bi, kh, 0:WIN_ROWS, :] = kp_ref[:, kslot]
                vbuf[bi, kh, 0:WIN_ROWS, :] = vp_ref[:, slot]
            kbuf[bi, kh, WIN_ROWS:WIN_ROWS + tb, :] = kc_ref[bi * tb:(bi + 1) * tb, kslot]
            vbuf[bi, kh, WIN_ROWS:WIN_ROWS + tb, :] = vc_ref[bi * tb:(bi + 1) * tb, slot]

    @pl.when(t == 0)
    def _init_state():
        for bi in range(bb):
            for hh in range(B_HEADS):
                if has_past:
                    st_ref[bi, hh] = s0_ref[bi, hh].T
                else:
                    st_ref[bi, hh] = jnp.zeros((B_DV, B_DK), F32)

    tril = (lax.broadcasted_iota(jnp.int32, (CHUNK, CHUNK), 0)
            >= lax.broadcasted_iota(jnp.int32, (CHUNK, CHUNK), 1))
    glag = glag_ref[...]
    tail_rows = 16
    tail_row0 = lax.broadcasted_iota(jnp.int32, (tail_rows, A_GROUP * CHUNK), 0) == 0
    tail_iota = lax.broadcasted_iota(jnp.int32, (tail_rows, 2 * A_HEAD_DIM), 0)
    tail_lane = lax.broadcasted_iota(jnp.int32, (tail_rows, 2 * A_HEAD_DIM), 1)
    v_tail = jnp.where((tail_iota == 0) & (tail_lane == A_HEAD_DIM), 1.0, 0.0).astype(BF16)

    def chunk_body(n, carry):
        bi = n // tc
        c = n % tc
        w0 = pl.multiple_of(c * CHUNK, CHUNK)
        rows = pl.ds(pl.multiple_of(n * CHUNK, CHUNK), CHUNK)

        variant = 0 if has_past else jnp.minimum(t * tc + c, WIN_ROWS // CHUNK)
        def attn_scores(kh):
            q4 = q_ref[rows, kh * A_GROUP * A_HEAD_DIM:(kh + 1) * A_GROUP * A_HEAD_DIM]
            qs = jnp.concatenate([q4[:, g * A_HEAD_DIM:(g + 1) * A_HEAD_DIM] for g in range(A_GROUP)],
                                 axis=0)
            kw = kbuf[bi, kh, pl.ds(w0, WIN_KEYS), :]
            return _dot_nt(kw, qs) - bias_ref[variant, kh]

        def attn_finish(kh, sg):
            cols = slice(kh * A_GROUP * A_HEAD_DIM, (kh + 1) * A_GROUP * A_HEAD_DIM)
            vw = vbuf[bi, kh, pl.ds(w0, WIN_KEYS), :]
            sink = sink_ref[kh]
            m = jnp.maximum(jnp.max(sg, axis=0, keepdims=True), sink)
            p = jnp.exp(sg - m).astype(BF16)
            p_sink = jnp.where(tail_row0, jnp.exp(sink - m), 0.0).astype(BF16)
            oe = _dot_tn(jnp.concatenate([vw, v_tail], axis=0),
                         jnp.concatenate([p, p_sink], axis=0))
            o = jnp.transpose(oe[0:A_HEAD_DIM, :] * (1.0 / oe[A_HEAD_DIM:A_HEAD_DIM + 1, :]))
            ao = jnp.concatenate([o[g * CHUNK:(g + 1) * CHUNK] for g in range(A_GROUP)], axis=1)
            y_ref[rows, cols] = (ao * sgate_ref[rows, cols].astype(F32)).astype(BF16)

        dec = jnp.exp(glast_ref[n])

        def gla_head(hh):
            ks = slice(hh * B_DK, (hh + 1) * B_DK)
            vs = slice(hh * B_DV, (hh + 1) * B_DV)
            qg = qg_ref[rows, ks]
            a = jnp.where(tril, _dot_nt(qg, kg_ref[rows, ks]), 0.0).astype(BF16)
            vh = bv_ref[rows, vs]
            st = st_ref[bi, hh]
            o = _dot(a, vh) + _dot_nt(qg, st.astype(BF16))
            st_ref[bi, hh] = st * dec[:, ks] + _dot_tn(vh, kd_ref[rows, ks])
            bo = _rms_scale(o) * glag
            cols = slice(A_WIDTH + hh * B_DV, A_WIDTH + (hh + 1) * B_DV)
            y_ref[rows, cols] = (bo * sgate_ref[rows, cols].astype(F32)).astype(BF16)

        scores = attn_scores(0)
        for kh in range(A_KV_HEADS):
            nxt = attn_scores(kh + 1) if kh + 1 < A_KV_HEADS else None
            gla_head(kh)
            attn_finish(kh, scores)
            scores = nxt
        return carry

    lax.fori_loop(0, bb * tc, chunk_body, 0, unroll=True)

    out_ref[...] = x_ref[...] + _dot(y_ref[...], wout_ref[...])

    @pl.when(t == nt - 1)
    def _emit_state():
        for bi in range(bb):
            for hh in range(B_HEADS):
                sfin_ref[bi, hh] = st_ref[bi, hh].T


def _even_mix(q, k, v, kprev, vprev, qg, kg, kd, bv, glast, sgate, x2, bias, sink_rows, gla_g, w_out, s0,
              *, batch, seq):
    has_past = s0 is not None
    tile = ROW_TILE // 2 if has_past else ROW_TILE
    tb = min(seq, tile)
    bb = tile // tb
    nt = seq // tb
    nb = batch // bb
    assert bb == 1 or nt == 1
    row_spec = lambda w: pl.BlockSpec((tile, w), lambda b, t: (b * nt + t, 0))
    if has_past:
        prev_spec = pl.BlockSpec((bb, A_KV_HEADS, A_HEAD_DIM, WIN_ROWS), lambda b, t: (b, 0, 0, 0))
    else:
        assert bb == 1
        per_b = seq // WIN_ROWS
        step = tb // WIN_ROWS
        prev_spec = pl.BlockSpec((WIN_ROWS, KV_PAD_WIDTH),
                                 lambda b, t: (b * per_b + jnp.maximum(t * step - 1, 0), 0))
    state_spec = pl.BlockSpec((bb, B_HEADS, B_DK, B_DV), lambda b, t: (b, 0, 0, 0))
    glast_spec = pl.BlockSpec((tile // CHUNK, 1, B_QK_WIDTH), lambda b, t: (b * nt + t, 0, 0))
    in_specs = [row_spec(A_WIDTH), row_spec(KV_PAD_WIDTH), row_spec(KV_PAD_WIDTH), prev_spec, prev_spec,
                row_spec(B_QK_WIDTH), row_spec(B_QK_WIDTH), row_spec(B_QK_WIDTH), row_spec(B_WIDTH),
                glast_spec, row_spec(MIX_EVEN), row_spec(D_MODEL),
                _const_spec(bias.shape), _const_spec(sink_rows.shape), _const_spec(gla_g.shape),
                _const_spec(w_out.shape)]
    args = [q, k, v, kprev, vprev, qg, kg, kd, bv, glast, sgate, x2, bias, sink_rows, gla_g, w_out]
    if has_past:
        in_specs.append(state_spec)
        args.append(s0)
    return pl.pallas_call(
        functools.partial(_even_mix_kernel, bb=bb, tb=tb, nt=nt, has_past=has_past),
        grid=(nb, nt),
        in_specs=in_specs,
        out_specs=[row_spec(D_MODEL), state_spec],
        out_shape=[jax.ShapeDtypeStruct((batch * seq, D_MODEL), F32),
                   jax.ShapeDtypeStruct((batch, B_HEADS, B_DK, B_DV), F32)],
        scratch_shapes=[pltpu.VMEM((bb, A_KV_HEADS, WIN_ROWS + tb, A_HEAD_DIM), BF16),
                        pltpu.VMEM((bb, A_KV_HEADS, WIN_ROWS + tb, 2 * A_HEAD_DIM), BF16),
                        pltpu.VMEM((bb, B_HEADS, B_DV, B_DK), F32),
                        pltpu.VMEM((tile, MIX_EVEN), BF16)],
        compiler_params=pltpu.CompilerParams(dimension_semantics=("arbitrary", "arbitrary"),
                                             vmem_limit_bytes=VMEM_LIMIT),
        name="even_mix",
    )(*args)


_PAIR = 2 * C_BLOCK
ODD_BATCH = SUBLANES
ODD_STEPS = ROW_TILE // ODD_BATCH
_HIST_ROWS = (CONV_W - 1) * ODD_BATCH


def _odd_kernel(x_ref, hist_ref, h0_ref, g_ref, win_ref, cw_ref, cb_ref, wra_ref, bra_ref,
                wri_ref, bri_ref, lam_ref, wout_ref, gfin_ref,
                out_ref, conv_ref, lru_ref,
                xs_ref, gate_ref, a_ref, b_ref, y_ref, hcar_ref, *, nt):
    t = pl.program_id(1)
    half = C_WIDTH // 2

    @pl.when(t == 0)
    def _init_state():
        hist = hist_ref[...].reshape(_HIST_ROWS, C_WIDTH)
        for k in range(2):
            xs_ref[k, 0:_HIST_ROWS, :] = hist[:, k * half:(k + 1) * half]
        hcar_ref[...] = h0_ref[...]

    steps_h = ODD_STEPS // 2
    rows_h = ROW_TILE // 2
    xh, hh = [None, None], [None, None]

    def load_half(hf):
        x3 = x_ref[:, hf * steps_h:(hf + 1) * steps_h, :]
        xh[hf] = jnp.transpose(x3, (1, 0, 2)).reshape(rows_h, D_MODEL)
        hh[hf] = (_rms_scale(xh[hf]) * g_ref[...]).astype(BF16)

    def project_x(hf, k):
        r0 = _HIST_ROWS + hf * rows_h
        xs_ref[k, r0:r0 + rows_h, :] = _dot(hh[hf], win_ref[:, k * half:(k + 1) * half])

    def project_gate(hf):
        gate_ref[hf * rows_h:(hf + 1) * rows_h, :] = _dot(hh[hf], win_ref[:, C_WIDTH:2 * C_WIDTH])

    soft_lam = lam_ref[...]
    soft_lam = jnp.maximum(-soft_lam, 0.0) + jnp.log1p(jnp.exp(-jnp.abs(soft_lam)))

    def gate_pair(hf, m, then_issue=None):
        c0 = m * _PAIR
        k, ck = divmod(c0, half)
        u = cb_ref[:, c0:c0 + _PAIR]
        for j in range(CONV_W):
            r = j * ODD_BATCH + hf * rows_h
            u = u + xs_ref[k, r:r + rows_h, ck:ck + _PAIR] * cw_ref[j:j + 1, c0:c0 + _PAIR]
        ub = u.astype(BF16)
        lo, hi = ub[:, 0:MXU_DIM], ub[:, LANES:LANES + MXU_DIM]

        def gate_pre(w_ref):
            e = _dot(lo, w_ref[2 * m])
            o = _dot(hi, w_ref[2 * m + 1])
            return jnp.concatenate([e[:, 0:LANES], e[:, LANES:] + o[:, 0:LANES], o[:, LANES:]], axis=1)

        z_r = gate_pre(wra_ref)
        z_i = gate_pre(wri_ref)
        if then_issue is not None:
            then_issue()
        t_r = jnp.tanh(z_r + bra_ref[:, c0:c0 + _PAIR])
        t_i = jnp.tanh(z_i + bri_ref[:, c0:c0 + _PAIR])
        c4 = (0.5 * LRU_C) * soft_lam[:, c0:c0 + _PAIR]
        neg_log_a = c4 * t_r + c4
        a = jnp.exp(-neg_log_a)
        rows = slice(hf * rows_h, (hf + 1) * rows_h)
        a_ref[rows, c0:c0 + _PAIR] = a
        x1 = jnp.tanh(neg_log_a) * (a * a + 1.0)
        root = jnp.where(x1 > 0.0, x1 * lax.rsqrt(x1), 0.0)
        hu = 0.5 * u
        b_ref[rows, c0:c0 + _PAIR] = root * (hu * t_i + hu)

    def scan_half(hf, hprev):
        pair = 2 * ODD_BATCH
        for i in range(rows_h // pair):
            r0 = hf * rows_h + i * pair
            lo8, hi8, both = slice(r0, r0 + ODD_BATCH), slice(r0 + ODD_BATCH, r0 + pair), slice(r0, r0 + pair)
            h1 = a_ref[lo8, :] * hprev + b_ref[lo8, :]
            hprev = a_ref[hi8, :] * h1 + b_ref[hi8, :]
            y_ref[both, :] = (jnp.concatenate([h1, hprev], axis=0) * _silu(gate_ref[both, :])).astype(BF16)
        return hprev

    def project_out(hf):
        return _dot(y_ref[hf * rows_h:(hf + 1) * rows_h, :], wout_ref[...])

    def finish_half(hf, proj):
        out = _rms_scale(xh[hf] + proj) * gfin_ref[...]
        out_ref[:, hf * steps_h:(hf + 1) * steps_h, :] = jnp.transpose(
            out.reshape(steps_h, ODD_BATCH, D_MODEL), (1, 0, 2))

    load_half(0)
    project_x(0, 0)
    load_half(1)
    gate_pair(0, 0, then_issue=lambda: project_x(0, 1))
    gate_pair(0, 1, then_issue=lambda: project_x(1, 0))
    gate_pair(0, 2, then_issue=lambda: project_x(1, 1))
    gate_pair(0, 3, then_issue=lambda: project_gate(0))
    gate_pair(1, 0, then_issue=lambda: project_gate(1))
    gate_pair(1, 1)
    h_mid = scan_half(0, hcar_ref[...])
    proj = [None, None]

    def issue_out0():
        proj[0] = project_out(0)

    gate_pair(1, 2, then_issue=issue_out0)
    gate_pair(1, 3)
    proj0 = proj[0]
    hcar_ref[...] = scan_half(1, h_mid)
    proj1 = project_out(1)
    finish_half(0, proj0)
    finish_half(1, proj1)
    for k in range(2):
        xs_ref[k, 0:_HIST_ROWS, :] = xs_ref[k, ROW_TILE:ROW_TILE + _HIST_ROWS, :]

    @pl.when(t == nt - 1)
    def _emit_state():
        for k in range(2):
            conv_ref[:, :, k * half:(k + 1) * half] = (
                xs_ref[k, 0:_HIST_ROWS, :].reshape(CONV_W - 1, ODD_BATCH, half))
        lru_ref[...] = hcar_ref[...]


def _odd_layer(x3, hist_t, h0, norm_g, w_in, conv_w, conv_b, w_ra, b_ra, w_ri, b_ri, lam, w_out, g_fin):
    batch, seq, _ = x3.shape
    nt = seq // ODD_STEPS
    nb = batch // ODD_BATCH
    n_hist = CONV_W - 1
    x_spec = pl.BlockSpec((ODD_BATCH, ODD_STEPS, D_MODEL), lambda b, t: (b, t, 0))
    hist_spec = pl.BlockSpec((n_hist, ODD_BATCH, C_WIDTH), lambda b, t: (0, b, 0))
    h_spec = pl.BlockSpec((ODD_BATCH, C_WIDTH), lambda b, t: (b, 0))
    consts = (norm_g, w_in, conv_w, conv_b, w_ra, b_ra, w_ri, b_ri, lam, w_out, g_fin)
    return pl.pallas_call(
        functools.partial(_odd_kernel, nt=nt),
        grid=(nb, nt),
        in_specs=[x_spec, hist_spec, h_spec] + [_const_spec(c.shape) for c in consts],
        out_specs=[x_spec, hist_spec, h_spec],
        out_shape=[jax.ShapeDtypeStruct((batch, seq, D_MODEL), F32),
                   jax.ShapeDtypeStruct((n_hist, batch, C_WIDTH), F32),
                   jax.ShapeDtypeStruct((batch, C_WIDTH), F32)],
        scratch_shapes=[pltpu.VMEM((2, _HIST_ROWS + ROW_TILE, C_WIDTH // 2), F32),
                        pltpu.VMEM((ROW_TILE, C_WIDTH), F32),
                        pltpu.VMEM((ROW_TILE, C_WIDTH), F32),
                        pltpu.VMEM((ROW_TILE, C_WIDTH), F32),
                        pltpu.VMEM((ROW_TILE, C_WIDTH), BF16),
                        pltpu.VMEM((ODD_BATCH, C_WIDTH), F32)],
        compiler_params=pltpu.CompilerParams(dimension_semantics=("arbitrary", "arbitrary"),
                                             vmem_limit_bytes=VMEM_LIMIT),
        name="odd_layer",
    )(x3, hist_t, h0, *consts)


def _bias_table(masked):
    slopes = 2.0 ** (-8.0 * jnp.arange(1, A_HEADS + 1, dtype=F32) / A_HEADS)
    dist = jnp.abs(WIN_ROWS + jnp.arange(CHUNK)[None, :] - jnp.arange(WIN_KEYS)[:, None]).astype(F32)
    alibi = slopes.reshape(A_KV_HEADS, 1, A_GROUP, 1) * dist[None, :, None, :]
    alibi = alibi.reshape(1, A_KV_HEADS, WIN_KEYS, A_GROUP * CHUNK)
    if not masked:
        return alibi
    first_chunk = jnp.arange(WIN_ROWS // CHUNK + 1)[:, None]
    key_pos = (first_chunk - WIN_ROWS // CHUNK) * CHUNK + jnp.arange(WIN_KEYS)[None, :]
    mask = jnp.where(key_pos < 0, F32(1e30), F32(0.0))
    return alibi + mask[:, None, :, None]


def _pad_gate_blocks(w):
    pad = MXU_DIM - C_BLOCK
    even = jnp.pad(w[0::2], ((0, 0), (0, pad), (0, pad)))
    odd = jnp.pad(w[1::2], ((0, 0), (pad, 0), (pad, 0)))
    return jnp.stack([even, odd], axis=1).reshape(C_BLOCKS, MXU_DIM, MXU_DIM).astype(BF16)


def kernel(x_prompt, x_sample, cache_swa_k, cache_swa_v, state_gla, cache_conv, state_lru, norm_even, w_in_even, w_gate_lr, b_gate_lr, sinks, gla_norm, w_out_even, norm_odd, w_in_odd, conv_w, conv_b, w_rg_a, b_rg_a, w_rg_i, b_rg_i, lru_lambda, w_out_odd, norm_final):
    batch, seq, _ = x_prompt.shape
    dbatch, dseq, _ = x_sample.shape
    row = lambda v: v.reshape(1, -1)

    lr0 = _OFF_GATE
    w_e = w_in_even[0]
    w_main = jnp.concatenate([w_e[:, :lr0].astype(BF16), w_e[:, lr0 + B_LOWRANK:].astype(BF16)], axis=1)
    w_lr1 = jnp.pad(w_e[:, lr0:lr0 + B_LOWRANK], ((0, 0), (0, LANES - B_LOWRANK))).astype(BF16)
    w_lr2 = jnp.pad(w_gate_lr[0], ((0, LANES - B_LOWRANK), (0, 0))).astype(BF16)
    w_out_e = w_out_even[0].astype(BF16)
    even_consts = (row(norm_even[0]), w_main, w_lr1, w_lr2, row(b_gate_lr[0]))
    sink_rows = jnp.repeat(sinks[0].reshape(A_KV_HEADS, 1, A_GROUP), CHUNK, axis=2)

    def even_layer(x, past):
        b, tl, _ = x.shape
        x2 = x.reshape(b * tl, D_MODEL)
        q, k, v, kb, vb, qg, kg, kd, bv, glast, sgate = _even_in(x2, *even_consts)
        if past is None:
            kprev, vprev, s0 = kb, vb, None
        else:
            kprev = jnp.transpose(past[0], (0, 2, 3, 1))
            vprev = jnp.transpose(past[1], (0, 2, 3, 1))
            s0 = past[2]
        out, s_new = _even_mix(q, kb, vb, kprev, vprev, qg, kg, kd, bv,
                               glast.reshape(-1, 1, B_QK_WIDTH), sgate, x2,
                               _bias_table(masked=past is None), sink_rows,
                               row(gla_norm[0]), w_out_e, s0, batch=b, seq=tl)
        keep = min(tl, WIN_ROWS)
        k4 = k.reshape(b, tl, A_KV_WIDTH)[:, tl - keep:].reshape(b, keep, A_KV_HEADS, A_HEAD_DIM)
        v4 = v.reshape(b, tl, A_KV_WIDTH)[:, tl - keep:].reshape(b, keep, A_KV_HEADS, A_HEAD_DIM)
        return out, k4, v4, s_new

    xp, pk, pv, pg = even_layer(x_prompt, None)
    xs, sk, sv, sg = even_layer(x_sample, (cache_swa_k[0], cache_swa_v[0], state_gla[0]))

    odd_consts = (row(norm_odd[0]), w_in_odd[0].astype(BF16), conv_w[0], row(conv_b[0]),
                  _pad_gate_blocks(0.5 * w_rg_a[0]), row(0.5 * b_rg_a[0]),
                  _pad_gate_blocks(0.5 * w_rg_i[0]), row(0.5 * b_rg_i[0]),
                  row(lru_lambda[0]), w_out_odd[0].astype(BF16), row(norm_final))
    zero_hist = jnp.zeros((CONV_W - 1, batch, C_WIDTH), F32)
    zero_h = jnp.zeros((batch, C_WIDTH), F32)
    yp, pc, plru = _odd_layer(xp.reshape(batch, seq, D_MODEL), zero_hist, zero_h, *odd_consts)
    ys, sc, slru = _odd_layer(xs.reshape(dbatch, dseq, D_MODEL), jnp.swapaxes(cache_conv[0], 0, 1),
                              state_lru[0], *odd_consts)

    return (yp, ys, pk[None], pv[None], pg[None],
            jnp.swapaxes(pc, 0, 1)[None], plru[None],
            sk[None], sv[None], sg[None], jnp.swapaxes(sc, 0, 1)[None], slru[None])
```

```python
import functools

import jax
import jax.numpy as jnp
from jax import lax
from jax.experimental import pallas as pl
from jax.experimental.pallas import tpu as pltpu

F32 = jnp.float32
BF16 = jnp.bfloat16

D_MODEL = 1024
CHUNK = 64
EPS = 1e-6
A_HEADS = 16
A_KV_HEADS = 4
A_HEAD_DIM = 64
A_GROUP = A_HEADS // A_KV_HEADS
A_WIDTH = A_HEADS * A_HEAD_DIM
A_KV_WIDTH = A_KV_HEADS * A_HEAD_DIM
WIN_ROWS = 128
WIN_KEYS = WIN_ROWS + CHUNK
KV_PAD_WIDTH = A_KV_HEADS * 2 * A_HEAD_DIM
B_HEADS = 4
B_DK = 128
B_DV = 256
B_QK_WIDTH = B_HEADS * B_DK
B_WIDTH = B_HEADS * B_DV
B_LOWRANK = 16
B_GATE_NORM = 16.0
C_WIDTH = 1536
C_BLOCKS = 8
C_BLOCK = C_WIDTH // C_BLOCKS
CONV_W = 4
LRU_C = 8.0
MIX_EVEN = A_WIDTH + B_WIDTH

LANES = 128
SUBLANES = 8
MXU_DIM = 256
ROW_TILE = 512
VMEM_LIMIT = 48 * 1024 * 1024

_OFF_Q = 0
_OFF_K = _OFF_Q + A_WIDTH
_OFF_V = _OFF_K + A_KV_WIDTH
_OFF_BQ = _OFF_V + A_KV_WIDTH
_OFF_BK = _OFF_BQ + B_QK_WIDTH
_OFF_BV = _OFF_BK + B_QK_WIDTH
_OFF_GATE = _OFF_BV + B_WIDTH
_MAIN_WIDTH = _OFF_GATE + MIX_EVEN


def _const_spec(shape):
    nd = len(shape)
    return pl.BlockSpec(shape, lambda *_: (0,) * nd, pipeline_mode=pl.Buffered(1))


def _rms_scale(x):
    return x * lax.rsqrt(jnp.mean(x * x, axis=-1, keepdims=True) + EPS)


def _silu(x):
    half = 0.5 * x
    return half * jnp.tanh(half) + half


def _dot(a, b):
    return jnp.dot(a, b, preferred_element_type=F32)


def _dot_nt(a, b):
    return lax.dot_general(a, b, (((1,), (1,)), ((), ())), preferred_element_type=F32)


def _dot_tn(a, b):
    return lax.dot_general(a, b, (((0,), (0,)), ((), ())), preferred_element_type=F32)


def _even_in_kernel(x_ref, g_ref, w_ref, wlr1_ref, wlr2_ref, blr_ref,
                    q_ref, k_ref, v_ref, kb_ref, vb_ref, qg_ref, kg_ref, kd_ref, bv_ref, glast_ref,
                    sgate_ref):
    h = (_rms_scale(x_ref[...]) * g_ref[...]).astype(BF16)

    def proj(off, width):
        return _dot(h, w_ref[:, off:off + width])

    q_ref[...] = (proj(_OFF_Q, A_WIDTH) * (A_HEAD_DIM ** -0.5)).astype(BF16)
    kf = proj(_OFF_K, A_KV_WIDTH)
    vf = proj(_OFF_V, A_KV_WIDTH)
    k_ref[...] = kf
    v_ref[...] = vf
    n_rows = kf.shape[0]
    zeros = jnp.zeros((n_rows, A_HEAD_DIM), F32)
    one_lane = jnp.where(lax.broadcasted_iota(jnp.int32, (n_rows, A_HEAD_DIM), 1) == 0, 1.0, 0.0)
    heads = [slice(kh * A_HEAD_DIM, (kh + 1) * A_HEAD_DIM) for kh in range(A_KV_HEADS)]
    kb_ref[...] = jnp.concatenate([piece for s in heads for piece in (kf[:, s], zeros)], axis=1).astype(BF16)
    vb_ref[...] = jnp.concatenate([piece for s in heads for piece in (vf[:, s], one_lane)], axis=1).astype(BF16)
    bv_ref[...] = proj(_OFF_BV, B_WIDTH).astype(BF16)
    sgate_ref[...] = _silu(proj(_OFF_GATE, MIX_EVEN)).astype(BF16)

    low = _dot(h, wlr1_ref[...])
    pre = _dot(low.astype(BF16), wlr2_ref[...]) + blr_ref[...]
    glog = (jnp.minimum(pre, 0.0) - jnp.log1p(jnp.exp(-jnp.abs(pre)))) * (1.0 / B_GATE_NORM)
    row_id = lax.broadcasted_iota(jnp.int32, (SUBLANES, B_QK_WIDTH), 0)
    groups = []
    for r in range(ROW_TILE // SUBLANES):
        g8 = glog[r * SUBLANES:(r + 1) * SUBLANES, :]
        for sh in (1, 2, 4):
            g8 = g8 + jnp.where(row_id >= sh, pltpu.roll(g8, sh, axis=0), 0.0)
        if r % (CHUNK // SUBLANES) != 0:
            g8 = g8 + groups[-1][SUBLANES - 1:SUBLANES, :]
        groups.append(g8)
    gcum = jnp.concatenate(groups, axis=0)

    bk = proj(_OFF_BK, B_QK_WIDTH)
    qg_ref[...] = (proj(_OFF_BQ, B_QK_WIDTH) * (B_DK ** -0.5) * jnp.exp(gcum)).astype(BF16)
    kg_ref[...] = (bk * jnp.exp(-gcum)).astype(BF16)
    for c in range(ROW_TILE // CHUNK):
        rows = slice(c * CHUNK, (c + 1) * CHUNK)
        glast = gcum[(c + 1) * CHUNK - 1:(c + 1) * CHUNK, :]
        glast_ref[c:c + 1, :] = glast
        kd_ref[rows, :] = (bk[rows] * jnp.exp(glast - gcum[rows])).astype(BF16)


def _even_in(x2, norm_g, w_main, w_lr1, w_lr2, b_lr):
    rows = x2.shape[0]
    nsteps = rows // ROW_TILE
    row_spec = lambda w: pl.BlockSpec((ROW_TILE, w), lambda i: (i, 0))
    chunks = ROW_TILE // CHUNK
    out_widths = (A_WIDTH, A_KV_WIDTH, A_KV_WIDTH, KV_PAD_WIDTH, KV_PAD_WIDTH, B_QK_WIDTH, B_QK_WIDTH,
                  B_QK_WIDTH, B_WIDTH, B_QK_WIDTH, MIX_EVEN)
    out_dtypes = (BF16, F32, F32, BF16, BF16, BF16, BF16, BF16, BF16, F32, BF16)
    out_rows = [ROW_TILE] * 9 + [chunks, ROW_TILE]
    return pl.pallas_call(
        _even_in_kernel,
        grid=(nsteps,),
        in_specs=[row_spec(D_MODEL), _const_spec(norm_g.shape), _const_spec(w_main.shape),
                  _const_spec(w_lr1.shape), _const_spec(w_lr2.shape), _const_spec(b_lr.shape)],
        out_specs=[pl.BlockSpec((r, w), lambda i: (i, 0)) for r, w in zip(out_rows, out_widths)],
        out_shape=[jax.ShapeDtypeStruct((nsteps * r, w), d)
                   for r, w, d in zip(out_rows, out_widths, out_dtypes)],
        compiler_params=pltpu.CompilerParams(dimension_semantics=("arbitrary",),
                                             vmem_limit_bytes=VMEM_LIMIT),
        name="even_in",
    )(x2, norm_g, w_main, w_lr1, w_lr2, b_lr)


def _even_mix_kernel(*refs, bb, tb, nt, has_past):
    if has_past:
        (q_ref, kc_ref, vc_ref, kp_ref, vp_ref, qg_ref, kg_ref, kd_ref, bv_ref, glast_ref,
         sgate_ref, x_ref, bias_ref, sink_ref, glag_ref, wout_ref, s0_ref,
         out_ref, sfin_ref, kbuf, vbuf, st_ref, y_ref) = refs
    else:
        (q_ref, kc_ref, vc_ref, kp_ref, vp_ref, qg_ref, kg_ref, kd_ref, bv_ref, glast_ref,
         sgate_ref, x_ref, bias_ref, sink_ref, glag_ref, wout_ref,
         out_ref, sfin_ref, kbuf, vbuf, st_ref, y_ref) = refs
        s0_ref = None
    t = pl.program_id(1)
    tc = tb // CHUNK

    one_lane = jnp.where(lax.broadcasted_iota(jnp.int32, (WIN_ROWS, A_HEAD_DIM), 1) == 0,
                         1.0, 0.0).astype(BF16)
    for bi in range(bb):
        for kh in range(A_KV_HEADS):
            slot = slice(kh * 2 * A_HEAD_DIM, (kh + 1) * 2 * A_HEAD_DIM)
            kslot = slice(kh * 2 * A_HEAD_DIM, kh * 2 * A_HEAD_DIM + A_HEAD_DIM)
            if has_past:
                kbuf[bi, kh, 0:WIN_ROWS, :] = kp_ref[bi, kh].T.astype(BF16)
                vbuf[bi, kh, 0:WIN_ROWS, 0:A_HEAD_DIM] = vp_ref[bi, kh].T.astype(BF16)
                vbuf[bi, kh, 0:WIN_ROWS, A_HEAD_DIM:2 * A_HEAD_DIM] = one_lane
            else:
                kbuf[bi, kh, 0:WIN_ROWS, :] = kp_ref[:, kslot]
                vbuf[bi, kh, 0:WIN_ROWS, :] = vp_ref[:, slot]
            kbuf[bi, kh, WIN_ROWS:WIN_ROWS + tb, :] = kc_ref[bi * tb:(bi + 1) * tb, kslot]
            vbuf[bi, kh, WIN_ROWS:WIN_ROWS + tb, :] = vc_ref[bi * tb:(bi + 1) * tb, slot]

    @pl.when(t == 0)
    def _init_state():
        for bi in range(bb):
            for hh in range(B_HEADS):
                if has_past:
                    st_ref[bi, hh] = s0_ref[bi, hh].T
                else:
                    st_ref[bi, hh] = jnp.zeros((B_DV, B_DK), F32)

    tril = (lax.broadcasted_iota(jnp.int32, (CHUNK, CHUNK), 0)
            >= lax.broadcasted_iota(jnp.int32, (CHUNK, CHUNK), 1))
    glag = glag_ref[...]
    tail_rows = 16
    tail_row0 = lax.broadcasted_iota(jnp.int32, (tail_rows, A_GROUP * CHUNK), 0) == 0
    tail_iota = lax.broadcasted_iota(jnp.int32, (tail_rows, 2 * A_HEAD_DIM), 0)
    tail_lane = lax.broadcasted_iota(jnp.int32, (tail_rows, 2 * A_HEAD_DIM), 1)
    v_tail = jnp.where((tail_iota == 0) & (tail_lane == A_HEAD_DIM), 1.0, 0.0).astype(BF16)

    def chunk_body(n, carry):
        bi = n // tc
        c = n % tc
        w0 = pl.multiple_of(c * CHUNK, CHUNK)
        rows = pl.ds(pl.multiple_of(n * CHUNK, CHUNK), CHUNK)

        variant = 0 if has_past else jnp.minimum(t * tc + c, WIN_ROWS // CHUNK)
        def attn_scores(kh):
            q4 = q_ref[rows, kh * A_GROUP * A_HEAD_DIM:(kh + 1) * A_GROUP * A_HEAD_DIM]
            qs = jnp.concatenate([q4[:, g * A_HEAD_DIM:(g + 1) * A_HEAD_DIM] for g in range(A_GROUP)],
                                 axis=0)
            kw = kbuf[bi, kh, pl.ds(w0, WIN_KEYS), :]
            return _dot_nt(kw, qs) - bias_ref[variant, kh]

        def attn_finish(kh, sg):
            cols = slice(kh * A_GROUP * A_HEAD_DIM, (kh + 1) * A_GROUP * A_HEAD_DIM)
            vw = vbuf[bi, kh, pl.ds(w0, WIN_KEYS), :]
            sink = sink_ref[kh]
            m = jnp.maximum(jnp.max(sg, axis=0, keepdims=True), sink)
            p = jnp.exp(sg - m).astype(BF16)
            p_sink = jnp.where(tail_row0, jnp.exp(sink - m), 0.0).astype(BF16)
            oe = _dot_tn(jnp.concatenate([vw, v_tail], axis=0),
                         jnp.concatenate([p, p_sink], axis=0))
            o = jnp.transpose(oe[0:A_HEAD_DIM, :] * (1.0 / oe[A_HEAD_DIM:A_HEAD_DIM + 1, :]))
            ao = jnp.concatenate([o[g * CHUNK:(g + 1) * CHUNK] for g in range(A_GROUP)], axis=1)
            y_ref[rows, cols] = (ao * sgate_ref[rows, cols].astype(F32)).astype(BF16)

        dec = jnp.exp(glast_ref[n])

        def gla_head(hh):
            ks = slice(hh * B_DK, (hh + 1) * B_DK)
            vs = slice(hh * B_DV, (hh + 1) * B_DV)
            qg = qg_ref[rows, ks]
            a = jnp.where(tril, _dot_nt(qg, kg_ref[rows, ks]), 0.0).astype(BF16)
            vh = bv_ref[rows, vs]
            st = st_ref[bi, hh]
            o = _dot(a, vh) + _dot_nt(qg, st.astype(BF16))
            st_ref[bi, hh] = st * dec[:, ks] + _dot_tn(vh, kd_ref[rows, ks])
            bo = _rms_scale(o) * glag
            cols = slice(A_WIDTH + hh * B_DV, A_WIDTH + (hh + 1) * B_DV)
            y_ref[rows, cols] = (bo * sgate_ref[rows, cols].astype(F32)).astype(BF16)

        scores = attn_scores(0)
        for kh in range(A_KV_HEADS):
            nxt = attn_scores(kh + 1) if kh + 1 < A_KV_HEADS else None
            gla_head(kh)
            attn_finish(kh, scores)
            scores = nxt
        return carry

    lax.fori_loop(0, bb * tc, chunk_body, 0, unroll=True)

    out_ref[...] = x_ref[...] + _dot(y_ref[...], wout_ref[...])

    @pl.when(t == nt - 1)
    def _emit_state():
        for bi in range(bb):
            for hh in range(B_HEADS):
                sfin_ref[bi, hh] = st_ref[bi, hh].T


def _even_mix(q, k, v, kprev, vprev, qg, kg, kd, bv, glast, sgate, x2, bias, sink_rows, gla_g, w_out, s0,
              *, batch, seq):
    has_past = s0 is not None
    tile = ROW_TILE // 2 if has_past else ROW_TILE
    tb = min(seq, tile)
    bb = tile // tb
    nt = seq // tb
    nb = batch // bb
    assert bb == 1 or nt == 1
    row_spec = lambda w: pl.BlockSpec((tile, w), lambda b, t: (b * nt + t, 0))
    if has_past:
        prev_spec = pl.BlockSpec((bb, A_KV_HEADS, A_HEAD_DIM, WIN_ROWS), lambda b, t: (b, 0, 0, 0))
    else:
        assert bb == 1
        per_b = seq // WIN_ROWS
        step = tb // WIN_ROWS
        prev_spec = pl.BlockSpec((WIN_ROWS, KV_PAD_WIDTH),
                                 lambda b, t: (b * per_b + jnp.maximum(t * step - 1, 0), 0))
    state_spec = pl.BlockSpec((bb, B_HEADS, B_DK, B_DV), lambda b, t: (b, 0, 0, 0))
    glast_spec = pl.BlockSpec((tile // CHUNK, 1, B_QK_WIDTH), lambda b, t: (b * nt + t, 0, 0))
    in_specs = [row_spec(A_WIDTH), row_spec(KV_PAD_WIDTH), row_spec(KV_PAD_WIDTH), prev_spec, prev_spec,
                row_spec(B_QK_WIDTH), row_spec(B_QK_WIDTH), row_spec(B_QK_WIDTH), row_spec(B_WIDTH),
                glast_spec, row_spec(MIX_EVEN), row_spec(D_MODEL),
                _const_spec(bias.shape), _const_spec(sink_rows.shape), _const_spec(gla_g.shape),
                _const_spec(w_out.shape)]
    args = [q, k, v, kprev, vprev, qg, kg, kd, bv, glast, sgate, x2, bias, sink_rows, gla_g, w_out]
    if has_past:
        in_specs.append(state_spec)
        args.append(s0)
    return pl.pallas_call(
        functools.partial(_even_mix_kernel, bb=bb, tb=tb, nt=nt, has_past=has_past),
        grid=(nb, nt),
        in_specs=in_specs,
        out_specs=[row_spec(D_MODEL), state_spec],
        out_shape=[jax.ShapeDtypeStruct((batch * seq, D_MODEL), F32),
                   jax.ShapeDtypeStruct((batch, B_HEADS, B_DK, B_DV), F32)],
        scratch_shapes=[pltpu.VMEM((bb, A_KV_HEADS, WIN_ROWS + tb, A_HEAD_DIM), BF16),
                        pltpu.VMEM((bb, A_KV_HEADS, WIN_ROWS + tb, 2 * A_HEAD_DIM), BF16),
                        pltpu.VMEM((bb, B_HEADS, B_DV, B_DK), F32),
                        pltpu.VMEM((tile, MIX_EVEN), BF16)],
        compiler_params=pltpu.CompilerParams(dimension_semantics=("arbitrary", "arbitrary"),
                                             vmem_limit_bytes=VMEM_LIMIT),
        name="even_mix",
    )(*args)


_PAIR = 2 * C_BLOCK
ODD_BATCH = SUBLANES
ODD_STEPS = ROW_TILE // ODD_BATCH
_HIST_ROWS = (CONV_W - 1) * ODD_BATCH


def _odd_kernel(x_ref, hist_ref, h0_ref, g_ref, win_ref, cw_ref, cb_ref, wra_ref, bra_ref,
                wri_ref, bri_ref, lam_ref, wout_ref, gfin_ref,
                out_ref, conv_ref, lru_ref,
                xs_ref, gate_ref, a_ref, b_ref, y_ref, hcar_ref, *, nt):
    t = pl.program_id(1)
    half = C_WIDTH // 2

    @pl.when(t == 0)
    def _init_state():
        hist = hist_ref[...].reshape(_HIST_ROWS, C_WIDTH)
        for k in range(2):
            xs_ref[k, 0:_HIST_ROWS, :] = hist[:, k * half:(k + 1) * half]
        hcar_ref[...] = h0_ref[...]

    steps_h = ODD_STEPS // 2
    rows_h = ROW_TILE // 2
    xh, hh = [None, None], [None, None]

    def load_half(hf):
        x3 = x_ref[:, hf * steps_h:(hf + 1) * steps_h, :]
        xh[hf] = jnp.transpose(x3, (1, 0, 2)).reshape(rows_h, D_MODEL)
        hh[hf] = (_rms_scale(xh[hf]) * g_ref[...]).astype(BF16)

    def project_x(hf, k):
        r0 = _HIST_ROWS + hf * rows_h
        xs_ref[k, r0:r0 + rows_h, :] = _dot(hh[hf], win_ref[:, k * half:(k + 1) * half])

    def project_gate(hf):
        gate_ref[hf * rows_h:(hf + 1) * rows_h, :] = _dot(hh[hf], win_ref[:, C_WIDTH:2 * C_WIDTH])

    soft_lam = lam_ref[...]
    soft_lam = jnp.maximum(-soft_lam, 0.0) + jnp.log1p(jnp.exp(-jnp.abs(soft_lam)))

    def gate_pair(hf, m, then_issue=None):
        c0 = m * _PAIR
        k, ck = divmod(c0, half)
        u = cb_ref[:, c0:c0 + _PAIR]
        for j in range(CONV_W):
            r = j * ODD_BATCH + hf * rows_h
            u = u + xs_ref[k, r:r + rows_h, ck:ck + _PAIR] * cw_ref[j:j + 1, c0:c0 + _PAIR]
        ub = u.astype(BF16)
        lo, hi = ub[:, 0:MXU_DIM], ub[:, LANES:LANES + MXU_DIM]

        def gate_pre(w_ref):
            e = _dot(lo, w_ref[2 * m])
            o = _dot(hi, w_ref[2 * m + 1])
            return jnp.concatenate([e[:, 0:LANES], e[:, LANES:] + o[:, 0:LANES], o[:, LANES:]], axis=1)

        z_r = gate_pre(wra_ref)
        z_i = gate_pre(wri_ref)
        if then_issue is not None:
            then_issue()
        t_r = jnp.tanh(z_r + bra_ref[:, c0:c0 + _PAIR])
        t_i = jnp.tanh(z_i + bri_ref[:, c0:c0 + _PAIR])
        c4 = (0.5 * LRU_C) * soft_lam[:, c0:c0 + _PAIR]
        neg_log_a = c4 * t_r + c4
        a = jnp.exp(-neg_log_a)
        rows = slice(hf * rows_h, (hf + 1) * rows_h)
        a_ref[rows, c0:c0 + _PAIR] = a
        x1 = jnp.tanh(neg_log_a) * (a * a + 1.0)
        root = jnp.where(x1 > 0.0, x1 * lax.rsqrt(x1), 0.0)
        hu = 0.5 * u
        b_ref[rows, c0:c0 + _PAIR] = root * (hu * t_i + hu)

    def scan_half(hf, hprev):
        pair = 2 * ODD_BATCH
        for i in range(rows_h // pair):
            r0 = hf * rows_h + i * pair
            lo8, hi8, both = slice(r0, r0 + ODD_BATCH), slice(r0 + ODD_BATCH, r0 + pair), slice(r0, r0 + pair)
            h1 = a_ref[lo8, :] * hprev + b_ref[lo8, :]
            hprev = a_ref[hi8, :] * h1 + b_ref[hi8, :]
            y_ref[both, :] = (jnp.concatenate([h1, hprev], axis=0) * _silu(gate_ref[both, :])).astype(BF16)
        return hprev

    def project_out(hf):
        return _dot(y_ref[hf * rows_h:(hf + 1) * rows_h, :], wout_ref[...])

    def finish_half(hf, proj):
        out = _rms_scale(xh[hf] + proj) * gfin_ref[...]
        out_ref[:, hf * steps_h:(hf + 1) * steps_h, :] = jnp.transpose(
            out.reshape(steps_h, ODD_BATCH, D_MODEL), (1, 0, 2))

    load_half(0)
    project_x(0, 0)
    load_half(1)
    gate_pair(0, 0, then_issue=lambda: project_x(0, 1))
    gate_pair(0, 1, then_issue=lambda: project_x(1, 0))
    gate_pair(0, 2, then_issue=lambda: project_x(1, 1))
    gate_pair(0, 3, then_issue=lambda: project_gate(0))
    gate_pair(1, 0, then_issue=lambda: project_gate(1))
    gate_pair(1, 1)
    h_mid = scan_half(0, hcar_ref[...])
    proj = [None, None]

    def issue_out0():
        proj[0] = project_out(0)

    gate_pair(1, 2, then_issue=issue_out0)
    gate_pair(1, 3)
    proj0 = proj[0]
    hcar_ref[...] = scan_half(1, h_mid)
    proj1 = project_out(1)
    finish_half(0, proj0)
    finish_half(1, proj1)
    for k in range(2):
        xs_ref[k, 0:_HIST_ROWS, :] = xs_ref[k, ROW_TILE:ROW_TILE + _HIST_ROWS, :]

    @pl.when(t == nt - 1)
    def _emit_state():
        for k in range(2):
            conv_ref[:, :, k * half:(k + 1) * half] = (
                xs_ref[k, 0:_HIST_ROWS, :].reshape(CONV_W - 1, ODD_BATCH, half))
        lru_ref[...] = hcar_ref[...]


def _odd_layer(x3, hist_t, h0, norm_g, w_in, conv_w, conv_b, w_ra, b_ra, w_ri, b_ri, lam, w_out, g_fin):
    batch, seq, _ = x3.shape
    nt = seq // ODD_STEPS
    nb = batch // ODD_BATCH
    n_hist = CONV_W - 1
    x_spec = pl.BlockSpec((ODD_BATCH, ODD_STEPS, D_MODEL), lambda b, t: (b, t, 0))
    hist_spec = pl.BlockSpec((n_hist, ODD_BATCH, C_WIDTH), lambda b, t: (0, b, 0))
    h_spec = pl.BlockSpec((ODD_BATCH, C_WIDTH), lambda b, t: (b, 0))
    consts = (norm_g, w_in, conv_w, conv_b, w_ra, b_ra, w_ri, b_ri, lam, w_out, g_fin)
    return pl.pallas_call(
        functools.partial(_odd_kernel, nt=nt),
        grid=(nb, nt),
        in_specs=[x_spec, hist_spec, h_spec] + [_const_spec(c.shape) for c in consts],
        out_specs=[x_spec, hist_spec, h_spec],
        out_shape=[jax.ShapeDtypeStruct((batch, seq, D_MODEL), F32),
                   jax.ShapeDtypeStruct((n_hist, batch, C_WIDTH), F32),
                   jax.ShapeDtypeStruct((batch, C_WIDTH), F32)],
        scratch_shapes=[pltpu.VMEM((2, _HIST_ROWS + ROW_TILE, C_WIDTH // 2), F32),
                        pltpu.VMEM((ROW_TILE, C_WIDTH), F32),
                        pltpu.VMEM((ROW_TILE, C_WIDTH), F32),
                        pltpu.VMEM((ROW_TILE, C_WIDTH), F32),
                        pltpu.VMEM((ROW_TILE, C_WIDTH), BF16),
                        pltpu.VMEM((ODD_BATCH, C_WIDTH), F32)],
        compiler_params=pltpu.CompilerParams(dimension_semantics=("arbitrary", "arbitrary"),
                                             vmem_limit_bytes=VMEM_LIMIT),
        name="odd_layer",
    )(x3, hist_t, h0, *consts)


def _bias_table(masked):
    slopes = 2.0 ** (-8.0 * jnp.arange(1, A_HEADS + 1, dtype=F32) / A_HEADS)
    dist = jnp.abs(WIN_ROWS + jnp.arange(CHUNK)[None, :] - jnp.arange(WIN_KEYS)[:, None]).astype(F32)
    alibi = slopes.reshape(A_KV_HEADS, 1, A_GROUP, 1) * dist[None, :, None, :]
    alibi = alibi.reshape(1, A_KV_HEADS, WIN_KEYS, A_GROUP * CHUNK)
    if not masked:
        return alibi
    first_chunk = jnp.arange(WIN_ROWS // CHUNK + 1)[:, None]
    key_pos = (first_chunk - WIN_ROWS // CHUNK) * CHUNK + jnp.arange(WIN_KEYS)[None, :]
    mask = jnp.where(key_pos < 0, F32(1e30), F32(0.0))
    return alibi + mask[:, None, :, None]


def _pad_gate_blocks(w):
    pad = MXU_DIM - C_BLOCK
    even = jnp.pad(w[0::2], ((0, 0), (0, pad), (0, pad)))
    odd = jnp.pad(w[1::2], ((0, 0), (pad, 0), (pad, 0)))
    return jnp.stack([even, odd], axis=1).reshape(C_BLOCKS, MXU_DIM, MXU_DIM).astype(BF16)


def kernel(x_prompt, x_sample, cache_swa_k, cache_swa_v, state_gla, cache_conv, state_lru, norm_even, w_in_even, w_gate_lr, b_gate_lr, sinks, gla_norm, w_out_even, norm_odd, w_in_odd, conv_w, conv_b, w_rg_a, b_rg_a, w_rg_i, b_rg_i, lru_lambda, w_out_odd, norm_final):
    batch, seq, _ = x_prompt.shape
    dbatch, dseq, _ = x_sample.shape
    row = lambda v: v.reshape(1, -1)

    lr0 = _OFF_GATE
    w_e = w_in_even[0]
    w_t = jnp.transpose(w_e)
    w_main = jnp.transpose(jnp.concatenate([w_t[:lr0], w_t[lr0 + B_LOWRANK:]], axis=0).astype(BF16))
    w_lr1 = jnp.pad(w_e[:, lr0:lr0 + B_LOWRANK], ((0, 0), (0, LANES - B_LOWRANK))).astype(BF16)
    w_lr2 = jnp.pad(w_gate_lr[0], ((0, LANES - B_LOWRANK), (0, 0))).astype(BF16)
    w_out_e = w_out_even[0].astype(BF16)
    even_consts = (row(norm_even[0]), w_main, w_lr1, w_lr2, row(b_gate_lr[0]))
    sink_rows = jnp.repeat(sinks[0].reshape(A_KV_HEADS, 1, A_GROUP), CHUNK, axis=2)

    def even_layer(x, past):
        b, tl, _ = x.shape
        x2 = x.reshape(b * tl, D_MODEL)
        q, k, v, kb, vb, qg, kg, kd, bv, glast, sgate = _even_in(x2, *even_consts)
        if past is None:
            kprev, vprev, s0 = kb, vb, None
        else:
            kprev = jnp.transpose(past[0], (0, 2, 3, 1))
            vprev = jnp.transpose(past[1], (0, 2, 3, 1))
            s0 = past[2]
        out, s_new = _even_mix(q, kb, vb, kprev, vprev, qg, kg, kd, bv,
                               glast.reshape(-1, 1, B_QK_WIDTH), sgate, x2,
                               _bias_table(masked=past is None), sink_rows,
                               row(gla_norm[0]), w_out_e, s0, batch=b, seq=tl)
        keep = min(tl, WIN_ROWS)
        k4 = k.reshape(b, tl, A_KV_WIDTH)[:, tl - keep:].reshape(b, keep, A_KV_HEADS, A_HEAD_DIM)
        v4 = v.reshape(b, tl, A_KV_WIDTH)[:, tl - keep:].reshape(b, keep, A_KV_HEADS, A_HEAD_DIM)
        return out, k4, v4, s_new

    xp, pk, pv, pg = even_layer(x_prompt, None)
    xs, sk, sv, sg = even_layer(x_sample, (cache_swa_k[0], cache_swa_v[0], state_gla[0]))

    odd_consts = (row(norm_odd[0]), w_in_odd[0].astype(BF16), conv_w[0], row(conv_b[0]),
                  _pad_gate_blocks(0.5 * w_rg_a[0]), row(0.5 * b_rg_a[0]),
                  _pad_gate_blocks(0.5 * w_rg_i[0]), row(0.5 * b_rg_i[0]),
                  row(lru_lambda[0]), w_out_odd[0].astype(BF16), row(norm_final))
    zero_hist = jnp.zeros((CONV_W - 1, batch, C_WIDTH), F32)
    zero_h = jnp.zeros((batch, C_WIDTH), F32)
    yp, pc, plru = _odd_layer(xp.reshape(batch, seq, D_MODEL), zero_hist, zero_h, *odd_consts)
    ys, sc, slru = _odd_layer(xs.reshape(dbatch, dseq, D_MODEL), jnp.swapaxes(cache_conv[0], 0, 1),
                              state_lru[0], *odd_consts)

    return (yp, ys, pk[None], pv[None], pg[None],
            jnp.swapaxes(pc, 0, 1)[None], plru[None],
            sk[None], sv[None], sg[None], jnp.swapaxes(sc, 0, 1)[None], slru[None])
```

```python
import functools

import jax
import jax.numpy as jnp
from jax import lax
from jax.experimental import pallas as pl
from jax.experimental.pallas import tpu as pltpu

F32 = jnp.float32
BF16 = jnp.bfloat16

D_MODEL = 1024
CHUNK = 64
EPS = 1e-6
A_HEADS = 16
A_KV_HEADS = 4
A_HEAD_DIM = 64
A_GROUP = A_HEADS // A_KV_HEADS
A_WIDTH = A_HEADS * A_HEAD_DIM
A_KV_WIDTH = A_KV_HEADS * A_HEAD_DIM
WIN_ROWS = 128
WIN_KEYS = WIN_ROWS + CHUNK
KV_PAD_WIDTH = A_KV_HEADS * 2 * A_HEAD_DIM
B_HEADS = 4
B_DK = 128
B_DV = 256
B_QK_WIDTH = B_HEADS * B_DK
B_WIDTH = B_HEADS * B_DV
B_LOWRANK = 16
B_GATE_NORM = 16.0
C_WIDTH = 1536
C_BLOCKS = 8
C_BLOCK = C_WIDTH // C_BLOCKS
CONV_W = 4
LRU_C = 8.0
MIX_EVEN = A_WIDTH + B_WIDTH

LANES = 128
SUBLANES = 8
MXU_DIM = 256
ROW_TILE = 512
VMEM_LIMIT = 48 * 1024 * 1024

_OFF_Q = 0
_OFF_K = _OFF_Q + A_WIDTH
_OFF_V = _OFF_K + A_KV_WIDTH
_OFF_BQ = _OFF_V + A_KV_WIDTH
_OFF_BK = _OFF_BQ + B_QK_WIDTH
_OFF_BV = _OFF_BK + B_QK_WIDTH
_OFF_GATE = _OFF_BV + B_WIDTH
_MAIN_WIDTH = _OFF_GATE + MIX_EVEN


def _const_spec(shape):
    nd = len(shape)
    return pl.BlockSpec(shape, lambda *_: (0,) * nd, pipeline_mode=pl.Buffered(1))


def _rms_scale(x):
    return x * lax.rsqrt(jnp.mean(x * x, axis=-1, keepdims=True) + EPS)


def _silu(x):
    half = 0.5 * x
    return half * jnp.tanh(half) + half


def _dot(a, b):
    return jnp.dot(a, b, preferred_element_type=F32)


def _dot_nt(a, b):
    return lax.dot_general(a, b, (((1,), (1,)), ((), ())), preferred_element_type=F32)


def _dot_tn(a, b):
    return lax.dot_general(a, b, (((0,), (0,)), ((), ())), preferred_element_type=F32)


def _even_in_kernel(x_ref, g_ref, w_ref, wlr1_ref, wlr2_ref, blr_ref,
                    q_ref, k_ref, v_ref, kb_ref, vb_ref, qg_ref, kg_ref, kd_ref, bv_ref, glast_ref,
                    sgate_ref):
    h = (_rms_scale(x_ref[...]) * g_ref[...]).astype(BF16)

    def proj(off, width):
        return _dot(h, w_ref[:, off:off + width])

    q_ref[...] = (proj(_OFF_Q, A_WIDTH) * (A_HEAD_DIM ** -0.5)).astype(BF16)
    kf = proj(_OFF_K, A_KV_WIDTH)
    vf = proj(_OFF_V, A_KV_WIDTH)
    k_ref[...] = kf
    v_ref[...] = vf
    n_rows = kf.shape[0]
    zeros = jnp.zeros((n_rows, A_HEAD_DIM), F32)
    one_lane = jnp.where(lax.broadcasted_iota(jnp.int32, (n_rows, A_HEAD_DIM), 1) == 0, 1.0, 0.0)
    heads = [slice(kh * A_HEAD_DIM, (kh + 1) * A_HEAD_DIM) for kh in range(A_KV_HEADS)]
    kb_ref[...] = jnp.concatenate([piece for s in heads for piece in (kf[:, s], zeros)], axis=1).astype(BF16)
    vb_ref[...] = jnp.concatenate([piece for s in heads for piece in (vf[:, s], one_lane)], axis=1).astype(BF16)
    bv_ref[...] = proj(_OFF_BV, B_WIDTH).astype(BF16)
    sgate_ref[...] = _silu(proj(_OFF_GATE, MIX_EVEN)).astype(BF16)

    low = _dot(h, wlr1_ref[...])
    pre = _dot(low.astype(BF16), wlr2_ref[...]) + blr_ref[...]
    glog = (jnp.minimum(pre, 0.0) - jnp.log1p(jnp.exp(-jnp.abs(pre)))) * (1.0 / B_GATE_NORM)
    row_id = lax.broadcasted_iota(jnp.int32, (SUBLANES, B_QK_WIDTH), 0)
    groups = []
    for r in range(ROW_TILE // SUBLANES):
        g8 = glog[r * SUBLANES:(r + 1) * SUBLANES, :]
        for sh in (1, 2, 4):
            g8 = g8 + jnp.where(row_id >= sh, pltpu.roll(g8, sh, axis=0), 0.0)
        if r % (CHUNK // SUBLANES) != 0:
            g8 = g8 + groups[-1][SUBLANES - 1:SUBLANES, :]
        groups.append(g8)
    gcum = jnp.concatenate(groups, axis=0)

    bk = proj(_OFF_BK, B_QK_WIDTH)
    qg_ref[...] = (proj(_OFF_BQ, B_QK_WIDTH) * (B_DK ** -0.5) * jnp.exp(gcum)).astype(BF16)
    kg_ref[...] = (bk * jnp.exp(-gcum)).astype(BF16)
    for c in range(ROW_TILE // CHUNK):
        rows = slice(c * CHUNK, (c + 1) * CHUNK)
        glast = gcum[(c + 1) * CHUNK - 1:(c + 1) * CHUNK, :]
        glast_ref[c:c + 1, :] = glast
        kd_ref[rows, :] = (bk[rows] * jnp.exp(glast - gcum[rows])).astype(BF16)


def _even_in(x2, norm_g, w_main, w_lr1, w_lr2, b_lr):
    rows = x2.shape[0]
    nsteps = rows // ROW_TILE
    row_spec = lambda w: pl.BlockSpec((ROW_TILE, w), lambda i: (i, 0))
    chunks = ROW_TILE // CHUNK
    out_widths = (A_WIDTH, A_KV_WIDTH, A_KV_WIDTH, KV_PAD_WIDTH, KV_PAD_WIDTH, B_QK_WIDTH, B_QK_WIDTH,
                  B_QK_WIDTH, B_WIDTH, B_QK_WIDTH, MIX_EVEN)
    out_dtypes = (BF16, F32, F32, BF16, BF16, BF16, BF16, BF16, BF16, F32, BF16)
    out_rows = [ROW_TILE] * 9 + [chunks, ROW_TILE]
    return pl.pallas_call(
        _even_in_kernel,
        grid=(nsteps,),
        in_specs=[row_spec(D_MODEL), _const_spec(norm_g.shape), _const_spec(w_main.shape),
                  _const_spec(w_lr1.shape), _const_spec(w_lr2.shape), _const_spec(b_lr.shape)],
        out_specs=[pl.BlockSpec((r, w), lambda i: (i, 0)) for r, w in zip(out_rows, out_widths)],
        out_shape=[jax.ShapeDtypeStruct((nsteps * r, w), d)
                   for r, w, d in zip(out_rows, out_widths, out_dtypes)],
        compiler_params=pltpu.CompilerParams(dimension_semantics=("arbitrary",),
                                             vmem_limit_bytes=VMEM_LIMIT),
        name="even_in",
    )(x2, norm_g, w_main, w_lr1, w_lr2, b_lr)


def _even_mix_kernel(*refs, bb, tb, nt, has_past):
    if has_past:
        (q_ref, kc_ref, vc_ref, kp_ref, vp_ref, qg_ref, kg_ref, kd_ref, bv_ref, glast_ref,
         sgate_ref, x_ref, bias_ref, sink_ref, glag_ref, wout_ref, s0_ref,
         out_ref, sfin_ref, kbuf, vbuf, st_ref, y_ref) = refs
    else:
        (q_ref, kc_ref, vc_ref, kp_ref, vp_ref, qg_ref, kg_ref, kd_ref, bv_ref, glast_ref,
         sgate_ref, x_ref, bias_ref, sink_ref, glag_ref, wout_ref,
         out_ref, sfin_ref, kbuf, vbuf, st_ref, y_ref) = refs
        s0_ref = None
    t = pl.program_id(1)
    tc = tb // CHUNK

    one_lane = jnp.where(lax.broadcasted_iota(jnp.int32, (WIN_ROWS, A_HEAD_DIM), 1) == 0,
                         1.0, 0.0).astype(BF16)
    for bi in range(bb):
        for kh in range(A_KV_HEADS):
            slot = slice(kh * 2 * A_HEAD_DIM, (kh + 1) * 2 * A_HEAD_DIM)
            kslot = slice(kh * 2 * A_HEAD_DIM, kh * 2 * A_HEAD_DIM + A_HEAD_DIM)
            if has_past:
                kbuf[bi, kh, 0:WIN_ROWS, :] = kp_ref[bi, kh].T.astype(BF16)
                vbuf[bi, kh, 0:WIN_ROWS, 0:A_HEAD_DIM] = vp_ref[bi, kh].T.astype(BF16)
                vbuf[bi, kh, 0:WIN_ROWS, A_HEAD_DIM:2 * A_HEAD_DIM] = one_lane
            else:
                kbuf[bi, kh, 0:WIN_ROWS, :] = kp_ref[:, kslot]
                vbuf[bi, kh, 0:WIN_ROWS, :] = vp_ref[:, slot]
            kbuf[bi, kh, WIN_ROWS:WIN_ROWS + tb, :] = kc_ref[bi * tb:(bi + 1) * tb, kslot]
            vbuf[bi, kh, WIN_ROWS:WIN_ROWS + tb, :] = vc_ref[bi * tb:(bi + 1) * tb, slot]

    @pl.when(t == 0)
    def _init_state():
        for bi in range(bb):
            for hh in range(B_HEADS):
                if has_past:
                    st_ref[bi, hh] = s0_ref[bi, hh].T
                else:
                    st_ref[bi, hh] = jnp.zeros((B_DV, B_DK), F32)

    tril = (lax.broadcasted_iota(jnp.int32, (CHUNK, CHUNK), 0)
            >= lax.broadcasted_iota(jnp.int32, (CHUNK, CHUNK), 1))
    glag = glag_ref[...]
    tail_rows = 16
    tail_row0 = lax.broadcasted_iota(jnp.int32, (tail_rows, A_GROUP * CHUNK), 0) == 0
    tail_iota = lax.broadcasted_iota(jnp.int32, (tail_rows, 2 * A_HEAD_DIM), 0)
    tail_lane = lax.broadcasted_iota(jnp.int32, (tail_rows, 2 * A_HEAD_DIM), 1)
    v_tail = jnp.where((tail_iota == 0) & (tail_lane == A_HEAD_DIM), 1.0, 0.0).astype(BF16)

    def chunk_body(n, carry):
        bi = n // tc
        c = n % tc
        w0 = pl.multiple_of(c * CHUNK, CHUNK)
        rows = pl.ds(pl.multiple_of(n * CHUNK, CHUNK), CHUNK)

        variant = 0 if has_past else jnp.minimum(t * tc + c, WIN_ROWS // CHUNK)
        def attn_scores(kh):
            q4 = q_ref[rows, kh * A_GROUP * A_HEAD_DIM:(kh + 1) * A_GROUP * A_HEAD_DIM]
            qs = jnp.concatenate([q4[:, g * A_HEAD_DIM:(g + 1) * A_HEAD_DIM] for g in range(A_GROUP)],
                                 axis=0)
            kw = kbuf[bi, kh, pl.ds(w0, WIN_KEYS), :]
            return _dot_nt(kw, qs) - bias_ref[variant, kh]

        def attn_finish(kh, sg):
            cols = slice(kh * A_GROUP * A_HEAD_DIM, (kh + 1) * A_GROUP * A_HEAD_DIM)
            vw = vbuf[bi, kh, pl.ds(w0, WIN_KEYS), :]
            sink = sink_ref[kh]
            m = jnp.maximum(jnp.max(sg, axis=0, keepdims=True), sink)
            p = jnp.exp(sg - m).astype(BF16)
            p_sink = jnp.where(tail_row0, jnp.exp(sink - m), 0.0).astype(BF16)
            oe = _dot_tn(jnp.concatenate([vw, v_tail], axis=0),
                         jnp.concatenate([p, p_sink], axis=0))
            o = jnp.transpose(oe[0:A_HEAD_DIM, :] * (1.0 / oe[A_HEAD_DIM:A_HEAD_DIM + 1, :]))
            ao = jnp.concatenate([o[g * CHUNK:(g + 1) * CHUNK] for g in range(A_GROUP)], axis=1)
            y_ref[rows, cols] = (ao * sgate_ref[rows, cols].astype(F32)).astype(BF16)

        dec = jnp.exp(glast_ref[n])

        def gla_head(hh):
            ks = slice(hh * B_DK, (hh + 1) * B_DK)
            vs = slice(hh * B_DV, (hh + 1) * B_DV)
            qg = qg_ref[rows, ks]
            a = jnp.where(tril, _dot_nt(qg, kg_ref[rows, ks]), 0.0).astype(BF16)
            vh = bv_ref[rows, vs]
            st = st_ref[bi, hh]
            o = _dot(a, vh) + _dot_nt(qg, st.astype(BF16))
            st_ref[bi, hh] = st * dec[:, ks] + _dot_tn(vh, kd_ref[rows, ks])
            bo = _rms_scale(o) * glag
            cols = slice(A_WIDTH + hh * B_DV, A_WIDTH + (hh + 1) * B_DV)
            y_ref[rows, cols] = (bo * sgate_ref[rows, cols].astype(F32)).astype(BF16)

        scores = attn_scores(0)
        for kh in range(A_KV_HEADS):
            nxt = attn_scores(kh + 1) if kh + 1 < A_KV_HEADS else None
            gla_head(kh)
            attn_finish(kh, scores)
            scores = nxt
        return carry

    lax.fori_loop(0, bb * tc, chunk_body, 0, unroll=True)

    out_ref[...] = x_ref[...] + _dot(y_ref[...], wout_ref[...])

    @pl.when(t == nt - 1)
    def _emit_state():
        for bi in range(bb):
            for hh in range(B_HEADS):
                sfin_ref[bi, hh] = st_ref[bi, hh].T


def _even_mix(q, k, v, kprev, vprev, qg, kg, kd, bv, glast, sgate, x2, bias, sink_rows, gla_g, w_out, s0,
              *, batch, seq):
    has_past = s0 is not None
    tile = ROW_TILE // 2 if has_past else ROW_TILE
    tb = min(seq, tile)
    bb = tile // tb
    nt = seq // tb
    nb = batch // bb
    assert bb == 1 or nt == 1
    row_spec = lambda w: pl.BlockSpec((tile, w), lambda b, t: (b * nt + t, 0))
    if has_past:
        prev_spec = pl.BlockSpec((bb, A_KV_HEADS, A_HEAD_DIM, WIN_ROWS), lambda b, t: (b, 0, 0, 0))
    else:
        assert bb == 1
        per_b = seq // WIN_ROWS
        step = tb // WIN_ROWS
        prev_spec = pl.BlockSpec((WIN_ROWS, KV_PAD_WIDTH),
                                 lambda b, t: (b * per_b + jnp.maximum(t * step - 1, 0), 0))
    state_spec = pl.BlockSpec((bb, B_HEADS, B_DK, B_DV), lambda b, t: (b, 0, 0, 0))
    glast_spec = pl.BlockSpec((tile // CHUNK, 1, B_QK_WIDTH), lambda b, t: (b * nt + t, 0, 0))
    in_specs = [row_spec(A_WIDTH), row_spec(KV_PAD_WIDTH), row_spec(KV_PAD_WIDTH), prev_spec, prev_spec,
                row_spec(B_QK_WIDTH), row_spec(B_QK_WIDTH), row_spec(B_QK_WIDTH), row_spec(B_WIDTH),
                glast_spec, row_spec(MIX_EVEN), row_spec(D_MODEL),
                _const_spec(bias.shape), _const_spec(sink_rows.shape), _const_spec(gla_g.shape),
                _const_spec(w_out.shape)]
    args = [q, k, v, kprev, vprev, qg, kg, kd, bv, glast, sgate, x2, bias, sink_rows, gla_g, w_out]
    if has_past:
        in_specs.append(state_spec)
        args.append(s0)
    return pl.pallas_call(
        functools.partial(_even_mix_kernel, bb=bb, tb=tb, nt=nt, has_past=has_past),
        grid=(nb, nt),
        in_specs=in_specs,
        out_specs=[row_spec(D_MODEL), state_spec],
        out_shape=[jax.ShapeDtypeStruct((batch * seq, D_MODEL), F32),
                   jax.ShapeDtypeStruct((batch, B_HEADS, B_DK, B_DV), F32)],
        scratch_shapes=[pltpu.VMEM((bb, A_KV_HEADS, WIN_ROWS + tb, A_HEAD_DIM), BF16),
                        pltpu.VMEM((bb, A_KV_HEADS, WIN_ROWS + tb, 2 * A_HEAD_DIM), BF16),
                        pltpu.VMEM((bb, B_HEADS, B_DV, B_DK), F32),
                        pltpu.VMEM((tile, MIX_EVEN), BF16)],
        compiler_params=pltpu.CompilerParams(dimension_semantics=("arbitrary", "arbitrary"),
                                             vmem_limit_bytes=VMEM_LIMIT),
        name="even_mix",
    )(*args)


_PAIR = 2 * C_BLOCK
ODD_BATCH = SUBLANES
ODD_PART_STEPS = 32
ODD_PART_ROWS = ODD_PART_STEPS * ODD_BATCH
ODD_MAX_PARTS = 4
_HIST_ROWS = (CONV_W - 1) * ODD_BATCH


def _odd_kernel(x_ref, hist_ref, h0_ref, g_ref, win_ref, cw_ref, cb_ref, wra_ref, bra_ref,
                wri_ref, bri_ref, lam_ref, wout_ref, gfin_ref,
                out_ref, conv_ref, lru_ref,
                xs_ref, gate_ref, a_ref, b_ref, y_ref, hcar_ref, *, nt, parts):
    t = pl.program_id(1)
    half = C_WIDTH // 2
    tile_rows = parts * ODD_PART_ROWS

    @pl.when(t == 0)
    def _init_state():
        hist = hist_ref[...].reshape(_HIST_ROWS, C_WIDTH)
        for k in range(2):
            xs_ref[k, 0:_HIST_ROWS, :] = hist[:, k * half:(k + 1) * half]
        hcar_ref[...] = h0_ref[...]

    steps_h = ODD_PART_STEPS
    rows_h = ODD_PART_ROWS
    xh, hh = [None] * parts, [None] * parts

    def load_half(hf):
        x3 = x_ref[:, hf * steps_h:(hf + 1) * steps_h, :]
        xh[hf] = jnp.transpose(x3, (1, 0, 2)).reshape(rows_h, D_MODEL)
        hh[hf] = (_rms_scale(xh[hf]) * g_ref[...]).astype(BF16)

    def project_x(hf, k):
        r0 = _HIST_ROWS + hf * rows_h
        xs_ref[k, r0:r0 + rows_h, :] = _dot(hh[hf], win_ref[:, k * half:(k + 1) * half])

    def project_gate(hf):
        gate_ref[...] = _dot(hh[hf], win_ref[:, C_WIDTH:2 * C_WIDTH])

    soft_lam = lam_ref[...]
    soft_lam = jnp.maximum(-soft_lam, 0.0) + jnp.log1p(jnp.exp(-jnp.abs(soft_lam)))

    def gate_pair(hf, m, then_issue=None):
        c0 = m * _PAIR
        k, ck = divmod(c0, half)
        u = cb_ref[:, c0:c0 + _PAIR]
        for j in range(CONV_W):
            r = j * ODD_BATCH + hf * rows_h
            u = u + xs_ref[k, r:r + rows_h, ck:ck + _PAIR] * cw_ref[j:j + 1, c0:c0 + _PAIR]
        ub = u.astype(BF16)
        lo, hi = ub[:, 0:MXU_DIM], ub[:, LANES:LANES + MXU_DIM]

        def gate_pre(w_ref):
            e = _dot(lo, w_ref[2 * m])
            o = _dot(hi, w_ref[2 * m + 1])
            return jnp.concatenate([e[:, 0:LANES], e[:, LANES:] + o[:, 0:LANES], o[:, LANES:]], axis=1)

        z_r = gate_pre(wra_ref)
        z_i = gate_pre(wri_ref)
        if then_issue is not None:
            then_issue()
        t_r = jnp.tanh(z_r + bra_ref[:, c0:c0 + _PAIR])
        t_i = jnp.tanh(z_i + bri_ref[:, c0:c0 + _PAIR])
        c4 = (0.5 * LRU_C) * soft_lam[:, c0:c0 + _PAIR]
        neg_log_a = c4 * t_r + c4
        a = jnp.exp(-neg_log_a)
        a_ref[:, c0:c0 + _PAIR] = a
        x1 = jnp.tanh(neg_log_a) * (a * a + 1.0)
        root = jnp.where(x1 > 0.0, x1 * lax.rsqrt(x1), 0.0)
        hu = 0.5 * u
        b_ref[:, c0:c0 + _PAIR] = root * (hu * t_i + hu)

    def scan_half(hf, hprev):
        pair = 2 * ODD_BATCH
        for i in range(rows_h // pair):
            r0 = i * pair
            lo8, hi8, both = slice(r0, r0 + ODD_BATCH), slice(r0 + ODD_BATCH, r0 + pair), slice(r0, r0 + pair)
            h1 = a_ref[lo8, :] * hprev + b_ref[lo8, :]
            hprev = a_ref[hi8, :] * h1 + b_ref[hi8, :]
            y_ref[hf % 2, both, :] = (
                jnp.concatenate([h1, hprev], axis=0) * _silu(gate_ref[both, :])).astype(BF16)
        return hprev

    proj = [None] * parts

    def project_out(hf):
        proj[hf] = _dot(y_ref[hf % 2], wout_ref[...])

    def finish_half(hf):
        out = _rms_scale(xh[hf] + proj[hf]) * gfin_ref[...]
        out_ref[:, hf * steps_h:(hf + 1) * steps_h, :] = jnp.transpose(
            out.reshape(steps_h, ODD_BATCH, D_MODEL), (1, 0, 2))

    load_half(0)
    project_x(0, 0)
    hstate = hcar_ref[...]
    for hf in range(parts):
        nxt = hf + 1 < parts
        if nxt:
            load_half(hf + 1)
        gate_pair(hf, 0, then_issue=functools.partial(project_x, hf, 1))
        gate_pair(hf, 1, then_issue=functools.partial(project_x, hf + 1, 0) if nxt else None)
        gate_pair(hf, 2, then_issue=functools.partial(project_gate, hf))
        gate_pair(hf, 3, then_issue=functools.partial(project_out, hf - 1) if hf > 0 else None)
        hstate = scan_half(hf, hstate)
        if hf > 0:
            finish_half(hf - 1)
    hcar_ref[...] = hstate
    project_out(parts - 1)
    finish_half(parts - 1)
    for k in range(2):
        xs_ref[k, 0:_HIST_ROWS, :] = xs_ref[k, tile_rows:tile_rows + _HIST_ROWS, :]

    @pl.when(t == nt - 1)
    def _emit_state():
        for k in range(2):
            conv_ref[:, :, k * half:(k + 1) * half] = (
                xs_ref[k, 0:_HIST_ROWS, :].reshape(CONV_W - 1, ODD_BATCH, half))
        lru_ref[...] = hcar_ref[...]


def _odd_layer(x3, hist_t, h0, norm_g, w_in, conv_w, conv_b, w_ra, b_ra, w_ri, b_ri, lam, w_out, g_fin):
    batch, seq, _ = x3.shape
    parts = min(ODD_MAX_PARTS, seq // ODD_PART_STEPS)
    steps = parts * ODD_PART_STEPS
    nt = seq // steps
    nb = batch // ODD_BATCH
    n_hist = CONV_W - 1
    x_spec = pl.BlockSpec((ODD_BATCH, steps, D_MODEL), lambda b, t: (b, t, 0))
    hist_spec = pl.BlockSpec((n_hist, ODD_BATCH, C_WIDTH), lambda b, t: (0, b, 0))
    h_spec = pl.BlockSpec((ODD_BATCH, C_WIDTH), lambda b, t: (b, 0))
    consts = (norm_g, w_in, conv_w, conv_b, w_ra, b_ra, w_ri, b_ri, lam, w_out, g_fin)
    return pl.pallas_call(
        functools.partial(_odd_kernel, nt=nt, parts=parts),
        grid=(nb, nt),
        in_specs=[x_spec, hist_spec, h_spec] + [_const_spec(c.shape) for c in consts],
        out_specs=[x_spec, hist_spec, h_spec],
        out_shape=[jax.ShapeDtypeStruct((batch, seq, D_MODEL), F32),
                   jax.ShapeDtypeStruct((n_hist, batch, C_WIDTH), F32),
                   jax.ShapeDtypeStruct((batch, C_WIDTH), F32)],
        scratch_shapes=[pltpu.VMEM((2, _HIST_ROWS + parts * ODD_PART_ROWS, C_WIDTH // 2), F32),
                        pltpu.VMEM((ODD_PART_ROWS, C_WIDTH), F32),
                        pltpu.VMEM((ODD_PART_ROWS, C_WIDTH), F32),
                        pltpu.VMEM((ODD_PART_ROWS, C_WIDTH), F32),
                        pltpu.VMEM((2, ODD_PART_ROWS, C_WIDTH), BF16),
                        pltpu.VMEM((ODD_BATCH, C_WIDTH), F32)],
        compiler_params=pltpu.CompilerParams(dimension_semantics=("arbitrary", "arbitrary"),
                                             vmem_limit_bytes=VMEM_LIMIT),
        name="odd_layer",
    )(x3, hist_t, h0, *consts)


def _bias_table(masked):
    slopes = 2.0 ** (-8.0 * jnp.arange(1, A_HEADS + 1, dtype=F32) / A_HEADS)
    dist = jnp.abs(WIN_ROWS + jnp.arange(CHUNK)[None, :] - jnp.arange(WIN_KEYS)[:, None]).astype(F32)
    alibi = slopes.reshape(A_KV_HEADS, 1, A_GROUP, 1) * dist[None, :, None, :]
    alibi = alibi.reshape(1, A_KV_HEADS, WIN_KEYS, A_GROUP * CHUNK)
    if not masked:
        return alibi
    first_chunk = jnp.arange(WIN_ROWS // CHUNK + 1)[:, None]
    key_pos = (first_chunk - WIN_ROWS // CHUNK) * CHUNK + jnp.arange(WIN_KEYS)[None, :]
    mask = jnp.where(key_pos < 0, F32(1e30), F32(0.0))
    return alibi + mask[:, None, :, None]


def _pad_gate_blocks(w):
    pad = MXU_DIM - C_BLOCK
    even = jnp.pad(w[0::2], ((0, 0), (0, pad), (0, pad)))
    odd = jnp.pad(w[1::2], ((0, 0), (pad, 0), (pad, 0)))
    return jnp.stack([even, odd], axis=1).reshape(C_BLOCKS, MXU_DIM, MXU_DIM).astype(BF16)


def kernel(x_prompt, x_sample, cache_swa_k, cache_swa_v, state_gla, cache_conv, state_lru, norm_even, w_in_even, w_gate_lr, b_gate_lr, sinks, gla_norm, w_out_even, norm_odd, w_in_odd, conv_w, conv_b, w_rg_a, b_rg_a, w_rg_i, b_rg_i, lru_lambda, w_out_odd, norm_final):
    batch, seq, _ = x_prompt.shape
    dbatch, dseq, _ = x_sample.shape
    row = lambda v: v.reshape(1, -1)

    lr0 = _OFF_GATE
    w_e = w_in_even[0]
    w_t = jnp.transpose(w_e)
    w_main = jnp.transpose(jnp.concatenate([w_t[:lr0], w_t[lr0 + B_LOWRANK:]], axis=0).astype(BF16))
    w_lr1 = jnp.pad(w_e[:, lr0:lr0 + B_LOWRANK], ((0, 0), (0, LANES - B_LOWRANK))).astype(BF16)
    w_lr2 = jnp.pad(w_gate_lr[0], ((0, LANES - B_LOWRANK), (0, 0))).astype(BF16)
    w_out_e = w_out_even[0].astype(BF16)
    even_consts = (row(norm_even[0]), w_main, w_lr1, w_lr2, row(b_gate_lr[0]))
    sink_rows = jnp.repeat(sinks[0].reshape(A_KV_HEADS, 1, A_GROUP), CHUNK, axis=2)

    def even_layer(x, past):
        b, tl, _ = x.shape
        x2 = x.reshape(b * tl, D_MODEL)
        q, k, v, kb, vb, qg, kg, kd, bv, glast, sgate = _even_in(x2, *even_consts)
        if past is None:
            kprev, vprev, s0 = kb, vb, None
        else:
            kprev = jnp.transpose(past[0], (0, 2, 3, 1))
            vprev = jnp.transpose(past[1], (0, 2, 3, 1))
            s0 = past[2]
        out, s_new = _even_mix(q, kb, vb, kprev, vprev, qg, kg, kd, bv,
                               glast.reshape(-1, 1, B_QK_WIDTH), sgate, x2,
                               _bias_table(masked=past is None), sink_rows,
                               row(gla_norm[0]), w_out_e, s0, batch=b, seq=tl)
        keep = min(tl, WIN_ROWS)
        k4 = k.reshape(b, tl, A_KV_WIDTH)[:, tl - keep:].reshape(b, keep, A_KV_HEADS, A_HEAD_DIM)
        v4 = v.reshape(b, tl, A_KV_WIDTH)[:, tl - keep:].reshape(b, keep, A_KV_HEADS, A_HEAD_DIM)
        return out, k4, v4, s_new

    xp, pk, pv, pg = even_layer(x_prompt, None)
    xs, sk, sv, sg = even_layer(x_sample, (cache_swa_k[0], cache_swa_v[0], state_gla[0]))

    odd_consts = (row(norm_odd[0]), w_in_odd[0].astype(BF16), conv_w[0], row(conv_b[0]),
                  _pad_gate_blocks(0.5 * w_rg_a[0]), row(0.5 * b_rg_a[0]),
                  _pad_gate_blocks(0.5 * w_rg_i[0]), row(0.5 * b_rg_i[0]),
                  row(lru_lambda[0]), w_out_odd[0].astype(BF16), row(norm_final))
    zero_hist = jnp.zeros((CONV_W - 1, batch, C_WIDTH), F32)
    zero_h = jnp.zeros((batch, C_WIDTH), F32)
    yp, pc, plru = _odd_layer(xp.reshape(batch, seq, D_MODEL), zero_hist, zero_h, *odd_consts)
    ys, sc, slru = _odd_layer(xs.reshape(dbatch, dseq, D_MODEL), jnp.swapaxes(cache_conv[0], 0, 1),
                              state_lru[0], *odd_consts)

    return (yp, ys, pk[None], pv[None], pg[None],
            jnp.swapaxes(pc, 0, 1)[None], plru[None],
            sk[None], sv[None], sg[None], jnp.swapaxes(sc, 0, 1)[None], slru[None])
```

```python
import functools

import jax
import jax.numpy as jnp
from jax import lax
from jax.experimental import pallas as pl
from jax.experimental.pallas import tpu as pltpu

F32 = jnp.float32
BF16 = jnp.bfloat16

D_MODEL = 1024
CHUNK = 64
EPS = 1e-6
A_HEADS = 16
A_KV_HEADS = 4
A_HEAD_DIM = 64
A_GROUP = A_HEADS // A_KV_HEADS
A_WIDTH = A_HEADS * A_HEAD_DIM
A_KV_WIDTH = A_KV_HEADS * A_HEAD_DIM
WIN_ROWS = 128
WIN_KEYS = WIN_ROWS + CHUNK
KV_PAD_WIDTH = A_KV_HEADS * 2 * A_HEAD_DIM
B_HEADS = 4
B_DK = 128
B_DV = 256
B_QK_WIDTH = B_HEADS * B_DK
B_WIDTH = B_HEADS * B_DV
B_LOWRANK = 16
B_GATE_NORM = 16.0
C_WIDTH = 1536
C_BLOCKS = 8
C_BLOCK = C_WIDTH // C_BLOCKS
CONV_W = 4
LRU_C = 8.0
MIX_EVEN = A_WIDTH + B_WIDTH

LANES = 128
SUBLANES = 8
MXU_DIM = 256
ROW_TILE = 512
VMEM_LIMIT = 48 * 1024 * 1024

_OFF_Q = 0
_OFF_K = _OFF_Q + A_WIDTH
_OFF_V = _OFF_K + A_KV_WIDTH
_OFF_BQ = _OFF_V + A_KV_WIDTH
_OFF_BK = _OFF_BQ + B_QK_WIDTH
_OFF_BV = _OFF_BK + B_QK_WIDTH
_OFF_GATE = _OFF_BV + B_WIDTH
_MAIN_WIDTH = _OFF_GATE + MIX_EVEN


def _const_spec(shape):
    nd = len(shape)
    return pl.BlockSpec(shape, lambda *_: (0,) * nd, pipeline_mode=pl.Buffered(1))


def _rms_scale(x):
    return x * lax.rsqrt(jnp.mean(x * x, axis=-1, keepdims=True) + EPS)


def _silu(x):
    half = 0.5 * x
    return half * jnp.tanh(half) + half


def _dot(a, b):
    return jnp.dot(a, b, preferred_element_type=F32)


def _dot_nt(a, b):
    return lax.dot_general(a, b, (((1,), (1,)), ((), ())), preferred_element_type=F32)


def _dot_tn(a, b):
    return lax.dot_general(a, b, (((0,), (0,)), ((), ())), preferred_element_type=F32)


def _even_in_kernel(x_ref, g_ref, w_ref, wlr1_ref, wlr2_ref, blr_ref,
                    q_ref, k_ref, v_ref, kb_ref, vb_ref, qg_ref, kg_ref, kd_ref, bv_ref, glast_ref,
                    sgate_ref):
    h = (_rms_scale(x_ref[...]) * g_ref[...]).astype(BF16)

    def proj(off, width):
        return _dot_nt(h, w_ref[off:off + width, :])

    q_ref[...] = (proj(_OFF_Q, A_WIDTH) * (A_HEAD_DIM ** -0.5)).astype(BF16)
    kf = proj(_OFF_K, A_KV_WIDTH)
    vf = proj(_OFF_V, A_KV_WIDTH)
    k_ref[...] = kf
    v_ref[...] = vf
    n_rows = kf.shape[0]
    zeros = jnp.zeros((n_rows, A_HEAD_DIM), F32)
    one_lane = jnp.where(lax.broadcasted_iota(jnp.int32, (n_rows, A_HEAD_DIM), 1) == 0, 1.0, 0.0)
    heads = [slice(kh * A_HEAD_DIM, (kh + 1) * A_HEAD_DIM) for kh in range(A_KV_HEADS)]
    kb_ref[...] = jnp.concatenate([piece for s in heads for piece in (kf[:, s], zeros)], axis=1).astype(BF16)
    vb_ref[...] = jnp.concatenate([piece for s in heads for piece in (vf[:, s], one_lane)], axis=1).astype(BF16)
    bv_ref[...] = proj(_OFF_BV, B_WIDTH).astype(BF16)
    sgate_ref[...] = _silu(proj(_OFF_GATE, MIX_EVEN)).astype(BF16)

    low = _dot(h, wlr1_ref[...])
    pre = _dot(low.astype(BF16), wlr2_ref[...]) + blr_ref[...]
    glog = (jnp.minimum(pre, 0.0) - jnp.log1p(jnp.exp(-jnp.abs(pre)))) * (1.0 / B_GATE_NORM)
    row_id = lax.broadcasted_iota(jnp.int32, (SUBLANES, B_QK_WIDTH), 0)
    groups = []
    for r in range(ROW_TILE // SUBLANES):
        g8 = glog[r * SUBLANES:(r + 1) * SUBLANES, :]
        for sh in (1, 2, 4):
            g8 = g8 + jnp.where(row_id >= sh, pltpu.roll(g8, sh, axis=0), 0.0)
        if r % (CHUNK // SUBLANES) != 0:
            g8 = g8 + groups[-1][SUBLANES - 1:SUBLANES, :]
        groups.append(g8)
    gcum = jnp.concatenate(groups, axis=0)

    bk = proj(_OFF_BK, B_QK_WIDTH)
    qg_ref[...] = (proj(_OFF_BQ, B_QK_WIDTH) * (B_DK ** -0.5) * jnp.exp(gcum)).astype(BF16)
    kg_ref[...] = (bk * jnp.exp(-gcum)).astype(BF16)
    for c in range(ROW_TILE // CHUNK):
        rows = slice(c * CHUNK, (c + 1) * CHUNK)
        glast = gcum[(c + 1) * CHUNK - 1:(c + 1) * CHUNK, :]
        glast_ref[c:c + 1, :] = glast
        kd_ref[rows, :] = (bk[rows] * jnp.exp(glast - gcum[rows])).astype(BF16)


def _even_in(x2, norm_g, w_main, w_lr1, w_lr2, b_lr):
    rows = x2.shape[0]
    nsteps = rows // ROW_TILE
    row_spec = lambda w: pl.BlockSpec((ROW_TILE, w), lambda i: (i, 0))
    chunks = ROW_TILE // CHUNK
    out_widths = (A_WIDTH, A_KV_WIDTH, A_KV_WIDTH, KV_PAD_WIDTH, KV_PAD_WIDTH, B_QK_WIDTH, B_QK_WIDTH,
                  B_QK_WIDTH, B_WIDTH, B_QK_WIDTH, MIX_EVEN)
    out_dtypes = (BF16, F32, F32, BF16, BF16, BF16, BF16, BF16, BF16, F32, BF16)
    out_rows = [ROW_TILE] * 9 + [chunks, ROW_TILE]
    return pl.pallas_call(
        _even_in_kernel,
        grid=(nsteps,),
        in_specs=[row_spec(D_MODEL), _const_spec(norm_g.shape), _const_spec(w_main.shape),
                  _const_spec(w_lr1.shape), _const_spec(w_lr2.shape), _const_spec(b_lr.shape)],
        out_specs=[pl.BlockSpec((r, w), lambda i: (i, 0)) for r, w in zip(out_rows, out_widths)],
        out_shape=[jax.ShapeDtypeStruct((nsteps * r, w), d)
                   for r, w, d in zip(out_rows, out_widths, out_dtypes)],
        compiler_params=pltpu.CompilerParams(dimension_semantics=("arbitrary",),
                                             vmem_limit_bytes=VMEM_LIMIT),
        name="even_in",
    )(x2, norm_g, w_main, w_lr1, w_lr2, b_lr)


def _even_mix_kernel(*refs, bb, tb, nt, has_past):
    if has_past:
        (q_ref, kc_ref, vc_ref, kp_ref, vp_ref, qg_ref, kg_ref, kd_ref, bv_ref, glast_ref,
         sgate_ref, x_ref, bias_ref, sink_ref, glag_ref, wout_ref, s0_ref,
         out_ref, sfin_ref, kbuf, vbuf, st_ref, y_ref) = refs
    else:
        (q_ref, kc_ref, vc_ref, kp_ref, vp_ref, qg_ref, kg_ref, kd_ref, bv_ref, glast_ref,
         sgate_ref, x_ref, bias_ref, sink_ref, glag_ref, wout_ref,
         out_ref, sfin_ref, kbuf, vbuf, st_ref, y_ref) = refs
        s0_ref = None
    t = pl.program_id(1)
    tc = tb // CHUNK

    one_lane = jnp.where(lax.broadcasted_iota(jnp.int32, (WIN_ROWS, A_HEAD_DIM), 1) == 0,
                         1.0, 0.0).astype(BF16)
    for bi in range(bb):
        for kh in range(A_KV_HEADS):
            slot = slice(kh * 2 * A_HEAD_DIM, (kh + 1) * 2 * A_HEAD_DIM)
            kslot = slice(kh * 2 * A_HEAD_DIM, kh * 2 * A_HEAD_DIM + A_HEAD_DIM)
            if has_past:
                kbuf[bi, kh, 0:WIN_ROWS, :] = kp_ref[bi, kh].T.astype(BF16)
                vbuf[bi, kh, 0:WIN_ROWS, 0:A_HEAD_DIM] = vp_ref[bi, kh].T.astype(BF16)
                vbuf[bi, kh, 0:WIN_ROWS, A_HEAD_DIM:2 * A_HEAD_DIM] = one_lane
            else:
                kbuf[bi, kh, 0:WIN_ROWS, :] = kp_ref[:, kslot]
                vbuf[bi, kh, 0:WIN_ROWS, :] = vp_ref[:, slot]
            kbuf[bi, kh, WIN_ROWS:WIN_ROWS + tb, :] = kc_ref[bi * tb:(bi + 1) * tb, kslot]
            vbuf[bi, kh, WIN_ROWS:WIN_ROWS + tb, :] = vc_ref[bi * tb:(bi + 1) * tb, slot]

    @pl.when(t == 0)
    def _init_state():
        for bi in range(bb):
            for hh in range(B_HEADS):
                if has_past:
                    st_ref[bi, hh] = s0_ref[bi, hh].T
                else:
                    st_ref[bi, hh] = jnp.zeros((B_DV, B_DK), F32)

    tril = (lax.broadcasted_iota(jnp.int32, (CHUNK, CHUNK), 0)
            >= lax.broadcasted_iota(jnp.int32, (CHUNK, CHUNK), 1))
    glag = glag_ref[...]
    tail_rows = 16
    tail_row0 = lax.broadcasted_iota(jnp.int32, (tail_rows, A_GROUP * CHUNK), 0) == 0
    tail_iota = lax.broadcasted_iota(jnp.int32, (tail_rows, 2 * A_HEAD_DIM), 0)
    tail_lane = lax.broadcasted_iota(jnp.int32, (tail_rows, 2 * A_HEAD_DIM), 1)
    v_tail = jnp.where((tail_iota == 0) & (tail_lane == A_HEAD_DIM), 1.0, 0.0).astype(BF16)

    def chunk_body(n, carry):
        bi = n // tc
        c = n % tc
        w0 = pl.multiple_of(c * CHUNK, CHUNK)
        rows = pl.ds(pl.multiple_of(n * CHUNK, CHUNK), CHUNK)

        variant = 0 if has_past else jnp.minimum(t * tc + c, WIN_ROWS // CHUNK)
        def attn_scores(kh):
            q4 = q_ref[rows, kh * A_GROUP * A_HEAD_DIM:(kh + 1) * A_GROUP * A_HEAD_DIM]
            qs = jnp.concatenate([q4[:, g * A_HEAD_DIM:(g + 1) * A_HEAD_DIM] for g in range(A_GROUP)],
                                 axis=0)
            kw = kbuf[bi, kh, pl.ds(w0, WIN_KEYS), :]
            return _dot_nt(kw, qs) - bias_ref[variant, kh]

        def attn_finish(kh, sg):
            cols = slice(kh * A_GROUP * A_HEAD_DIM, (kh + 1) * A_GROUP * A_HEAD_DIM)
            vw = vbuf[bi, kh, pl.ds(w0, WIN_KEYS), :]
            sink = sink_ref[kh]
            m = jnp.maximum(jnp.max(sg, axis=0, keepdims=True), sink)
            p = jnp.exp(sg - m).astype(BF16)
            p_sink = jnp.where(tail_row0, jnp.exp(sink - m), 0.0).astype(BF16)
            oe = _dot_tn(jnp.concatenate([vw, v_tail], axis=0),
                         jnp.concatenate([p, p_sink], axis=0))
            o = jnp.transpose(oe[0:A_HEAD_DIM, :] * (1.0 / oe[A_HEAD_DIM:A_HEAD_DIM + 1, :]))
            ao = jnp.concatenate([o[g * CHUNK:(g + 1) * CHUNK] for g in range(A_GROUP)], axis=1)
            y_ref[rows, cols] = (ao * sgate_ref[rows, cols].astype(F32)).astype(BF16)

        dec = jnp.exp(glast_ref[n])

        def gla_head(hh):
            ks = slice(hh * B_DK, (hh + 1) * B_DK)
            vs = slice(hh * B_DV, (hh + 1) * B_DV)
            qg = qg_ref[rows, ks]
            a = jnp.where(tril, _dot_nt(qg, kg_ref[rows, ks]), 0.0).astype(BF16)
            vh = bv_ref[rows, vs]
            st = st_ref[bi, hh]
            o = _dot(a, vh) + _dot_nt(qg, st.astype(BF16))
            st_ref[bi, hh] = st * dec[:, ks] + _dot_tn(vh, kd_ref[rows, ks])
            bo = _rms_scale(o) * glag
            cols = slice(A_WIDTH + hh * B_DV, A_WIDTH + (hh + 1) * B_DV)
            y_ref[rows, cols] = (bo * sgate_ref[rows, cols].astype(F32)).astype(BF16)

        scores = attn_scores(0)
        for kh in range(A_KV_HEADS):
            nxt = attn_scores(kh + 1) if kh + 1 < A_KV_HEADS else None
            gla_head(kh)
            attn_finish(kh, scores)
            scores = nxt
        return carry

    lax.fori_loop(0, bb * tc, chunk_body, 0, unroll=True)

    out_ref[...] = x_ref[...] + _dot(y_ref[...], wout_ref[...])

    @pl.when(t == nt - 1)
    def _emit_state():
        for bi in range(bb):
            for hh in range(B_HEADS):
                sfin_ref[bi, hh] = st_ref[bi, hh].T


def _even_mix(q, k, v, kprev, vprev, qg, kg, kd, bv, glast, sgate, x2, bias, sink_rows, gla_g, w_out, s0,
              *, batch, seq):
    has_past = s0 is not None
    tile = ROW_TILE // 2 if has_past else ROW_TILE
    tb = min(seq, tile)
    bb = tile // tb
    nt = seq // tb
    nb = batch // bb
    assert bb == 1 or nt == 1
    row_spec = lambda w: pl.BlockSpec((tile, w), lambda b, t: (b * nt + t, 0))
    if has_past:
        prev_spec = pl.BlockSpec((bb, A_KV_HEADS, A_HEAD_DIM, WIN_ROWS), lambda b, t: (b, 0, 0, 0))
    else:
        assert bb == 1
        per_b = seq // WIN_ROWS
        step = tb // WIN_ROWS
        prev_spec = pl.BlockSpec((WIN_ROWS, KV_PAD_WIDTH),
                                 lambda b, t: (b * per_b + jnp.maximum(t * step - 1, 0), 0))
    state_spec = pl.BlockSpec((bb, B_HEADS, B_DK, B_DV), lambda b, t: (b, 0, 0, 0))
    glast_spec = pl.BlockSpec((tile // CHUNK, 1, B_QK_WIDTH), lambda b, t: (b * nt + t, 0, 0))
    in_specs = [row_spec(A_WIDTH), row_spec(KV_PAD_WIDTH), row_spec(KV_PAD_WIDTH), prev_spec, prev_spec,
                row_spec(B_QK_WIDTH), row_spec(B_QK_WIDTH), row_spec(B_QK_WIDTH), row_spec(B_WIDTH),
                glast_spec, row_spec(MIX_EVEN), row_spec(D_MODEL),
                _const_spec(bias.shape), _const_spec(sink_rows.shape), _const_spec(gla_g.shape),
                _const_spec(w_out.shape)]
    args = [q, k, v, kprev, vprev, qg, kg, kd, bv, glast, sgate, x2, bias, sink_rows, gla_g, w_out]
    if has_past:
        in_specs.append(state_spec)
        args.append(s0)
    return pl.pallas_call(
        functools.partial(_even_mix_kernel, bb=bb, tb=tb, nt=nt, has_past=has_past),
        grid=(nb, nt),
        in_specs=in_specs,
        out_specs=[row_spec(D_MODEL), state_spec],
        out_shape=[jax.ShapeDtypeStruct((batch * seq, D_MODEL), F32),
                   jax.ShapeDtypeStruct((batch, B_HEADS, B_DK, B_DV), F32)],
        scratch_shapes=[pltpu.VMEM((bb, A_KV_HEADS, WIN_ROWS + tb, A_HEAD_DIM), BF16),
                        pltpu.VMEM((bb, A_KV_HEADS, WIN_ROWS + tb, 2 * A_HEAD_DIM), BF16),
                        pltpu.VMEM((bb, B_HEADS, B_DV, B_DK), F32),
                        pltpu.VMEM((tile, MIX_EVEN), BF16)],
        compiler_params=pltpu.CompilerParams(dimension_semantics=("arbitrary", "arbitrary"),
                                             vmem_limit_bytes=VMEM_LIMIT),
        name="even_mix",
    )(*args)


_PAIR = 2 * C_BLOCK
ODD_BATCH = SUBLANES
ODD_PART_STEPS = 32
ODD_PART_ROWS = ODD_PART_STEPS * ODD_BATCH
ODD_MAX_PARTS = 4
_HIST_ROWS = (CONV_W - 1) * ODD_BATCH


def _odd_kernel(x_ref, hist_ref, h0_ref, g_ref, win_ref, cw_ref, cb_ref, wra_ref, bra_ref,
                wri_ref, bri_ref, lam_ref, wout_ref, gfin_ref,
                out_ref, conv_ref, lru_ref,
                xs_ref, gate_ref, a_ref, b_ref, y_ref, hcar_ref, *, nt, parts):
    t = pl.program_id(1)
    half = C_WIDTH // 2
    tile_rows = parts * ODD_PART_ROWS

    @pl.when(t == 0)
    def _init_state():
        hist = hist_ref[...].reshape(_HIST_ROWS, C_WIDTH)
        for k in range(2):
            xs_ref[k, 0:_HIST_ROWS, :] = hist[:, k * half:(k + 1) * half]
        hcar_ref[...] = h0_ref[...]

    steps_h = ODD_PART_STEPS
    rows_h = ODD_PART_ROWS
    xh, hh = [None] * parts, [None] * parts

    def load_half(hf):
        x3 = x_ref[:, hf * steps_h:(hf + 1) * steps_h, :]
        xh[hf] = jnp.transpose(x3, (1, 0, 2)).reshape(rows_h, D_MODEL)
        hh[hf] = (_rms_scale(xh[hf]) * g_ref[...]).astype(BF16)

    def project_x(hf, k):
        r0 = _HIST_ROWS + hf * rows_h
        xs_ref[k, r0:r0 + rows_h, :] = _dot(hh[hf], win_ref[:, k * half:(k + 1) * half])

    def project_gate(hf):
        gate_ref[...] = _dot(hh[hf], win_ref[:, C_WIDTH:2 * C_WIDTH])

    soft_lam = lam_ref[...]
    soft_lam = jnp.maximum(-soft_lam, 0.0) + jnp.log1p(jnp.exp(-jnp.abs(soft_lam)))

    def gate_pair(hf, m, then_issue=None):
        c0 = m * _PAIR
        k, ck = divmod(c0, half)
        u = cb_ref[:, c0:c0 + _PAIR]
        for j in range(CONV_W):
            r = j * ODD_BATCH + hf * rows_h
            u = u + xs_ref[k, r:r + rows_h, ck:ck + _PAIR] * cw_ref[j:j + 1, c0:c0 + _PAIR]
        ub = u.astype(BF16)
        lo, hi = ub[:, 0:MXU_DIM], ub[:, LANES:LANES + MXU_DIM]

        def gate_pre(w_ref):
            e = _dot(lo, w_ref[2 * m])
            o = _dot(hi, w_ref[2 * m + 1])
            return jnp.concatenate([e[:, 0:LANES], e[:, LANES:] + o[:, 0:LANES], o[:, LANES:]], axis=1)

        z_r = gate_pre(wra_ref)
        z_i = gate_pre(wri_ref)
        if then_issue is not None:
            then_issue()
        t_r = jnp.tanh(z_r + bra_ref[:, c0:c0 + _PAIR])
        t_i = jnp.tanh(z_i + bri_ref[:, c0:c0 + _PAIR])
        c4 = (0.5 * LRU_C) * soft_lam[:, c0:c0 + _PAIR]
        neg_log_a = c4 * t_r + c4
        a = jnp.exp(-neg_log_a)
        a_ref[:, c0:c0 + _PAIR] = a
        x1 = jnp.tanh(neg_log_a) * (a * a + 1.0)
        root = jnp.where(x1 > 0.0, x1 * lax.rsqrt(x1), 0.0)
        hu = 0.5 * u
        b_ref[:, c0:c0 + _PAIR] = root * (hu * t_i + hu)

    def scan_half(hf, hprev):
        pair = 2 * ODD_BATCH
        for i in range(rows_h // pair):
            r0 = i * pair
            lo8, hi8, both = slice(r0, r0 + ODD_BATCH), slice(r0 + ODD_BATCH, r0 + pair), slice(r0, r0 + pair)
            h1 = a_ref[lo8, :] * hprev + b_ref[lo8, :]
            hprev = a_ref[hi8, :] * h1 + b_ref[hi8, :]
            y_ref[hf % 2, both, :] = (
                jnp.concatenate([h1, hprev], axis=0) * _silu(gate_ref[both, :])).astype(BF16)
        return hprev

    proj = [None] * parts

    def project_out(hf):
        proj[hf] = _dot(y_ref[hf % 2], wout_ref[...])

    def finish_half(hf):
        out = _rms_scale(xh[hf] + proj[hf]) * gfin_ref[...]
        out_ref[:, hf * steps_h:(hf + 1) * steps_h, :] = jnp.transpose(
            out.reshape(steps_h, ODD_BATCH, D_MODEL), (1, 0, 2))

    load_half(0)
    project_x(0, 0)
    hstate = hcar_ref[...]
    for hf in range(parts):
        nxt = hf + 1 < parts
        if nxt:
            load_half(hf + 1)
        gate_pair(hf, 0, then_issue=functools.partial(project_x, hf, 1))
        gate_pair(hf, 1, then_issue=functools.partial(project_x, hf + 1, 0) if nxt else None)
        gate_pair(hf, 2, then_issue=functools.partial(project_gate, hf))
        gate_pair(hf, 3, then_issue=functools.partial(project_out, hf - 1) if hf > 0 else None)
        hstate = scan_half(hf, hstate)
        if hf > 0:
            finish_half(hf - 1)
    hcar_ref[...] = hstate
    project_out(parts - 1)
    finish_half(parts - 1)
    for k in range(2):
        xs_ref[k, 0:_HIST_ROWS, :] = xs_ref[k, tile_rows:tile_rows + _HIST_ROWS, :]

    @pl.when(t == nt - 1)
    def _emit_state():
        for k in range(2):
            conv_ref[:, :, k * half:(k + 1) * half] = (
                xs_ref[k, 0:_HIST_ROWS, :].reshape(CONV_W - 1, ODD_BATCH, half))
        lru_ref[...] = hcar_ref[...]


def _odd_layer(x3, hist_t, h0, norm_g, w_in, conv_w, conv_b, w_ra, b_ra, w_ri, b_ri, lam, w_out, g_fin):
    batch, seq, _ = x3.shape
    parts = min(ODD_MAX_PARTS, seq // ODD_PART_STEPS)
    steps = parts * ODD_PART_STEPS
    nt = seq // steps
    nb = batch // ODD_BATCH
    n_hist = CONV_W - 1
    x_spec = pl.BlockSpec((ODD_BATCH, steps, D_MODEL), lambda b, t: (b, t, 0))
    hist_spec = pl.BlockSpec((n_hist, ODD_BATCH, C_WIDTH), lambda b, t: (0, b, 0))
    h_spec = pl.BlockSpec((ODD_BATCH, C_WIDTH), lambda b, t: (b, 0))
    consts = (norm_g, w_in, conv_w, conv_b, w_ra, b_ra, w_ri, b_ri, lam, w_out, g_fin)
    return pl.pallas_call(
        functools.partial(_odd_kernel, nt=nt, parts=parts),
        grid=(nb, nt),
        in_specs=[x_spec, hist_spec, h_spec] + [_const_spec(c.shape) for c in consts],
        out_specs=[x_spec, hist_spec, h_spec],
        out_shape=[jax.ShapeDtypeStruct((batch, seq, D_MODEL), F32),
                   jax.ShapeDtypeStruct((n_hist, batch, C_WIDTH), F32),
                   jax.ShapeDtypeStruct((batch, C_WIDTH), F32)],
        scratch_shapes=[pltpu.VMEM((2, _HIST_ROWS + parts * ODD_PART_ROWS, C_WIDTH // 2), F32),
                        pltpu.VMEM((ODD_PART_ROWS, C_WIDTH), F32),
                        pltpu.VMEM((ODD_PART_ROWS, C_WIDTH), F32),
                        pltpu.VMEM((ODD_PART_ROWS, C_WIDTH), F32),
                        pltpu.VMEM((2, ODD_PART_ROWS, C_WIDTH), BF16),
                        pltpu.VMEM((ODD_BATCH, C_WIDTH), F32)],
        compiler_params=pltpu.CompilerParams(dimension_semantics=("arbitrary", "arbitrary"),
                                             vmem_limit_bytes=VMEM_LIMIT),
        name="odd_layer",
    )(x3, hist_t, h0, *consts)


def _bias_table(masked):
    slopes = 2.0 ** (-8.0 * jnp.arange(1, A_HEADS + 1, dtype=F32) / A_HEADS)
    dist = jnp.abs(WIN_ROWS + jnp.arange(CHUNK)[None, :] - jnp.arange(WIN_KEYS)[:, None]).astype(F32)
    alibi = slopes.reshape(A_KV_HEADS, 1, A_GROUP, 1) * dist[None, :, None, :]
    alibi = alibi.reshape(1, A_KV_HEADS, WIN_KEYS, A_GROUP * CHUNK)
    if not masked:
        return alibi
    first_chunk = jnp.arange(WIN_ROWS // CHUNK + 1)[:, None]
    key_pos = (first_chunk - WIN_ROWS // CHUNK) * CHUNK + jnp.arange(WIN_KEYS)[None, :]
    mask = jnp.where(key_pos < 0, F32(1e30), F32(0.0))
    return alibi + mask[:, None, :, None]


def _pad_gate_blocks(w):
    pad = MXU_DIM - C_BLOCK
    even = jnp.pad(w[0::2], ((0, 0), (0, pad), (0, pad)))
    odd = jnp.pad(w[1::2], ((0, 0), (pad, 0), (pad, 0)))
    return jnp.stack([even, odd], axis=1).reshape(C_BLOCKS, MXU_DIM, MXU_DIM).astype(BF16)


def kernel(x_prompt, x_sample, cache_swa_k, cache_swa_v, state_gla, cache_conv, state_lru, norm_even, w_in_even, w_gate_lr, b_gate_lr, sinks, gla_norm, w_out_even, norm_odd, w_in_odd, conv_w, conv_b, w_rg_a, b_rg_a, w_rg_i, b_rg_i, lru_lambda, w_out_odd, norm_final):
    batch, seq, _ = x_prompt.shape
    dbatch, dseq, _ = x_sample.shape
    row = lambda v: v.reshape(1, -1)

    lr0 = _OFF_GATE
    w_e = w_in_even[0]
    w_t = jnp.transpose(w_e)
    w_main = jnp.concatenate([w_t[:lr0], w_t[lr0 + B_LOWRANK:]], axis=0).astype(BF16)
    w_lr1 = jnp.pad(w_e[:, lr0:lr0 + B_LOWRANK], ((0, 0), (0, LANES - B_LOWRANK))).astype(BF16)
    w_lr2 = jnp.pad(w_gate_lr[0], ((0, LANES - B_LOWRANK), (0, 0))).astype(BF16)
    w_out_e = w_out_even[0].astype(BF16)
    even_consts = (row(norm_even[0]), w_main, w_lr1, w_lr2, row(b_gate_lr[0]))
    sink_rows = jnp.repeat(sinks[0].reshape(A_KV_HEADS, 1, A_GROUP), CHUNK, axis=2)

    def even_layer(x, past):
        b, tl, _ = x.shape
        x2 = x.reshape(b * tl, D_MODEL)
        q, k, v, kb, vb, qg, kg, kd, bv, glast, sgate = _even_in(x2, *even_consts)
        if past is None:
            kprev, vprev, s0 = kb, vb, None
        else:
            kprev = jnp.transpose(past[0], (0, 2, 3, 1))
            vprev = jnp.transpose(past[1], (0, 2, 3, 1))
            s0 = past[2]
        out, s_new = _even_mix(q, kb, vb, kprev, vprev, qg, kg, kd, bv,
                               glast.reshape(-1, 1, B_QK_WIDTH), sgate, x2,
                               _bias_table(masked=past is None), sink_rows,
                               row(gla_norm[0]), w_out_e, s0, batch=b, seq=tl)
        keep = min(tl, WIN_ROWS)
        k4 = k.reshape(b, tl, A_KV_WIDTH)[:, tl - keep:].reshape(b, keep, A_KV_HEADS, A_HEAD_DIM)
        v4 = v.reshape(b, tl, A_KV_WIDTH)[:, tl - keep:].reshape(b, keep, A_KV_HEADS, A_HEAD_DIM)
        return out, k4, v4, s_new

    xp, pk, pv, pg = even_layer(x_prompt, None)
    xs, sk, sv, sg = even_layer(x_sample, (cache_swa_k[0], cache_swa_v[0], state_gla[0]))

    odd_consts = (row(norm_odd[0]), w_in_odd[0].astype(BF16), conv_w[0], row(conv_b[0]),
                  _pad_gate_blocks(0.5 * w_rg_a[0]), row(0.5 * b_rg_a[0]),
                  _pad_gate_blocks(0.5 * w_rg_i[0]), row(0.5 * b_rg_i[0]),
                  row(lru_lambda[0]), w_out_odd[0].astype(BF16), row(norm_final))
    zero_hist = jnp.zeros((CONV_W - 1, batch, C_WIDTH), F32)
    zero_h = jnp.zeros((batch, C_WIDTH), F32)
    yp, pc, plru = _odd_layer(xp.reshape(batch, seq, D_MODEL), zero_hist, zero_h, *odd_consts)
    ys, sc, slru = _odd_layer(xs.reshape(dbatch, dseq, D_MODEL), jnp.swapaxes(cache_conv[0], 0, 1),
                              state_lru[0], *odd_consts)

    return (yp, ys, pk[None], pv[None], pg[None],
            jnp.swapaxes(pc, 0, 1)[None], plru[None],
            sk[None], sv[None], sg[None], jnp.swapaxes(sc, 0, 1)[None], slru[None])
```

```python
import functools

import jax
import jax.numpy as jnp
from jax import lax
from jax.experimental import pallas as pl
from jax.experimental.pallas import tpu as pltpu

F32 = jnp.float32
BF16 = jnp.bfloat16

D_MODEL = 1024
CHUNK = 64
EPS = 1e-6
A_HEADS = 16
A_KV_HEADS = 4
A_HEAD_DIM = 64
A_GROUP = A_HEADS // A_KV_HEADS
A_WIDTH = A_HEADS * A_HEAD_DIM
A_KV_WIDTH = A_KV_HEADS * A_HEAD_DIM
WIN_ROWS = 128
WIN_KEYS = WIN_ROWS + CHUNK
KV_PAD_WIDTH = A_KV_HEADS * 2 * A_HEAD_DIM
B_HEADS = 4
B_DK = 128
B_DV = 256
B_QK_WIDTH = B_HEADS * B_DK
B_WIDTH = B_HEADS * B_DV
B_LOWRANK = 16
B_GATE_NORM = 16.0
C_WIDTH = 1536
C_BLOCKS = 8
C_BLOCK = C_WIDTH // C_BLOCKS
CONV_W = 4
LRU_C = 8.0
MIX_EVEN = A_WIDTH + B_WIDTH

LANES = 128
SUBLANES = 8
MXU_DIM = 256
ROW_TILE = 512
VMEM_LIMIT = 48 * 1024 * 1024

_OFF_Q = 0
_OFF_K = _OFF_Q + A_WIDTH
_OFF_V = _OFF_K + A_KV_WIDTH
_OFF_BQ = _OFF_V + A_KV_WIDTH
_OFF_BK = _OFF_BQ + B_QK_WIDTH
_OFF_BV = _OFF_BK + B_QK_WIDTH
_OFF_GATE = _OFF_BV + B_WIDTH
_MAIN_WIDTH = _OFF_GATE + MIX_EVEN


def _const_spec(shape):
    nd = len(shape)
    return pl.BlockSpec(shape, lambda *_: (0,) * nd, pipeline_mode=pl.Buffered(1))


def _rms_scale(x):
    return x * lax.rsqrt(jnp.mean(x * x, axis=-1, keepdims=True) + EPS)


def _silu(x):
    half = 0.5 * x
    return half * jnp.tanh(half) + half


def _dot(a, b):
    return jnp.dot(a, b, preferred_element_type=F32)


def _dot_nt(a, b):
    return lax.dot_general(a, b, (((1,), (1,)), ((), ())), preferred_element_type=F32)


def _dot_tn(a, b):
    return lax.dot_general(a, b, (((0,), (0,)), ((), ())), preferred_element_type=F32)


def _even_in_kernel(x_ref, g_ref, w_ref, wlr1_ref, wlr2_ref, blr_ref,
                    q_ref, k_ref, v_ref, kb_ref, vb_ref, qg_ref, kg_ref, kd_ref, bv_ref, glast_ref,
                    sgate_ref):
    h = (_rms_scale(x_ref[...]) * g_ref[...]).astype(BF16)

    def proj(off, width):
        return _dot(h, w_ref[:, off:off + width])

    q_ref[...] = (proj(_OFF_Q, A_WIDTH) * (A_HEAD_DIM ** -0.5)).astype(BF16)
    kf = proj(_OFF_K, A_KV_WIDTH)
    vf = proj(_OFF_V, A_KV_WIDTH)
    k_ref[...] = kf
    v_ref[...] = vf
    n_rows = kf.shape[0]
    zeros = jnp.zeros((n_rows, A_HEAD_DIM), F32)
    one_lane = jnp.where(lax.broadcasted_iota(jnp.int32, (n_rows, A_HEAD_DIM), 1) == 0, 1.0, 0.0)
    heads = [slice(kh * A_HEAD_DIM, (kh + 1) * A_HEAD_DIM) for kh in range(A_KV_HEADS)]
    kb_ref[...] = jnp.concatenate([piece for s in heads for piece in (kf[:, s], zeros)], axis=1).astype(BF16)
    vb_ref[...] = jnp.concatenate([piece for s in heads for piece in (vf[:, s], one_lane)], axis=1).astype(BF16)
    bv_ref[...] = proj(_OFF_BV, B_WIDTH).astype(BF16)
    sgate_ref[...] = _silu(proj(_OFF_GATE, MIX_EVEN)).astype(BF16)

    low = _dot(h, wlr1_ref[...])
    pre = _dot(low.astype(BF16), wlr2_ref[...]) + blr_ref[...]
    glog = (jnp.minimum(pre, 0.0) - jnp.log1p(jnp.exp(-jnp.abs(pre)))) * (1.0 / B_GATE_NORM)
    row_id = lax.broadcasted_iota(jnp.int32, (SUBLANES, B_QK_WIDTH), 0)
    groups = []
    for r in range(ROW_TILE // SUBLANES):
        g8 = glog[r * SUBLANES:(r + 1) * SUBLANES, :]
        for sh in (1, 2, 4):
            g8 = g8 + jnp.where(row_id >= sh, pltpu.roll(g8, sh, axis=0), 0.0)
        if r % (CHUNK // SUBLANES) != 0:
            g8 = g8 + groups[-1][SUBLANES - 1:SUBLANES, :]
        groups.append(g8)
    gcum = jnp.concatenate(groups, axis=0)

    bk = proj(_OFF_BK, B_QK_WIDTH)
    qg_ref[...] = (proj(_OFF_BQ, B_QK_WIDTH) * (B_DK ** -0.5) * jnp.exp(gcum)).astype(BF16)
    kg_ref[...] = (bk * jnp.exp(-gcum)).astype(BF16)
    for c in range(ROW_TILE // CHUNK):
        rows = slice(c * CHUNK, (c + 1) * CHUNK)
        glast = gcum[(c + 1) * CHUNK - 1:(c + 1) * CHUNK, :]
        glast_ref[c:c + 1, :] = glast
        kd_ref[rows, :] = (bk[rows] * jnp.exp(glast - gcum[rows])).astype(BF16)


def _even_in(x2, norm_g, w_main, w_lr1, w_lr2, b_lr):
    rows = x2.shape[0]
    nsteps = rows // ROW_TILE
    row_spec = lambda w: pl.BlockSpec((ROW_TILE, w), lambda i: (i, 0))
    chunks = ROW_TILE // CHUNK
    out_widths = (A_WIDTH, A_KV_WIDTH, A_KV_WIDTH, KV_PAD_WIDTH, KV_PAD_WIDTH, B_QK_WIDTH, B_QK_WIDTH,
                  B_QK_WIDTH, B_WIDTH, B_QK_WIDTH, MIX_EVEN)
    out_dtypes = (BF16, F32, F32, BF16, BF16, BF16, BF16, BF16, BF16, F32, BF16)
    out_rows = [ROW_TILE] * 9 + [chunks, ROW_TILE]
    return pl.pallas_call(
        _even_in_kernel,
        grid=(nsteps,),
        in_specs=[row_spec(D_MODEL), _const_spec(norm_g.shape), _const_spec(w_main.shape),
                  _const_spec(w_lr1.shape), _const_spec(w_lr2.shape), _const_spec(b_lr.shape)],
        out_specs=[pl.BlockSpec((r, w), lambda i: (i, 0)) for r, w in zip(out_rows, out_widths)],
        out_shape=[jax.ShapeDtypeStruct((nsteps * r, w), d)
                   for r, w, d in zip(out_rows, out_widths, out_dtypes)],
        compiler_params=pltpu.CompilerParams(dimension_semantics=("arbitrary",),
                                             vmem_limit_bytes=VMEM_LIMIT),
        name="even_in",
    )(x2, norm_g, w_main, w_lr1, w_lr2, b_lr)


def _even_mix_kernel(*refs, bb, tb, nt, has_past):
    if has_past:
        (q_ref, kc_ref, vc_ref, kp_ref, vp_ref, qg_ref, kg_ref, kd_ref, bv_ref, glast_ref,
         sgate_ref, x_ref, bias_ref, sink_ref, glag_ref, wout_ref, s0_ref,
         out_ref, sfin_ref, kbuf, vbuf, st_ref, y_ref) = refs
    else:
        (q_ref, kc_ref, vc_ref, kp_ref, vp_ref, qg_ref, kg_ref, kd_ref, bv_ref, glast_ref,
         sgate_ref, x_ref, bias_ref, sink_ref, glag_ref, wout_ref,
         out_ref, sfin_ref, kbuf, vbuf, st_ref, y_ref) = refs
        s0_ref = None
    t = pl.program_id(1)
    tc = tb // CHUNK

    one_lane = jnp.where(lax.broadcasted_iota(jnp.int32, (WIN_ROWS, A_HEAD_DIM), 1) == 0,
                         1.0, 0.0).astype(BF16)
    for bi in range(bb):
        for kh in range(A_KV_HEADS):
            slot = slice(kh * 2 * A_HEAD_DIM, (kh + 1) * 2 * A_HEAD_DIM)
            kslot = slice(kh * 2 * A_HEAD_DIM, kh * 2 * A_HEAD_DIM + A_HEAD_DIM)
            if has_past:
                kbuf[bi, kh, 0:WIN_ROWS, :] = kp_ref[bi, kh].T.astype(BF16)
                vbuf[bi, kh, 0:WIN_ROWS, 0:A_HEAD_DIM] = vp_ref[bi, kh].T.astype(BF16)
                vbuf[bi, kh, 0:WIN_ROWS, A_HEAD_DIM:2 * A_HEAD_DIM] = one_lane
            else:
                kbuf[bi, kh, 0:WIN_ROWS, :] = kp_ref[:, kslot]
                vbuf[bi, kh, 0:WIN_ROWS, :] = vp_ref[:, slot]
            kbuf[bi, kh, WIN_ROWS:WIN_ROWS + tb, :] = kc_ref[bi * tb:(bi + 1) * tb, kslot]
            vbuf[bi, kh, WIN_ROWS:WIN_ROWS + tb, :] = vc_ref[bi * tb:(bi + 1) * tb, slot]

    @pl.when(t == 0)
    def _init_state():
        for bi in range(bb):
            for hh in range(B_HEADS):
                if has_past:
                    st_ref[bi, hh] = s0_ref[bi, hh].T
                else:
                    st_ref[bi, hh] = jnp.zeros((B_DV, B_DK), F32)

    tril = (lax.broadcasted_iota(jnp.int32, (CHUNK, CHUNK), 0)
            >= lax.broadcasted_iota(jnp.int32, (CHUNK, CHUNK), 1))
    glag = glag_ref[...]
    tail_rows = 16
    tail_row0 = lax.broadcasted_iota(jnp.int32, (tail_rows, A_GROUP * CHUNK), 0) == 0
    tail_iota = lax.broadcasted_iota(jnp.int32, (tail_rows, 2 * A_HEAD_DIM), 0)
    tail_lane = lax.broadcasted_iota(jnp.int32, (tail_rows, 2 * A_HEAD_DIM), 1)
    v_tail = jnp.where((tail_iota == 0) & (tail_lane == A_HEAD_DIM), 1.0, 0.0).astype(BF16)

    def chunk_body(n, carry):
        bi = n // tc
        c = n % tc
        w0 = pl.multiple_of(c * CHUNK, CHUNK)
        rows = pl.ds(pl.multiple_of(n * CHUNK, CHUNK), CHUNK)

        def attn_scores(kh, m=None):
            m = n if m is None else m
            bi_m = m // tc
            c_m = m % tc
            w_m = pl.multiple_of(c_m * CHUNK, CHUNK)
            rows_m = pl.ds(pl.multiple_of(m * CHUNK, CHUNK), CHUNK)
            variant = 0 if has_past else jnp.minimum(t * tc + c_m, WIN_ROWS // CHUNK)
            q4 = q_ref[rows_m, kh * A_GROUP * A_HEAD_DIM:(kh + 1) * A_GROUP * A_HEAD_DIM]
            qs = jnp.concatenate([q4[:, g * A_HEAD_DIM:(g + 1) * A_HEAD_DIM] for g in range(A_GROUP)],
                                 axis=0)
            kw = kbuf[bi_m, kh, pl.ds(w_m, WIN_KEYS), :]
            return _dot_nt(kw, qs) - bias_ref[variant, kh]

        def attn_finish(kh, sg):
            cols = slice(kh * A_GROUP * A_HEAD_DIM, (kh + 1) * A_GROUP * A_HEAD_DIM)
            vw = vbuf[bi, kh, pl.ds(w0, WIN_KEYS), :]
            sink = sink_ref[kh]
            m = jnp.maximum(jnp.max(sg, axis=0, keepdims=True), sink)
            p = jnp.exp(sg - m).astype(BF16)
            p_sink = jnp.where(tail_row0, jnp.exp(sink - m), 0.0).astype(BF16)
            oe = _dot_tn(jnp.concatenate([vw, v_tail], axis=0),
                         jnp.concatenate([p, p_sink], axis=0))
            o = jnp.transpose(oe[0:A_HEAD_DIM, :] * (1.0 / oe[A_HEAD_DIM:A_HEAD_DIM + 1, :]))
            ao = jnp.concatenate([o[g * CHUNK:(g + 1) * CHUNK] for g in range(A_GROUP)], axis=1)
            y_ref[rows, cols] = (ao * sgate_ref[rows, cols].astype(F32)).astype(BF16)

        dec = jnp.exp(glast_ref[n])

        def gla_head(hh):
            ks = slice(hh * B_DK, (hh + 1) * B_DK)
            vs = slice(hh * B_DV, (hh + 1) * B_DV)
            qg = qg_ref[rows, ks]
            a = jnp.where(tril, _dot_nt(qg, kg_ref[rows, ks]), 0.0).astype(BF16)
            vh = bv_ref[rows, vs]
            st = st_ref[bi, hh]
            o = _dot(a, vh) + _dot_nt(qg, st.astype(BF16))
            st_ref[bi, hh] = st * dec[:, ks] + _dot_tn(vh, kd_ref[rows, ks])
            bo = _rms_scale(o) * glag
            cols = slice(A_WIDTH + hh * B_DV, A_WIDTH + (hh + 1) * B_DV)
            y_ref[rows, cols] = (bo * sgate_ref[rows, cols].astype(F32)).astype(BF16)

        scores = carry
        for kh in range(A_KV_HEADS):
            if kh + 1 < A_KV_HEADS:
                nxt = attn_scores(kh + 1)
            else:
                nxt = attn_scores(0, jnp.minimum(n + 1, bb * tc - 1))
            gla_head(kh)
            attn_finish(kh, scores)
            scores = nxt
        return scores

    def first_scores():
        q4 = q_ref[0:CHUNK, 0:A_GROUP * A_HEAD_DIM]
        qs = jnp.concatenate([q4[:, g * A_HEAD_DIM:(g + 1) * A_HEAD_DIM] for g in range(A_GROUP)], axis=0)
        variant = 0 if has_past else jnp.minimum(t * tc, WIN_ROWS // CHUNK)
        return _dot_nt(kbuf[0, 0, 0:WIN_KEYS, :], qs) - bias_ref[variant, 0]

    lax.fori_loop(0, bb * tc, chunk_body, first_scores(), unroll=True)

    out_ref[...] = x_ref[...] + _dot(y_ref[...], wout_ref[...])

    @pl.when(t == nt - 1)
    def _emit_state():
        for bi in range(bb):
            for hh in range(B_HEADS):
                sfin_ref[bi, hh] = st_ref[bi, hh].T


def _even_mix(q, k, v, kprev, vprev, qg, kg, kd, bv, glast, sgate, x2, bias, sink_rows, gla_g, w_out, s0,
              *, batch, seq):
    has_past = s0 is not None
    tile = ROW_TILE // 2 if has_past else ROW_TILE
    tb = min(seq, tile)
    bb = tile // tb
    nt = seq // tb
    nb = batch // bb
    assert bb == 1 or nt == 1
    row_spec = lambda w: pl.BlockSpec((tile, w), lambda b, t: (b * nt + t, 0))
    if has_past:
        prev_spec = pl.BlockSpec((bb, A_KV_HEADS, A_HEAD_DIM, WIN_ROWS), lambda b, t: (b, 0, 0, 0))
    else:
        assert bb == 1
        per_b = seq // WIN_ROWS
        step = tb // WIN_ROWS
        prev_spec = pl.BlockSpec((WIN_ROWS, KV_PAD_WIDTH),
                                 lambda b, t: (b * per_b + jnp.maximum(t * step - 1, 0), 0))
    state_spec = pl.BlockSpec((bb, B_HEADS, B_DK, B_DV), lambda b, t: (b, 0, 0, 0))
    glast_spec = pl.BlockSpec((tile // CHUNK, 1, B_QK_WIDTH), lambda b, t: (b * nt + t, 0, 0))
    in_specs = [row_spec(A_WIDTH), row_spec(KV_PAD_WIDTH), row_spec(KV_PAD_WIDTH), prev_spec, prev_spec,
                row_spec(B_QK_WIDTH), row_spec(B_QK_WIDTH), row_spec(B_QK_WIDTH), row_spec(B_WIDTH),
                glast_spec, row_spec(MIX_EVEN), row_spec(D_MODEL),
                _const_spec(bias.shape), _const_spec(sink_rows.shape), _const_spec(gla_g.shape),
                _const_spec(w_out.shape)]
    args = [q, k, v, kprev, vprev, qg, kg, kd, bv, glast, sgate, x2, bias, sink_rows, gla_g, w_out]
    if has_past:
        in_specs.append(state_spec)
        args.append(s0)
    return pl.pallas_call(
        functools.partial(_even_mix_kernel, bb=bb, tb=tb, nt=nt, has_past=has_past),
        grid=(nb, nt),
        in_specs=in_specs,
        out_specs=[row_spec(D_MODEL), state_spec],
        out_shape=[jax.ShapeDtypeStruct((batch * seq, D_MODEL), F32),
                   jax.ShapeDtypeStruct((batch, B_HEADS, B_DK, B_DV), F32)],
        scratch_shapes=[pltpu.VMEM((bb, A_KV_HEADS, WIN_ROWS + tb, A_HEAD_DIM), BF16),
                        pltpu.VMEM((bb, A_KV_HEADS, WIN_ROWS + tb, 2 * A_HEAD_DIM), BF16),
                        pltpu.VMEM((bb, B_HEADS, B_DV, B_DK), F32),
                        pltpu.VMEM((tile, MIX_EVEN), BF16)],
        compiler_params=pltpu.CompilerParams(dimension_semantics=("arbitrary", "arbitrary"),
                                             vmem_limit_bytes=VMEM_LIMIT),
        name="even_mix",
    )(*args)


_PAIR = 2 * C_BLOCK
ODD_BATCH = SUBLANES
ODD_PART_STEPS = 32
ODD_PART_ROWS = ODD_PART_STEPS * ODD_BATCH
ODD_MAX_PARTS = 4
_HIST_ROWS = (CONV_W - 1) * ODD_BATCH


def _odd_kernel(x_ref, hist_ref, h0_ref, g_ref, win_ref, cw_ref, cb_ref, wra_ref, bra_ref,
                wri_ref, bri_ref, lam_ref, wout_ref, gfin_ref,
                out_ref, conv_ref, lru_ref,
                xs_ref, gate_ref, a_ref, b_ref, y_ref, hcar_ref, *, nt, parts):
    t = pl.program_id(1)
    half = C_WIDTH // 2
    tile_rows = parts * ODD_PART_ROWS

    @pl.when(t == 0)
    def _init_state():
        hist = hist_ref[...].reshape(_HIST_ROWS, C_WIDTH)
        for k in range(2):
            xs_ref[k, 0:_HIST_ROWS, :] = hist[:, k * half:(k + 1) * half]
        hcar_ref[...] = h0_ref[...]

    steps_h = ODD_PART_STEPS
    rows_h = ODD_PART_ROWS
    xh, hh = [None] * parts, [None] * parts

    def load_half(hf):
        x3 = x_ref[:, hf * steps_h:(hf + 1) * steps_h, :]
        xh[hf] = jnp.transpose(x3, (1, 0, 2)).reshape(rows_h, D_MODEL)
        hh[hf] = (_rms_scale(xh[hf]) * g_ref[...]).astype(BF16)

    def project_x(hf, k):
        r0 = _HIST_ROWS + hf * rows_h
        xs_ref[k, r0:r0 + rows_h, :] = _dot(hh[hf], win_ref[:, k * half:(k + 1) * half])

    def project_gate(hf):
        gate_ref[...] = _dot(hh[hf], win_ref[:, C_WIDTH:2 * C_WIDTH])

    soft_lam = lam_ref[...]
    soft_lam = jnp.maximum(-soft_lam, 0.0) + jnp.log1p(jnp.exp(-jnp.abs(soft_lam)))

    def gate_pair(hf, m, then_issue=None):
        c0 = m * _PAIR
        k, ck = divmod(c0, half)
        u = cb_ref[:, c0:c0 + _PAIR]
        for j in range(CONV_W):
            r = j * ODD_BATCH + hf * rows_h
            u = u + xs_ref[k, r:r + rows_h, ck:ck + _PAIR] * cw_ref[j:j + 1, c0:c0 + _PAIR]
        ub = u.astype(BF16)
        lo, hi = ub[:, 0:MXU_DIM], ub[:, LANES:LANES + MXU_DIM]

        def gate_pre(w_ref):
            e = _dot(lo, w_ref[2 * m])
            o = _dot(hi, w_ref[2 * m + 1])
            return jnp.concatenate([e[:, 0:LANES], e[:, LANES:] + o[:, 0:LANES], o[:, LANES:]], axis=1)

        z_r = gate_pre(wra_ref)
        z_i = gate_pre(wri_ref)
        if then_issue is not None:
            then_issue()
        t_r = jnp.tanh(z_r + bra_ref[:, c0:c0 + _PAIR])
        t_i = jnp.tanh(z_i + bri_ref[:, c0:c0 + _PAIR])
        c4 = (0.5 * LRU_C) * soft_lam[:, c0:c0 + _PAIR]
        neg_log_a = c4 * t_r + c4
        a = jnp.exp(-neg_log_a)
        a_ref[:, c0:c0 + _PAIR] = a
        x1 = jnp.tanh(neg_log_a) * (a * a + 1.0)
        root = jnp.where(x1 > 0.0, x1 * lax.rsqrt(x1), 0.0)
        hu = 0.5 * u
        b_ref[:, c0:c0 + _PAIR] = root * (hu * t_i + hu)

    def scan_half(hf, hprev):
        pair = 2 * ODD_BATCH
        for i in range(rows_h // pair):
            r0 = i * pair
            lo8, hi8, both = slice(r0, r0 + ODD_BATCH), slice(r0 + ODD_BATCH, r0 + pair), slice(r0, r0 + pair)
            h1 = a_ref[lo8, :] * hprev + b_ref[lo8, :]
            hprev = a_ref[hi8, :] * h1 + b_ref[hi8, :]
            y_ref[hf % 2, both, :] = (
                jnp.concatenate([h1, hprev], axis=0) * _silu(gate_ref[both, :])).astype(BF16)
        return hprev

    proj = [None] * parts

    def project_out(hf):
        proj[hf] = _dot(y_ref[hf % 2], wout_ref[...])

    def finish_half(hf):
        out = _rms_scale(xh[hf] + proj[hf]) * gfin_ref[...]
        out_ref[:, hf * steps_h:(hf + 1) * steps_h, :] = jnp.transpose(
            out.reshape(steps_h, ODD_BATCH, D_MODEL), (1, 0, 2))

    load_half(0)
    project_x(0, 0)
    hstate = hcar_ref[...]
    for hf in range(parts):
        nxt = hf + 1 < parts
        if nxt:
            load_half(hf + 1)
        gate_pair(hf, 0, then_issue=functools.partial(project_x, hf, 1))
        gate_pair(hf, 1, then_issue=functools.partial(project_x, hf + 1, 0) if nxt else None)
        gate_pair(hf, 2, then_issue=functools.partial(project_gate, hf))
        gate_pair(hf, 3, then_issue=functools.partial(project_out, hf - 1) if hf > 0 else None)
        hstate = scan_half(hf, hstate)
        if hf > 0:
            finish_half(hf - 1)
    hcar_ref[...] = hstate
    project_out(parts - 1)
    finish_half(parts - 1)
    for k in range(2):
        xs_ref[k, 0:_HIST_ROWS, :] = xs_ref[k, tile_rows:tile_rows + _HIST_ROWS, :]

    @pl.when(t == nt - 1)
    def _emit_state():
        for k in range(2):
            conv_ref[:, :, k * half:(k + 1) * half] = (
                xs_ref[k, 0:_HIST_ROWS, :].reshape(CONV_W - 1, ODD_BATCH, half))
        lru_ref[...] = hcar_ref[...]


def _odd_layer(x3, hist_t, h0, norm_g, w_in, conv_w, conv_b, w_ra, b_ra, w_ri, b_ri, lam, w_out, g_fin):
    batch, seq, _ = x3.shape
    parts = min(ODD_MAX_PARTS, seq // ODD_PART_STEPS)
    steps = parts * ODD_PART_STEPS
    nt = seq // steps
    nb = batch // ODD_BATCH
    n_hist = CONV_W - 1
    x_spec = pl.BlockSpec((ODD_BATCH, steps, D_MODEL), lambda b, t: (b, t, 0))
    hist_spec = pl.BlockSpec((n_hist, ODD_BATCH, C_WIDTH), lambda b, t: (0, b, 0))
    h_spec = pl.BlockSpec((ODD_BATCH, C_WIDTH), lambda b, t: (b, 0))
    consts = (norm_g, w_in, conv_w, conv_b, w_ra, b_ra, w_ri, b_ri, lam, w_out, g_fin)
    return pl.pallas_call(
        functools.partial(_odd_kernel, nt=nt, parts=parts),
        grid=(nb, nt),
        in_specs=[x_spec, hist_spec, h_spec] + [_const_spec(c.shape) for c in consts],
        out_specs=[x_spec, hist_spec, h_spec],
        out_shape=[jax.ShapeDtypeStruct((batch, seq, D_MODEL), F32),
                   jax.ShapeDtypeStruct((n_hist, batch, C_WIDTH), F32),
                   jax.ShapeDtypeStruct((batch, C_WIDTH), F32)],
        scratch_shapes=[pltpu.VMEM((2, _HIST_ROWS + parts * ODD_PART_ROWS, C_WIDTH // 2), F32),
                        pltpu.VMEM((ODD_PART_ROWS, C_WIDTH), F32),
                        pltpu.VMEM((ODD_PART_ROWS, C_WIDTH), F32),
                        pltpu.VMEM((ODD_PART_ROWS, C_WIDTH), F32),
                        pltpu.VMEM((2, ODD_PART_ROWS, C_WIDTH), BF16),
                        pltpu.VMEM((ODD_BATCH, C_WIDTH), F32)],
        compiler_params=pltpu.CompilerParams(dimension_semantics=("arbitrary", "arbitrary"),
                                             vmem_limit_bytes=VMEM_LIMIT),
        name="odd_layer",
    )(x3, hist_t, h0, *consts)


def _bias_table(masked):
    slopes = 2.0 ** (-8.0 * jnp.arange(1, A_HEADS + 1, dtype=F32) / A_HEADS)
    dist = jnp.abs(WIN_ROWS + jnp.arange(CHUNK)[None, :] - jnp.arange(WIN_KEYS)[:, None]).astype(F32)
    alibi = slopes.reshape(A_KV_HEADS, 1, A_GROUP, 1) * dist[None, :, None, :]
    alibi = alibi.reshape(1, A_KV_HEADS, WIN_KEYS, A_GROUP * CHUNK)
    if not masked:
        return alibi
    first_chunk = jnp.arange(WIN_ROWS // CHUNK + 1)[:, None]
    key_pos = (first_chunk - WIN_ROWS // CHUNK) * CHUNK + jnp.arange(WIN_KEYS)[None, :]
    mask = jnp.where(key_pos < 0, F32(1e30), F32(0.0))
    return alibi + mask[:, None, :, None]


def _pad_gate_blocks(w):
    pad = MXU_DIM - C_BLOCK
    even = jnp.pad(w[0::2], ((0, 0), (0, pad), (0, pad)))
    odd = jnp.pad(w[1::2], ((0, 0), (pad, 0), (pad, 0)))
    return jnp.stack([even, odd], axis=1).reshape(C_BLOCKS, MXU_DIM, MXU_DIM).astype(BF16)


def kernel(x_prompt, x_sample, cache_swa_k, cache_swa_v, state_gla, cache_conv, state_lru, norm_even, w_in_even, w_gate_lr, b_gate_lr, sinks, gla_norm, w_out_even, norm_odd, w_in_odd, conv_w, conv_b, w_rg_a, b_rg_a, w_rg_i, b_rg_i, lru_lambda, w_out_odd, norm_final):
    batch, seq, _ = x_prompt.shape
    dbatch, dseq, _ = x_sample.shape
    row = lambda v: v.reshape(1, -1)

    lr0 = _OFF_GATE
    w_e = w_in_even[0]
    w_t = jnp.transpose(w_e)
    w_main = jnp.transpose(jnp.concatenate([w_t[:lr0], w_t[lr0 + B_LOWRANK:]], axis=0).astype(BF16))
    w_lr1 = jnp.pad(w_e[:, lr0:lr0 + B_LOWRANK], ((0, 0), (0, LANES - B_LOWRANK))).astype(BF16)
    w_lr2 = jnp.pad(w_gate_lr[0], ((0, LANES - B_LOWRANK), (0, 0))).astype(BF16)
    w_out_e = w_out_even[0].astype(BF16)
    even_consts = (row(norm_even[0]), w_main, w_lr1, w_lr2, row(b_gate_lr[0]))
    sink_rows = jnp.repeat(sinks[0].reshape(A_KV_HEADS, 1, A_GROUP), CHUNK, axis=2)

    def even_layer(x, past):
        b, tl, _ = x.shape
        x2 = x.reshape(b * tl, D_MODEL)
        q, k, v, kb, vb, qg, kg, kd, bv, glast, sgate = _even_in(x2, *even_consts)
        if past is None:
            kprev, vprev, s0 = kb, vb, None
        else:
            kprev = jnp.transpose(past[0], (0, 2, 3, 1))
            vprev = jnp.transpose(past[1], (0, 2, 3, 1))
            s0 = past[2]
        out, s_new = _even_mix(q, kb, vb, kprev, vprev, qg, kg, kd, bv,
                               glast.reshape(-1, 1, B_QK_WIDTH), sgate, x2,
                               _bias_table(masked=past is None), sink_rows,
                               row(gla_norm[0]), w_out_e, s0, batch=b, seq=tl)
        keep = min(tl, WIN_ROWS)
        k4 = k.reshape(b, tl, A_KV_WIDTH)[:, tl - keep:].reshape(b, keep, A_KV_HEADS, A_HEAD_DIM)
        v4 = v.reshape(b, tl, A_KV_WIDTH)[:, tl - keep:].reshape(b, keep, A_KV_HEADS, A_HEAD_DIM)
        return out, k4, v4, s_new

    xp, pk, pv, pg = even_layer(x_prompt, None)
    xs, sk, sv, sg = even_layer(x_sample, (cache_swa_k[0], cache_swa_v[0], state_gla[0]))

    odd_consts = (row(norm_odd[0]), w_in_odd[0].astype(BF16), conv_w[0], row(conv_b[0]),
                  _pad_gate_blocks(0.5 * w_rg_a[0]), row(0.5 * b_rg_a[0]),
                  _pad_gate_blocks(0.5 * w_rg_i[0]), row(0.5 * b_rg_i[0]),
                  row(lru_lambda[0]), w_out_odd[0].astype(BF16), row(norm_final))
    zero_hist = jnp.zeros((CONV_W - 1, batch, C_WIDTH), F32)
    zero_h = jnp.zeros((batch, C_WIDTH), F32)
    yp, pc, plru = _odd_layer(xp.reshape(batch, seq, D_MODEL), zero_hist, zero_h, *odd_consts)
    ys, sc, slru = _odd_layer(xs.reshape(dbatch, dseq, D_MODEL), jnp.swapaxes(cache_conv[0], 0, 1),
                              state_lru[0], *odd_consts)

    return (yp, ys, pk[None], pv[None], pg[None],
            jnp.swapaxes(pc, 0, 1)[None], plru[None],
            sk[None], sv[None], sg[None], jnp.swapaxes(sc, 0, 1)[None], slru[None])
```
